```python
import jax, jax.numpy as jnp
from jax import lax
import numpy as np

D_MODEL = 1024
BATCH = 8
SEQ = 8192
DEPTH = 4

GRID_W = 64
CTX_LEN = 256
D_MIX = D_MODEL
HEAD_DIM = 64
NA_W = D_MIX // 2
NA_HEADS = NA_W // HEAD_DIM
NA_KH = 8
NA_KW = 16
LRU_W = D_MIX // 4
LRU_HEADS = 4
LRU_HD = LRU_W // LRU_HEADS
LRU_CONV = 4
LRU_C = 8.0
FNO_W = D_MIX - NA_W - LRU_W
FNO_GROUPS = 4
FNO_GD = FNO_W // FNO_GROUPS
IN_COLS = 3 * NA_W + 2 * LRU_W + FNO_W
D_FF = ((8 * D_MODEL // 3 + 255) // 256) * 256
N_SUB = 3
N_MOD = 3 * N_SUB
MACARON = 0.5
ALPHA = (2 * DEPTH) ** 0.25
BETA = (8 * DEPTH) ** -0.25
LN_EPS = 1e-5

kernel_name = "hymba_style_na_rglru_fourier_macaron_deepnorm_dit"


def layer_norm(z, g, b):
    mu = jnp.mean(z, axis=-1, keepdims=True)
    var = jnp.mean(jnp.square(z - mu), axis=-1, keepdims=True)
    return (z - mu) * lax.rsqrt(var + LN_EPS) * g + b


def residual_post_norm(x, y, g, b):
    z = ALPHA * x.astype(jnp.float32) + y.astype(jnp.float32)
    return layer_norm(z, g.astype(jnp.float32), b.astype(jnp.float32)).astype(x.dtype)


def modulate(xs, m, j):
    return xs * (1 + m[3 * j + 1]) + m[3 * j]


def swiglu(h, wg, wu, wd):
    return (jax.nn.silu(h @ wg) * (h @ wu)) @ wd


def ffn_sublayer(xs, m, j, wg, wu, wd, g, b):
    y = MACARON * swiglu(modulate(xs, m, j), wg, wu, wd)
    return residual_post_norm(xs, m[3 * j + 2] * y, g, b)


def split_heads(t):
    bsz, n, w = t.shape
    return t.reshape(bsz, n, w // HEAD_DIM, HEAD_DIM).transpose(0, 2, 1, 3)


def merge_heads(t):
    bsz, h, n, d = t.shape
    return t.transpose(0, 2, 1, 3).reshape(bsz, n, h * d)


def context_attention(qc, kc, vc):
    s = jnp.einsum('bhqd,bhkd->bhqk', qc, kc).astype(jnp.float32)
    p = jax.nn.softmax(s, axis=-1).astype(vc.dtype)
    return jnp.einsum('bhqk,bhkd->bhqd', p, vc)


def neighbourhood_attention(q, k, v, kc, vc, rpb):
    bsz, nh, s_len, hd = q.shape
    rows = s_len // GRID_W
    kh = min(NA_KH, rows)
    qg = q.reshape(bsz, nh, rows, GRID_W, hd)
    kg = k.reshape(bsz, nh, rows, GRID_W, hd)
    vg = v.reshape(bsz, nh, rows, GRID_W, hd)
    col0 = np.clip(np.arange(GRID_W) - NA_KW // 2, 0, GRID_W - NA_KW)
    cols = col0[:, None] + np.arange(NA_KW)[None, :]
    dc_idx = cols - np.arange(GRID_W)[:, None] + (NA_KW - 1)
    n_loc = kh * NA_KW

    def row_fn(r):
        rs = jnp.clip(r - kh // 2, 0, rows - kh)
        kb = lax.dynamic_slice_in_dim(kg, rs, kh, axis=2)
        vb = lax.dynamic_slice_in_dim(vg, rs, kh, axis=2)
        kw = kb[:, :, :, cols, :]
        vw = vb[:, :, :, cols, :]
        qr = lax.dynamic_index_in_dim(qg, r, axis=2, keepdims=False)
        s_loc = jnp.einsum('bhqd,bhiqjd->bhqij', qr, kw).astype(jnp.float32)
        dr_idx = rs + jnp.arange(kh) - r + (NA_KH - 1)
        bias = rpb[:, dr_idx][:, :, dc_idx].transpose(0, 2, 1, 3)
        s_loc = s_loc + bias[None].astype(jnp.float32)
        s_ctx = jnp.einsum('bhqd,bhkd->bhqk', qr, kc).astype(jnp.float32)
        s = jnp.concatenate([s_loc.reshape(bsz, nh, GRID_W, n_loc), s_ctx], axis=-1)
        p = jax.nn.softmax(s, axis=-1)
        p_loc = p[..., :n_loc].reshape(bsz, nh, GRID_W, kh, NA_KW).astype(vw.dtype)
        p_ctx = p[..., n_loc:].astype(vc.dtype)
        return (jnp.einsum('bhqij,bhiqjd->bhqd', p_loc, vw)
                + jnp.einsum('bhqk,bhkd->bhqd', p_ctx, vc))

    out = lax.map(row_fn, jnp.arange(rows))
    return out.transpose(1, 2, 0, 3, 4).reshape(bsz, nh, s_len, hd)


def conv_centred(x, w, b):
    n = x.shape[1]
    left = LRU_CONV // 2
    xp = jnp.pad(x, ((0, 0), (left, LRU_CONV - 1 - left), (0, 0)))
    y = xp[:, 0:n] * w[0]
    for t in range(1, LRU_CONV):
        y = y + xp[:, t:t + n] * w[t]
    return y + b


def rglru_coeffs(x, wa, ba, wx, bx, lam):
    bsz, n, _ = x.shape
    xb = x.reshape(bsz, n, LRU_HEADS, LRU_HD)
    r = jax.nn.sigmoid(jnp.einsum('bnhc,hcd->bnhd', xb, wa).reshape(bsz, n, LRU_W) + ba)
    i = jax.nn.sigmoid(jnp.einsum('bnhc,hcd->bnhd', xb, wx).reshape(bsz, n, LRU_W) + bx)
    log_a = -LRU_C * r * jax.nn.softplus(-lam)
    a = jnp.exp(log_a)
    bterm = jnp.sqrt(-jnp.expm1(2.0 * log_a)) * (i * x)
    return a, bterm


def linear_scan(a, b, h0, reverse):
    def combine(e1, e2):
        a1, b1 = e1
        a2, b2 = e2
        return a1 * a2, a2 * b1 + b2
    a_cum, b_cum = lax.associative_scan(combine, (a, b), axis=1, reverse=reverse)
    return b_cum + a_cum * h0[:, None, :]


def rglru_bidirectional(x_lat, x_ctx, wa, ba, wx, bx, lam):
    outs_lat, outs_ctx = [], []
    for d, reverse in enumerate((False, True)):
        f32 = jnp.float32
        pa = (wa[d].astype(f32), ba[d].astype(f32), wx[d].astype(f32), bx[d].astype(f32), lam[d].astype(f32))
        a_c, b_c = rglru_coeffs(x_ctx, *pa)
        h_c = linear_scan(a_c, b_c, jnp.zeros((x_ctx.shape[0], LRU_W), f32), reverse)
        h_end = h_c[:, 0] if reverse else h_c[:, -1]
        a_l, b_l = rglru_coeffs(x_lat, *pa)
        outs_lat.append(linear_scan(a_l, b_l, h_end, reverse))
        outs_ctx.append(h_c)
    return outs_lat[0] + outs_lat[1], outs_ctx[0] + outs_ctx[1]


def fourier_mix(f, fw):
    bsz, n, _ = f.shape
    fb = f.astype(jnp.float32).reshape(bsz, n, FNO_GROUPS, FNO_GD)
    y = jnp.fft.fft2(fb, axes=(1, 3), norm='ortho').real
    y = jnp.einsum('bngc,gcd->bngd', y, fw.astype(jnp.float32))
    return y.reshape(bsz, n, FNO_W)


def hybrid_mixer(h_lat, h_ctx, w_in, w_out, rpb, conv_w, conv_b, wa, ba, wx, bx, lam, fw, need_ctx):
    dt = h_lat.dtype
    cuts = [NA_W, 2 * NA_W, 3 * NA_W, 3 * NA_W + LRU_W, 3 * NA_W + 2 * LRU_W]
    q, k, v, xr, gr, f = jnp.split(h_lat @ w_in, cuts, axis=-1)
    qc, kc, vc, xrc, grc, fc = jnp.split(h_ctx @ w_in, cuts, axis=-1)
    scale = HEAD_DIM ** -0.5
    kch, vch = split_heads(kc), split_heads(vc)
    na_lat = merge_heads(neighbourhood_attention(split_heads(q) * scale, split_heads(k), split_heads(v), kch, vch, rpb))
    xl = conv_centred(xr, conv_w, conv_b).astype(jnp.float32)
    xc = conv_centred(xrc, conv_w, conv_b).astype(jnp.float32)
    hl, hc = rglru_bidirectional(xl, xc, wa, ba, wx, bx, lam)
    lru_lat = (hl * jax.nn.gelu(gr.astype(jnp.float32))).astype(dt)
    fno_lat = fourier_mix(f, fw).astype(dt)
    y_lat = jnp.concatenate([na_lat, lru_lat, fno_lat], axis=-1) @ w_out
    if not need_ctx:
        return y_lat, None
    na_ctx = merge_heads(context_attention(split_heads(qc) * scale, kch, vch))
    lru_ctx = (hc * jax.nn.gelu(grc.astype(jnp.float32))).astype(dt)
    fno_ctx = fourier_mix(fc, fw).astype(dt)
    y_ctx = jnp.concatenate([na_ctx, lru_ctx, fno_ctx], axis=-1) @ w_out
    return y_lat, y_ctx


def setup_inputs(seed: int = 0) -> dict:
    key = jax.random.key(seed)
    ks = jax.random.split(key, 32)
    f32 = jnp.float32

    def nrm(k, shape, s):
        return s * jax.random.normal(k, shape, f32)

    a0 = jax.random.uniform(ks[22], (DEPTH, 2, LRU_W), f32, 0.9, 0.999)
    s0 = a0 ** (1.0 / LRU_C)
    lru_lambda = jnp.log(s0) - jnp.log1p(-s0)
    return {
        "x": nrm(ks[0], (BATCH, SEQ, D_MODEL), 1.0),
        "c": nrm(ks[1], (BATCH, D_MODEL), 1.0),
        "ctx": nrm(ks[2], (BATCH, CTX_LEN, D_MODEL), 1.0),
        "c_ctx": nrm(ks[3], (D_MODEL,), 1.0),
        "w_ada": nrm(ks[4], (DEPTH, D_MODEL, N_MOD * D_MODEL), 0.5 * D_MODEL ** -0.5),
        "b_ada": nrm(ks[5], (DEPTH, N_MOD * D_MODEL), 0.01),
        "ln_g": 1.0 + nrm(ks[6], (DEPTH, N_SUB, D_MODEL), 0.02),
        "ln_b": nrm(ks[7], (DEPTH, N_SUB, D_MODEL), 0.02),
        "ff1_gate": nrm(ks[8], (DEPTH, D_MODEL, D_FF), D_MODEL ** -0.5),
        "ff1_up": nrm(ks[9], (DEPTH, D_MODEL, D_FF), D_MODEL ** -0.5),
        "ff1_down": nrm(ks[10], (DEPTH, D_FF, D_MODEL), BETA * D_FF ** -0.5),
        "ff2_gate": nrm(ks[11], (DEPTH, D_MODEL, D_FF), D_MODEL ** -0.5),
        "ff2_up": nrm(ks[12], (DEPTH, D_MODEL, D_FF), D_MODEL ** -0.5),
        "ff2_down": nrm(ks[13], (DEPTH, D_FF, D_MODEL), BETA * D_FF ** -0.5),
        "w_in": nrm(ks[14], (DEPTH, D_MODEL, IN_COLS), D_MODEL ** -0.5),
        "w_out": nrm(ks[15], (DEPTH, D_MIX, D_MODEL), BETA * D_MIX ** -0.5),
        "na_rpb": nrm(ks[16], (DEPTH, NA_HEADS, 2 * NA_KH - 1, 2 * NA_KW - 1), 0.02),
        "lru_conv_w": nrm(ks[17], (DEPTH, LRU_CONV, LRU_W), LRU_CONV ** -0.5),
        "lru_conv_b": nrm(ks[18], (DEPTH, LRU_W), 0.01),
        "lru_wa": nrm(ks[19], (DEPTH, 2, LRU_HEADS, LRU_HD, LRU_HD), LRU_HD ** -0.5),
        "lru_ba": nrm(ks[20], (DEPTH, 2, LRU_W), 0.01),
        "lru_wx": nrm(ks[21], (DEPTH, 2, LRU_HEADS, LRU_HD, LRU_HD), LRU_HD ** -0.5),
        "lru_bx": nrm(ks[23], (DEPTH, 2, LRU_W), 0.01),
        "lru_lambda": lru_lambda,
        "fno_w": nrm(ks[24], (DEPTH, FNO_GROUPS, FNO_GD, FNO_GD), FNO_GD ** -0.5),
    }


def reference(x, c, ctx, c_ctx, w_ada, b_ada, ln_g, ln_b, ff1_gate, ff1_up, ff1_down, ff2_gate, ff2_up, ff2_down,
              w_in, w_out, na_rpb, lru_conv_w, lru_conv_b, lru_wa, lru_ba, lru_wx, lru_bx, lru_lambda, fno_w):
    bsz = x.shape[0]
    x_lat, x_ctx = x, ctx
    for l in range(DEPTH):
        last = l == DEPTH - 1
        m_lat = (jax.nn.silu(c) @ w_ada[l] + b_ada[l]).reshape(bsz, N_MOD, D_MODEL).transpose(1, 0, 2)[:, :, None, :]
        m_ctx = (jax.nn.silu(c_ctx) @ w_ada[l] + b_ada[l]).reshape(N_MOD, D_MODEL)
        x_lat = ffn_sublayer(x_lat, m_lat, 0, ff1_gate[l], ff1_up[l], ff1_down[l], ln_g[l, 0], ln_b[l, 0])
        x_ctx = ffn_sublayer(x_ctx, m_ctx, 0, ff1_gate[l], ff1_up[l], ff1_down[l], ln_g[l, 0], ln_b[l, 0])
        y_lat, y_ctx = hybrid_mixer(modulate(x_lat, m_lat, 1), modulate(x_ctx, m_ctx, 1),
                                    w_in[l], w_out[l], na_rpb[l], lru_conv_w[l], lru_conv_b[l],
                                    lru_wa[l], lru_ba[l], lru_wx[l], lru_bx[l], lru_lambda[l], fno_w[l],
                                    not last)
        x_lat = residual_post_norm(x_lat, m_lat[5] * y_lat, ln_g[l, 1], ln_b[l, 1])
        x_lat = ffn_sublayer(x_lat, m_lat, 2, ff2_gate[l], ff2_up[l], ff2_down[l], ln_g[l, 2], ln_b[l, 2])
        if not last:
            x_ctx = residual_post_norm(x_ctx, m_ctx[5] * y_ctx, ln_g[l, 1], ln_b[l, 1])
            x_ctx = ffn_sublayer(x_ctx, m_ctx, 2, ff2_gate[l], ff2_up[l], ff2_down[l], ln_g[l, 2], ln_b[l, 2])
    return x_lat
```

```python
import functools
import math

import numpy as np
import jax
import jax.numpy as jnp
from jax import lax
from jax.experimental import pallas as pl
from jax.experimental.pallas import tpu as pltpu

F32 = jnp.float32
BF16 = jnp.bfloat16
HIGHEST = lax.Precision.HIGHEST

HEAD_DIM = 64
GRID_W = 64
NA_KH = 8
NA_KW = 16
LRU_HEADS = 4
LRU_CONV = 4
LRU_C = 8.0
FNO_GROUPS = 4
N_MOD = 9
MACARON = 0.5
LN_EPS = 1e-5
MASK_BIAS = -1e30
SUBLANES = 8
FNO_N1 = 64
VMEM_LIMIT = 56 * 1024 * 1024


def _cparams(sem):
    return pltpu.CompilerParams(dimension_semantics=sem, vmem_limit_bytes=VMEM_LIMIT)


def _layer_norm(z, g, b):
    mu = jnp.mean(z, axis=-1, keepdims=True)
    zc = z - mu
    var = jnp.mean(zc * zc, axis=-1, keepdims=True)
    return zc * lax.rsqrt(var + LN_EPS) * g + b


def _ada_kernel(c_ref, w_ref, b_ref, o_ref):
    c = c_ref[...]
    a = c * jax.nn.sigmoid(c)
    o_ref[...] = jnp.dot(a, w_ref[...], preferred_element_type=F32, precision=HIGHEST) + b_ref[...]


def _ada_params(cc, w_ada, b_ada):
    L, D, ND = w_ada.shape
    tn = D
    out = pl.pallas_call(
        _ada_kernel,
        grid=(L, ND // tn),
        in_specs=[pl.BlockSpec((16, D), lambda l, n: (0, 0)),
                  pl.BlockSpec((None, D, tn), lambda l, n: (l, 0, n)),
                  pl.BlockSpec((None, 1, tn), lambda l, n: (l, 0, n))],
        out_specs=pl.BlockSpec((None, 16, tn), lambda l, n: (l, 0, n)),
        out_shape=jax.ShapeDtypeStruct((L, 16, ND), F32),
        compiler_params=_cparams(("parallel", "parallel")),
        name="ada",
    )(cc, w_ada, b_ada.reshape(L, 1, ND))
    return out.reshape(L, 16, N_MOD, D)


def _ffn_kernel(*refs, j, tf, alpha, nlat, two_x):
    if two_x:
        xl_ref, xc_ref, m_ref, wg_ref, wu_ref, wd_ref, g_ref, b_ref, o_ref, h_ref, acc_ref = refs
        x = jnp.where(pl.program_id(0) < nlat, xl_ref[...], xc_ref[...])
    else:
        x_ref, m_ref, wg_ref, wu_ref, wd_ref, g_ref, b_ref, o_ref, h_ref, acc_ref = refs
        x = x_ref[...]
    m = m_ref[...]
    shift, scale, gate = m[3 * j:3 * j + 1], m[3 * j + 1:3 * j + 2], m[3 * j + 2:3 * j + 3]
    h_ref[...] = (x * (1.0 + scale) + shift).astype(BF16)
    acc_ref[...] = jnp.zeros_like(acc_ref)
    nchunks = wg_ref.shape[1] // tf

    def body(c, carry):
        off = pl.multiple_of(c * tf, tf)
        h = h_ref[...]
        g = jnp.dot(h, wg_ref[:, pl.ds(off, tf)], preferred_element_type=F32)
        u = jnp.dot(h, wu_ref[:, pl.ds(off, tf)], preferred_element_type=F32)
        a = (g * jax.nn.sigmoid(g) * u).astype(BF16)
        acc_ref[...] += jnp.dot(a, wd_ref[pl.ds(off, tf), :], preferred_element_type=F32)
        return carry

    lax.fori_loop(0, nchunks, body, 0)
    z = alpha * x + (MACARON * gate) * acc_ref[...]
    o_ref[...] = _layer_norm(z, g_ref[...], b_ref[...])


def _ffn(xs, mods_l, wg, wu, wd, ln_g, ln_b, *, j, l, alpha, tm, tiles_per_batch, nlat, ntiles, nbatch):
    two_x = len(xs) == 2
    D = xs[0].shape[1]
    Fd = wg.shape[2]
    tf = 256
    row_idx = lambda i: jnp.minimum(i // tiles_per_batch, nbatch)
    if two_x:
        nctx = ntiles - nlat
        x_specs = [pl.BlockSpec((tm, D), lambda i: (jnp.minimum(i, nlat - 1), 0)),
                   pl.BlockSpec((tm, D), lambda i: (jnp.clip(i - nlat, 0, nctx - 1), 0))]
    else:
        x_specs = [pl.BlockSpec((tm, D), lambda i: (i, 0))]
    const = pl.Buffered(1)
    return pl.pallas_call(
        functools.partial(_ffn_kernel, j=j, tf=tf, alpha=alpha, nlat=nlat, two_x=two_x),
        grid=(ntiles,),
        in_specs=x_specs + [
            pl.BlockSpec((None, None, N_MOD, D), lambda i: (l, row_idx(i), 0, 0)),
            pl.BlockSpec((None, D, Fd), lambda i: (l, 0, 0), pipeline_mode=const),
            pl.BlockSpec((None, D, Fd), lambda i: (l, 0, 0), pipeline_mode=const),
            pl.BlockSpec((None, Fd, D), lambda i: (l, 0, 0), pipeline_mode=const),
            pl.BlockSpec((None, 1, D), lambda i: (l, 0, 0)),
            pl.BlockSpec((None, 1, D), lambda i: (l, 0, 0))],
        out_specs=pl.BlockSpec((tm, D), lambda i: (i, 0)),
        out_shape=jax.ShapeDtypeStruct((ntiles * tm, D), F32),
        scratch_shapes=[pltpu.VMEM((tm, D), BF16), pltpu.VMEM((tm, D), F32)],
        compiler_params=_cparams(("parallel",)),
        name=f"ffn{j}",
    )(*xs, mods_l, wg, wu, wd, ln_g, ln_b)


def _inproj_kernel(x_ref, m_ref, w_ref, q_ref, k_ref, v_ref, xr_ref, gr_ref, f_ref, *, na_w, lru_w):
    m = m_ref[...]
    h = (x_ref[...] * (1.0 + m[4:5]) + m[3:4]).astype(BF16)

    def proj(lo, width):
        return jnp.dot(h, w_ref[:, lo:lo + width], preferred_element_type=F32)

    q_ref[...] = (proj(0, na_w) * (HEAD_DIM ** -0.5)).astype(BF16)
    k_ref[...] = proj(na_w, na_w).astype(BF16)
    v_ref[...] = proj(2 * na_w, na_w).astype(BF16)
    xr_ref[...] = proj(3 * na_w, lru_w)
    gr_ref[...] = proj(3 * na_w + lru_w, lru_w)
    f_ref[...] = proj(3 * na_w + 2 * lru_w, w_ref.shape[1] - 3 * na_w - 2 * lru_w)


def _inproj(x_all, mods_l, w_in, *, l, tm, tiles_per_batch, nbatch, na_w, lru_w):
    T, D = x_all.shape
    cols = w_in.shape[2]
    fno_w = cols - 3 * na_w - 2 * lru_w
    row_idx = lambda i: jnp.minimum(i // tiles_per_batch, nbatch)
    widths = [(na_w, BF16)] * 3 + [(lru_w, F32)] * 2 + [(fno_w, F32)]
    return pl.pallas_call(
        functools.partial(_inproj_kernel, na_w=na_w, lru_w=lru_w),
        grid=(T // tm,),
        in_specs=[pl.BlockSpec((tm, D), lambda i: (i, 0)),
                  pl.BlockSpec((None, None, N_MOD, D), lambda i: (l, row_idx(i), 0, 0)),
                  pl.BlockSpec((None, D, cols), lambda i: (l, 0, 0), pipeline_mode=pl.Buffered(1))],
        out_specs=[pl.BlockSpec((tm, w), lambda i: (i, 0)) for w, _ in widths],
        out_shape=[jax.ShapeDtypeStruct((T, w), dt) for w, dt in widths],
        compiler_params=_cparams(("parallel",)),
        name="inproj",
    )(x_all, mods_l, w_in)


def _na_bias_tables(rpb):
    H = rpb.shape[0]
    W = GRID_W
    c = np.arange(W)
    col0 = np.clip(c - NA_KW // 2, 0, W - NA_KW)
    jj = np.arange(W)
    valid = (jj[None, :] >= col0[:, None]) & (jj[None, :] < col0[:, None] + NA_KW)
    dc = np.clip(jj[None, :] - c[:, None] + (NA_KW - 1), 0, 2 * NA_KW - 2)
    bt = jnp.where(valid[None, None], rpb[:, :, dc], MASK_BIAS)
    case = np.arange(NA_KH)[:, None]
    i = np.arange(NA_KH)[None, :]
    wb = bt[:, i - case + (NA_KH - 1)]
    wb = wb.transpose(0, 1, 3, 2, 4).reshape(H // 2, 2, NA_KH, W, NA_KH * W)
    return wb.astype(F32)


def _masked_heads(q):
    lane = lax.broadcasted_iota(jnp.int32, q.shape, 1)
    qf = q.astype(F32)
    return (jnp.where(lane < HEAD_DIM, qf, 0.0).astype(BF16),
            jnp.where(lane >= HEAD_DIM, qf, 0.0).astype(BF16))


def _qk(q, k):
    return lax.dot_general(q, k, (((1,), (1,)), ((), ())), preferred_element_type=F32)


def _attn_kernel(q_ref, k_ref, v_ref, kc_ref, vc_ref, wb_ref, o_ref, *, R, rows):
    jb = pl.program_id(2)
    kc = kc_ref[...]
    vc = vc_ref[...]
    W = GRID_W
    nwin = NA_KH * W
    lane = lax.broadcasted_iota(jnp.int32, (W, 2 * HEAD_DIM), 1)

    def body(rr, carry):
        r = jb * R + rr
        rs = jnp.clip(r - NA_KH // 2, 0, rows - NA_KH)
        case = r - rs
        koff = pl.multiple_of(rs * W, W)
        kw = k_ref[pl.ds(koff, nwin), :]
        vw = v_ref[pl.ds(koff, nwin), :]
        qoff = pl.multiple_of(rr * W, W)
        outs = []
        for hh, qm in enumerate(_masked_heads(q_ref[pl.ds(qoff, W), :])):
            s_loc = _qk(qm, kw) + wb_ref[hh, case]
            s_ctx = _qk(qm, kc)
            m = jnp.maximum(jnp.max(s_loc, axis=-1, keepdims=True), jnp.max(s_ctx, axis=-1, keepdims=True))
            p_loc = jnp.exp(s_loc - m)
            p_ctx = jnp.exp(s_ctx - m)
            den = jnp.sum(p_loc, axis=-1, keepdims=True) + jnp.sum(p_ctx, axis=-1, keepdims=True)
            o = (jnp.dot(p_loc.astype(BF16), vw, preferred_element_type=F32)
                 + jnp.dot(p_ctx.astype(BF16), vc, preferred_element_type=F32))
            outs.append(o / den)
        o_ref[pl.ds(qoff, W), :] = jnp.where(lane < HEAD_DIM, outs[0], outs[1]).astype(BF16)
        return carry

    lax.fori_loop(0, R, body, 0)


def _attn_lat(q, k, v, wb_l, *, B, S, CTX):
    na_w = q.shape[1]
    npair = na_w // (2 * HEAD_DIM)
    rows = S // GRID_W
    R = 8
    nrb = rows // R
    pw = 2 * HEAD_DIM
    ctx0 = (B * S) // CTX
    return pl.pallas_call(
        functools.partial(_attn_kernel, R=R, rows=rows),
        grid=(B, npair, nrb),
        in_specs=[pl.BlockSpec((R * GRID_W, pw), lambda b, p, j: (b * nrb + j, p)),
                  pl.BlockSpec((S, pw), lambda b, p, j: (b, p)),
                  pl.BlockSpec((S, pw), lambda b, p, j: (b, p)),
                  pl.BlockSpec((CTX, pw), lambda b, p, j: (ctx0 + b, p)),
                  pl.BlockSpec((CTX, pw), lambda b, p, j: (ctx0 + b, p)),
                  pl.BlockSpec((None, 2, NA_KH, GRID_W, NA_KH * GRID_W), lambda b, p, j: (p, 0, 0, 0, 0))],
        out_specs=pl.BlockSpec((R * GRID_W, pw), lambda b, p, j: (b * nrb + j, p)),
        out_shape=jax.ShapeDtypeStruct((B * S, na_w), BF16),
        compiler_params=_cparams(("parallel", "parallel", "parallel")),
        name="attn_lat",
    )(q, k, v, k, v, wb_l)


def _attn_ctx_kernel(q_ref, k_ref, v_ref, o_ref):
    k = k_ref[...]
    v = v_ref[...]
    lane = lax.broadcasted_iota(jnp.int32, o_ref.shape, 1)
    outs = []
    for qm in _masked_heads(q_ref[...]):
        s = _qk(qm, k)
        p = jnp.exp(s - jnp.max(s, axis=-1, keepdims=True))
        den = jnp.sum(p, axis=-1, keepdims=True)
        outs.append(jnp.dot(p.astype(BF16), v, preferred_element_type=F32) / den)
    o_ref[...] = jnp.where(lane < HEAD_DIM, outs[0], outs[1]).astype(BF16)


def _attn_ctx(q, k, v, *, B, S, CTX):
    na_w = q.shape[1]
    pw = 2 * HEAD_DIM
    ctx0 = (B * S) // CTX
    spec = pl.BlockSpec((CTX, pw), lambda b, p: (ctx0 + b, p))
    return pl.pallas_call(
        _attn_ctx_kernel,
        grid=(B, na_w // pw),
        in_specs=[spec, spec, spec],
        out_specs=pl.BlockSpec((CTX, pw), lambda b, p: (b, p)),
        out_shape=jax.ShapeDtypeStruct((B * CTX, na_w), BF16),
        compiler_params=_cparams(("parallel", "parallel")),
        name="attn_ctx",
    )(q, k, v)


def _conv4(x_prev, x_main, x_next, cw):
    n = x_main.shape[0]
    xe = jnp.concatenate([x_prev, x_main, x_next], axis=0)
    ne = n + 2 * SUBLANES
    y = cw[2:3] * x_main
    y = y + cw[0:1] * pltpu.roll(xe, 2, 0)[SUBLANES:SUBLANES + n]
    y = y + cw[1:2] * pltpu.roll(xe, 1, 0)[SUBLANES:SUBLANES + n]
    y = y + cw[3:4] * pltpu.roll(xe, ne - 1, 0)[SUBLANES:SUBLANES + n]
    return y + cw[4:5]


def _softplus(x):
    return jnp.maximum(x, 0.0) + jnp.log1p(jnp.exp(-jnp.abs(x)))


def _lru_coeffs(xl, w, vec):
    C = xl.shape[1]
    gates = jnp.dot(xl.astype(BF16), w, preferred_element_type=F32)
    r = jax.nn.sigmoid(gates[:, :C] + vec[0:1])
    i = jax.nn.sigmoid(gates[:, C:] + vec[1:2])
    log_a = (-LRU_C * r) * _softplus(-vec[2:3])
    a = jnp.exp(log_a)
    b = jnp.sqrt(-jnp.tanh(log_a) * (1.0 + a * a)) * (i * xl)
    return a, b


def _group_scan(a, b, reverse):
    n, C = a.shape
    a = a.reshape(n // SUBLANES, SUBLANES, C)
    b = b.reshape(n // SUBLANES, SUBLANES, C)
    row = lax.broadcasted_iota(jnp.int32, a.shape, 1)
    for k in (1, 2, 4):
        shift = SUBLANES - k if reverse else k
        mask = (row < SUBLANES - k) if reverse else (row >= k)
        a_s = pltpu.roll(a, shift, 1)
        b_s = pltpu.roll(b, shift, 1)
        b = jnp.where(mask, a * b_s + b, b)
        a = jnp.where(mask, a * a_s, a)
    return a.reshape(n, C), b.reshape(n, C)


def _chunk_scan(a, b, h_in, reverse, acum_ref, bcum_ref, out_ref):
    n, C = a.shape
    ac, bc = _group_scan(a, b, reverse)
    acum_ref[...] = ac
    bcum_ref[...] = bc
    ng = n // SUBLANES

    def body(g, hb):
        idx = (ng - 1 - g) if reverse else g
        sl = pl.ds(pl.multiple_of(idx * SUBLANES, SUBLANES), SUBLANES)
        h = acum_ref[sl, :] * hb + bcum_ref[sl, :]
        out_ref[sl, :] = h
        edge = h[0:1] if reverse else h[SUBLANES - 1:SUBLANES]
        return jnp.broadcast_to(edge, (SUBLANES, C))

    return lax.fori_loop(0, ng, body, h_in, unroll=8)


def _lru_ctx_kernel(xr_ref, gr_ref, cw_ref, w_ref, vec_ref, o_ref, hend_ref, acum_ref, bcum_ref, hf_ref, hb_ref):
    xr = xr_ref[...]
    zeros8 = jnp.zeros((SUBLANES, xr.shape[1]), F32)
    xl = _conv4(zeros8, xr, zeros8, cw_ref[...])
    a, b = _lru_coeffs(xl, w_ref[0], vec_ref[0])
    hf_end = _chunk_scan(a, b, zeros8, False, acum_ref, bcum_ref, hf_ref)
    a, b = _lru_coeffs(xl, w_ref[1], vec_ref[1])
    hb_end = _chunk_scan(a, b, zeros8, True, acum_ref, bcum_ref, hb_ref)
    o_ref[...] = ((hf_ref[...] + hb_ref[...]) * jax.nn.gelu(gr_ref[...])).astype(BF16)
    row = lax.broadcasted_iota(jnp.int32, hf_end.shape, 0)
    hend_ref[...] = jnp.where(row == 0, hf_end, hb_end)


def _lru_ctx(xr, gr, cw, w, vec, *, B, S, CTX):
    C = xr.shape[1]
    ctx0 = (B * S) // CTX
    spec = pl.BlockSpec((CTX, C), lambda b: (ctx0 + b, 0))
    return pl.pallas_call(
        _lru_ctx_kernel,
        grid=(B,),
        in_specs=[spec, spec,
                  pl.BlockSpec((SUBLANES, C), lambda b: (0, 0)),
                  pl.BlockSpec((2, C, 2 * C), lambda b: (0, 0, 0)),
                  pl.BlockSpec((2, SUBLANES, C), lambda b: (0, 0, 0))],
        out_specs=[pl.BlockSpec((CTX, C), lambda b: (b, 0)),
                   pl.BlockSpec((None, SUBLANES, C), lambda b: (b, 0, 0))],
        out_shape=[jax.ShapeDtypeStruct((B * CTX, C), BF16),
                   jax.ShapeDtypeStruct((B, SUBLANES, C), F32)],
        scratch_shapes=[pltpu.VMEM((CTX, C), F32)] * 4,
        compiler_params=_cparams(("parallel",)),
        name="lru_ctx",
    )(xr, gr, cw, w, vec)


def _lru_lat_kernel(*refs, reverse, nch):
    if reverse:
        xr_ref, xp_ref, xn_ref, hend_ref, cw_ref, w_ref, vec_ref, o_ref, carry_ref, acum_ref, bcum_ref = refs
    else:
        (xr_ref, xp_ref, xn_ref, hend_ref, cw_ref, w_ref, vec_ref, gr_ref, hb_ref,
         o_ref, carry_ref, acum_ref, bcum_ref, hf_ref) = refs
    i = pl.program_id(1)
    c = (nch - 1 - i) if reverse else i
    C = xr_ref.shape[1]

    @pl.when(i == 0)
    def _():
        row = hend_ref[1:2, :] if reverse else hend_ref[0:1, :]
        carry_ref[...] = jnp.broadcast_to(row, (SUBLANES, C))

    xp = jnp.where(c == 0, 0.0, xp_ref[...])
    xn = jnp.where(c == nch - 1, 0.0, xn_ref[...])
    xl = _conv4(xp, xr_ref[...], xn, cw_ref[...])
    a, b = _lru_coeffs(xl, w_ref[...], vec_ref[...])
    if reverse:
        carry_ref[...] = _chunk_scan(a, b, carry_ref[...], True, acum_ref, bcum_ref, o_ref)
    else:
        carry_ref[...] = _chunk_scan(a, b, carry_ref[...], False, acum_ref, bcum_ref, hf_ref)
        o_ref[...] = ((hf_ref[...] + hb_ref[...]) * jax.nn.gelu(gr_ref[...])).astype(BF16)


def _lru_lat(xr, gr, hb, hend, cw, w, vec, *, reverse, B, S, tt):
    C = xr.shape[1]
    nch = S // tt
    d = 1 if reverse else 0
    chunk = (lambda i: nch - 1 - i) if reverse else (lambda i: i)
    nblk8 = (B * S) // SUBLANES
    main = pl.BlockSpec((tt, C), lambda b, i: (b * nch + chunk(i), 0))
    in_specs = [
        main,
        pl.BlockSpec((SUBLANES, C), lambda b, i: (jnp.maximum((b * nch + chunk(i)) * (tt // SUBLANES) - 1, 0), 0)),
        pl.BlockSpec((SUBLANES, C), lambda b, i: (jnp.minimum((b * nch + chunk(i) + 1) * (tt // SUBLANES), nblk8 - 1), 0)),
        pl.BlockSpec((None, SUBLANES, C), lambda b, i: (b, 0, 0)),
        pl.BlockSpec((SUBLANES, C), lambda b, i: (0, 0)),
        pl.BlockSpec((None, C, 2 * C), lambda b, i: (d, 0, 0)),
        pl.BlockSpec((None, SUBLANES, C), lambda b, i: (d, 0, 0))]
    args = [xr, xr, xr, hend, cw, w, vec]
    scratch = [pltpu.VMEM((SUBLANES, C), F32), pltpu.VMEM((tt, C), F32), pltpu.VMEM((tt, C), F32)]
    if reverse:
        out_dtype = F32
    else:
        in_specs += [main, main]
        args += [gr, hb]
        scratch.append(pltpu.VMEM((tt, C), F32))
        out_dtype = BF16
    return pl.pallas_call(
        functools.partial(_lru_lat_kernel, reverse=reverse, nch=nch),
        grid=(B, nch),
        in_specs=in_specs,
        out_specs=main,
        out_shape=jax.ShapeDtypeStruct((B * S, C), out_dtype),
        scratch_shapes=scratch,
        compiler_params=_cparams(("parallel", "arbitrary")),
        name="lru_bwd" if reverse else "lru_fwd",
    )(*args)


def _fno_w_kernel(fw_ref, cs_ref, o_ref):
    fw = fw_ref[...]
    o_ref[0] = jnp.dot(cs_ref[0], fw, preferred_element_type=F32, precision=HIGHEST).astype(BF16)
    o_ref[1] = jnp.dot(cs_ref[1], fw, preferred_element_type=F32, precision=HIGHEST).astype(BF16)


def _fno_weights(fw_bd, cs_bd):
    L, C, _ = fw_bd.shape
    return pl.pallas_call(
        _fno_w_kernel,
        grid=(L,),
        in_specs=[pl.BlockSpec((None, C, C), lambda l: (l, 0, 0)),
                  pl.BlockSpec((2, C, C), lambda l: (0, 0, 0))],
        out_specs=pl.BlockSpec((None, 2, C, C), lambda l: (l, 0, 0, 0)),
        out_shape=jax.ShapeDtypeStruct((L, 2, C, C), BF16),
        compiler_params=_cparams(("parallel",)),
        name="fno_w",
    )(fw_bd, cs_bd)


def _fno_a_kernel(x_ref, w_ref, m_ref, o_ref):
    n1, s, C = x_ref.shape
    x = x_ref[...].reshape(n1 * s, C).astype(BF16)
    u = jnp.dot(x, w_ref[0], preferred_element_type=F32).astype(BF16)
    v = jnp.dot(x, w_ref[1], preferred_element_type=F32).astype(BF16)
    uv = jnp.concatenate([u, v], axis=0)
    a = jnp.dot(m_ref[...], uv, preferred_element_type=F32).astype(BF16)
    o_ref[...] = a.reshape(o_ref.shape)


def _fno_c_kernel(a_ref, f_ref, o_ref, *, scale):
    o_ref[...] = (jnp.dot(f_ref[...], a_ref[...], preferred_element_type=F32) * scale).astype(BF16)


def _fno_lat(f, w_l, m_a, f_c, *, B, S):
    T, C = f.shape
    n1 = FNO_N1
    n2 = S // n1
    nj = n2 // SUBLANES
    f4 = f.reshape(T // n2, nj, SUBLANES, C)
    rows_a = 2 * SUBLANES * n1
    a2 = pl.pallas_call(
        _fno_a_kernel,
        grid=(nj, B),
        in_specs=[pl.BlockSpec((n1, None, SUBLANES, C), lambda j, b: (b, j, 0, 0)),
                  pl.BlockSpec((2, C, C), lambda j, b: (0, 0, 0)),
                  pl.BlockSpec((None, rows_a, rows_a), lambda j, b: (j, 0, 0))],
        out_specs=pl.BlockSpec((None, 2, SUBLANES, n1, C), lambda j, b: (b, 0, j, 0, 0)),
        out_shape=jax.ShapeDtypeStruct((B, 2, n2, n1, C), BF16),
        compiler_params=_cparams(("parallel", "parallel")),
        name="fno_a",
    )(f4, w_l, m_a)
    ncol = n1 * C
    tc = min(ncol, 2048)
    scale = 1.0 / math.sqrt(S * (C // FNO_GROUPS))
    out = pl.pallas_call(
        functools.partial(_fno_c_kernel, scale=scale),
        grid=(B, ncol // tc),
        in_specs=[pl.BlockSpec((None, 2 * n2, tc), lambda b, t: (b, 0, t)),
                  pl.BlockSpec((n2, 2 * n2), lambda b, t: (0, 0))],
        out_specs=pl.BlockSpec((None, n2, tc), lambda b, t: (b, 0, t)),
        out_shape=jax.ShapeDtypeStruct((B, n2, ncol), BF16),
        compiler_params=_cparams(("parallel", "parallel")),
        name="fno_c",
    )(a2.reshape(B, 2 * n2, ncol), f_c)
    return out.reshape(B * S, C)


def _fno_ctx_kernel(x_ref, w_ref, f_ref, o_ref, *, scale):
    x = x_ref[...].astype(BF16)
    u = jnp.dot(x, w_ref[0], preferred_element_type=F32).astype(BF16)
    v = jnp.dot(x, w_ref[1], preferred_element_type=F32).astype(BF16)
    uv = jnp.concatenate([u, v], axis=0)
    o_ref[...] = (jnp.dot(f_ref[...], uv, preferred_element_type=F32) * scale).astype(BF16)


def _fno_ctx(f, w_l, f_ctx, *, B, S, CTX):
    C = f.shape[1]
    ctx0 = (B * S) // CTX
    scale = 1.0 / math.sqrt(CTX * (C // FNO_GROUPS))
    return pl.pallas_call(
        functools.partial(_fno_ctx_kernel, scale=scale),
        grid=(B,),
        in_specs=[pl.BlockSpec((CTX, C), lambda b: (ctx0 + b, 0)),
                  pl.BlockSpec((2, C, C), lambda b: (0, 0, 0)),
                  pl.BlockSpec((CTX, 2 * CTX), lambda b: (0, 0))],
        out_specs=pl.BlockSpec((CTX, C), lambda b: (b, 0)),
        out_shape=jax.ShapeDtypeStruct((B * CTX, C), BF16),
        compiler_params=_cparams(("parallel",)),
        name="fno_ctx",
    )(f, w_l, f_ctx)


def _dft_constants(S, CTX, C):
    gd = C // FNO_GROUPS
    n1, n2 = FNO_N1, S // FNO_N1
    nj = n2 // SUBLANES
    idx = jnp.arange(gd, dtype=jnp.int32)
    ang = (2.0 * math.pi / gd) * ((idx[:, None] * idx[None, :]) % gd).astype(F32)
    eye_g = jnp.eye(FNO_GROUPS, dtype=F32)
    cs_bd = jnp.stack([jnp.kron(eye_g, jnp.cos(ang)), jnp.kron(eye_g, jnp.sin(ang))])
    k1 = jnp.arange(n1, dtype=jnp.int32)[:, None, None]
    nn1 = jnp.arange(n1, dtype=jnp.int32)[None, :, None]
    nn2 = jnp.arange(n2, dtype=jnp.int32)[None, None, :]
    ph = (2.0 * math.pi / S) * ((k1 * (n2 * nn1 + nn2)) % S).astype(F32)
    cph = jnp.cos(ph).reshape(n1, n1, nj, SUBLANES)
    sph = jnp.sin(ph).reshape(n1, n1, nj, SUBLANES)
    eye_s = jnp.eye(SUBLANES, dtype=F32)

    def block(t):
        return jnp.einsum('knjs,st->jsknt', t, eye_s)

    cb, sb = block(cph), block(sph)
    top = jnp.stack([cb, -sb], axis=3)
    bot = jnp.stack([-sb, -cb], axis=3)
    m_a = jnp.stack([top, bot], axis=1)
    m_a = m_a.reshape(nj, 2 * SUBLANES * n1, 2 * n1 * SUBLANES).astype(BF16)
    i2 = jnp.arange(n2, dtype=jnp.int32)
    a2 = (2.0 * math.pi / n2) * ((i2[:, None] * i2[None, :]) % n2).astype(F32)
    f_c = jnp.concatenate([jnp.cos(a2), jnp.sin(a2)], axis=1).astype(BF16)
    ic = jnp.arange(CTX, dtype=jnp.int32)
    ac = (2.0 * math.pi / CTX) * ((ic[:, None] * ic[None, :]) % CTX).astype(F32)
    f_ctx = jnp.concatenate([jnp.cos(ac), -jnp.sin(ac)], axis=1).astype(BF16)
    return cs_bd, m_a, f_c, f_ctx


def _outproj_kernel(*refs, alpha, nlat, has_ctx, na_w, lru_w):
    if has_ctx:
        x_ref, m_ref, nal, lrul, fnol, nac, lruc, fnoc, w_ref, g_ref, b_ref, o_ref = refs
        is_lat = pl.program_id(0) < nlat
        na = jnp.where(is_lat, nal[...], nac[...])
        lru = jnp.where(is_lat, lrul[...], lruc[...])
        fno = jnp.where(is_lat, fnol[...], fnoc[...])
    else:
        x_ref, m_ref, nal, lrul, fnol, w_ref, g_ref, b_ref, o_ref = refs
        na, lru, fno = nal[...], lrul[...], fnol[...]
    y = jnp.dot(na, w_ref[0:na_w, :], preferred_element_type=F32)
    y = y + jnp.dot(lru, w_ref[na_w:na_w + lru_w, :], preferred_element_type=F32)
    y = y + jnp.dot(fno, w_ref[na_w + lru_w:, :], preferred_element_type=F32)
    m = m_ref[...]
    z = alpha * x_ref[...] + m[5:6] * y
    o_ref[...] = _layer_norm(z, g_ref[...], b_ref[...])


def _outproj(x_all, mods_l, lat, ctx, w_out, ln_g, ln_b, *, l, alpha, tm, tiles_per_batch, nbatch, nlat, ntiles):
    D = x_all.shape[1]
    has_ctx = ctx is not None
    row_idx = lambda i: jnp.minimum(i // tiles_per_batch, nbatch)
    lat_specs = [pl.BlockSpec((tm, a.shape[1]), lambda i: (jnp.minimum(i, nlat - 1), 0)) for a in lat]
    ctx_specs, ctx_args = [], []
    if has_ctx:
        nctx = ntiles - nlat
        ctx_specs = [pl.BlockSpec((tm, a.shape[1]), lambda i: (jnp.clip(i - nlat, 0, nctx - 1), 0)) for a in ctx]
        ctx_args = list(ctx)
    return pl.pallas_call(
        functools.partial(_outproj_kernel, alpha=alpha, nlat=nlat, has_ctx=has_ctx,
                          na_w=lat[0].shape[1], lru_w=lat[1].shape[1]),
        grid=(ntiles,),
        in_specs=[pl.BlockSpec((tm, D), lambda i: (i, 0)),
                  pl.BlockSpec((None, None, N_MOD, D), lambda i: (l, row_idx(i), 0, 0))]
        + lat_specs + ctx_specs
        + [pl.BlockSpec((None, D, D), lambda i: (l, 0, 0), pipeline_mode=pl.Buffered(1)),
           pl.BlockSpec((None, 1, D), lambda i: (l, 0, 0)),
           pl.BlockSpec((None, 1, D), lambda i: (l, 0, 0))],
        out_specs=pl.BlockSpec((tm, D), lambda i: (i, 0)),
        out_shape=jax.ShapeDtypeStruct((ntiles * tm, D), F32),
        compiler_params=_cparams(("parallel",)),
        name="outproj",
    )(x_all, mods_l, *lat, *ctx_args, w_out, ln_g, ln_b)


def _block_diag(w):
    G, n = w.shape[-3], w.shape[-1]
    eye = jnp.eye(G, dtype=w.dtype)
    out = jnp.einsum('...gij,gh->...gihj', w, eye)
    return out.reshape(w.shape[:-3] + (G * n, G * n))


def kernel(x, c, ctx, c_ctx, w_ada, b_ada, ln_g, ln_b, ff1_gate, ff1_up, ff1_down, ff2_gate, ff2_up, ff2_down,
           w_in, w_out, na_rpb, lru_conv_w, lru_conv_b, lru_wa, lru_ba, lru_wx, lru_bx, lru_lambda, fno_w):
    B, S, D = x.shape
    CTX = ctx.shape[1]
    L = w_ada.shape[0]
    na_w = na_rpb.shape[1] * HEAD_DIM
    lru_w = lru_conv_w.shape[2]
    alpha = float((2 * L) ** 0.25)
    tm = min(512, B * CTX)
    tiles_per_batch = S // tm
    nlat = (B * S) // tm
    ntiles = nlat + (B * CTX) // tm
    assert B < 16 and S % tm == 0 and (B * CTX) % tm == 0 and S % (GRID_W * NA_KH) == 0

    cc = jnp.zeros((16, D), F32).at[:B].set(c).at[B].set(c_ctx)
    mods = _ada_params(cc, w_ada, b_ada)

    bf = lambda w: w.astype(BF16)
    ff1 = (bf(ff1_gate), bf(ff1_up), bf(ff1_down))
    ff2 = (bf(ff2_gate), bf(ff2_up), bf(ff2_down))
    w_in_b, w_out_b = bf(w_in), bf(w_out)
    ln_g4 = ln_g[:, :, None, :]
    ln_b4 = ln_b[:, :, None, :]
    wb = jax.vmap(_na_bias_tables)(na_rpb)
    lru_gate_w = bf(jnp.concatenate([_block_diag(lru_wa), _block_diag(lru_wx)], axis=-1))
    zrow = jnp.zeros((L, 2, SUBLANES - 3, lru_w), F32)
    lru_vec = jnp.concatenate([lru_ba[:, :, None], lru_bx[:, :, None], lru_lambda[:, :, None], zrow], axis=2)
    lru_cw = jnp.concatenate([lru_conv_w, lru_conv_b[:, None], jnp.zeros((L, SUBLANES - LRU_CONV - 1, lru_w), F32)], axis=1)
    cs_bd, m_a, f_c, f_ctx = _dft_constants(S, CTX, D - na_w - lru_w)
    fno_wcs = _fno_weights(_block_diag(fno_w), cs_bd)

    common = dict(tm=tm, tiles_per_batch=tiles_per_batch, nbatch=B)
    xs = (x.reshape(B * S, D), ctx.reshape(B * CTX, D))
    for l in range(L):
        last = l == L - 1
        x1 = _ffn(xs, mods, *ff1, ln_g4[:, 0], ln_b4[:, 0], j=0, l=l, alpha=alpha, nlat=nlat, ntiles=ntiles, **common)
        q, k, v, xr, gr, f = _inproj(x1, mods, w_in_b, l=l, na_w=na_w, lru_w=lru_w, **common)
        na_lat = _attn_lat(q, k, v, wb[l], B=B, S=S, CTX=CTX)
        lru_ctx, hend = _lru_ctx(xr, gr, lru_cw[l], lru_gate_w[l], lru_vec[l], B=B, S=S, CTX=CTX)
        tt = min(S, 1024)
        hb = _lru_lat(xr, None, None, hend, lru_cw[l], lru_gate_w[l], lru_vec[l], reverse=True, B=B, S=S, tt=tt)
        lru_lat = _lru_lat(xr, gr, hb, hend, lru_cw[l], lru_gate_w[l], lru_vec[l], reverse=False, B=B, S=S, tt=tt)
        fno_lat = _fno_lat(f, fno_wcs[l], m_a, f_c, B=B, S=S)
        if last:
            ctx_parts = None
            nt = nlat
        else:
            ctx_parts = (_attn_ctx(q, k, v, B=B, S=S, CTX=CTX), lru_ctx, _fno_ctx(f, fno_wcs[l], f_ctx, B=B, S=S, CTX=CTX))
            nt = ntiles
        x2 = _outproj(x1, mods, (na_lat, lru_lat, fno_lat), ctx_parts, w_out_b, ln_g4[:, 1], ln_b4[:, 1],
                      l=l, alpha=alpha, nlat=nlat, ntiles=nt, **common)
        x3 = _ffn((x2,), mods, *ff2, ln_g4[:, 2], ln_b4[:, 2], j=2, l=l, alpha=alpha, nlat=nlat, ntiles=nt, **common)
        xs = (x3,)
    return xs[0].reshape(B, S, D)
```

```python
import functools
import math

import numpy as np
import jax
import jax.numpy as jnp
from jax import lax
from jax.experimental import pallas as pl
from jax.experimental.pallas import tpu as pltpu

F32 = jnp.float32
BF16 = jnp.bfloat16
HIGHEST = lax.Precision.HIGHEST

HEAD_DIM = 64
GRID_W = 64
NA_KH = 8
NA_KW = 16
LRU_HEADS = 4
LRU_CONV = 4
LRU_C = 8.0
FNO_GROUPS = 4
N_MOD = 9
MACARON = 0.5
LN_EPS = 1e-5
MASK_BIAS = -1e30
SUBLANES = 8
FNO_N1 = 64
VMEM_LIMIT = 56 * 1024 * 1024


def _cparams(sem):
    return pltpu.CompilerParams(dimension_semantics=sem, vmem_limit_bytes=VMEM_LIMIT)


def _layer_norm(z, g, b):
    mu = jnp.mean(z, axis=-1, keepdims=True)
    zc = z - mu
    var = jnp.mean(zc * zc, axis=-1, keepdims=True)
    return zc * lax.rsqrt(var + LN_EPS) * g + b


def _ada_kernel(c_ref, w_ref, b_ref, o_ref):
    c = c_ref[...]
    a = c * jax.nn.sigmoid(c)
    o_ref[...] = jnp.dot(a, w_ref[...], preferred_element_type=F32, precision=HIGHEST) + b_ref[...]


def _ada_params(cc, w_ada, b_ada):
    L, D, ND = w_ada.shape
    tn = D
    out = pl.pallas_call(
        _ada_kernel,
        grid=(L, ND // tn),
        in_specs=[pl.BlockSpec((16, D), lambda l, n: (0, 0)),
                  pl.BlockSpec((None, D, tn), lambda l, n: (l, 0, n)),
                  pl.BlockSpec((None, 1, tn), lambda l, n: (l, 0, n))],
        out_specs=pl.BlockSpec((None, 16, tn), lambda l, n: (l, 0, n)),
        out_shape=jax.ShapeDtypeStruct((L, 16, ND), F32),
        compiler_params=_cparams(("parallel", "parallel")),
        name="ada",
    )(cc, w_ada, b_ada.reshape(L, 1, ND))
    return out.reshape(L, 16, N_MOD, D)


def _ffn_kernel(*refs, j, tf, alpha, nlat, two_x):
    if two_x:
        xl_ref, xc_ref, m_ref, wg_ref, wu_ref, wd_ref, g_ref, b_ref, o_ref, h_ref, a_ref = refs
        x = jnp.where(pl.program_id(0) < nlat, xl_ref[...], xc_ref[...])
    else:
        x_ref, m_ref, wg_ref, wu_ref, wd_ref, g_ref, b_ref, o_ref, h_ref, a_ref = refs
        x = x_ref[...]
    m = m_ref[...]
    shift, scale, gate = m[3 * j:3 * j + 1], m[3 * j + 1:3 * j + 2], m[3 * j + 2:3 * j + 3]
    h_ref[...] = (x * (1.0 + scale) + shift).astype(BF16)
    for c in range(wg_ref.shape[1] // tf):
        cols = slice(c * tf, (c + 1) * tf)
        h = h_ref[...]
        g = jnp.dot(h, wg_ref[:, cols], preferred_element_type=F32)
        u = jnp.dot(h, wu_ref[:, cols], preferred_element_type=F32)
        a_ref[:, cols] = (g * jax.nn.sigmoid(g) * u).astype(BF16)
    y = jnp.dot(a_ref[...], wd_ref[...], preferred_element_type=F32)
    z = alpha * x + (MACARON * gate) * y
    o_ref[...] = _layer_norm(z, g_ref[...], b_ref[...])


def _ffn(xs, mods_l, wg, wu, wd, ln_g, ln_b, *, j, l, alpha, tm, tiles_per_batch, nlat, ntiles, nbatch):
    two_x = len(xs) == 2
    D = xs[0].shape[1]
    Fd = wg.shape[2]
    tf = 256
    row_idx = lambda i: jnp.minimum(i // tiles_per_batch, nbatch)
    if two_x:
        nctx = ntiles - nlat
        x_specs = [pl.BlockSpec((tm, D), lambda i: (jnp.minimum(i, nlat - 1), 0)),
                   pl.BlockSpec((tm, D), lambda i: (jnp.clip(i - nlat, 0, nctx - 1), 0))]
    else:
        x_specs = [pl.BlockSpec((tm, D), lambda i: (i, 0))]
    const = pl.Buffered(1)
    return pl.pallas_call(
        functools.partial(_ffn_kernel, j=j, tf=tf, alpha=alpha, nlat=nlat, two_x=two_x),
        grid=(ntiles,),
        in_specs=x_specs + [
            pl.BlockSpec((None, None, N_MOD, D), lambda i: (l, row_idx(i), 0, 0)),
            pl.BlockSpec((None, D, Fd), lambda i: (l, 0, 0), pipeline_mode=const),
            pl.BlockSpec((None, D, Fd), lambda i: (l, 0, 0), pipeline_mode=const),
            pl.BlockSpec((None, Fd, D), lambda i: (l, 0, 0), pipeline_mode=const),
            pl.BlockSpec((None, 1, D), lambda i: (l, 0, 0)),
            pl.BlockSpec((None, 1, D), lambda i: (l, 0, 0))],
        out_specs=pl.BlockSpec((tm, D), lambda i: (i, 0)),
        out_shape=jax.ShapeDtypeStruct((ntiles * tm, D), F32),
        scratch_shapes=[pltpu.VMEM((tm, D), BF16), pltpu.VMEM((tm, Fd), BF16)],
        compiler_params=_cparams(("parallel",)),
        name=f"ffn{j}",
    )(*xs, mods_l, wg, wu, wd, ln_g, ln_b)


def _inproj_kernel(x_ref, m_ref, w_ref, q_ref, k_ref, v_ref, xr_ref, gr_ref, f_ref, *, na_w, lru_w):
    m = m_ref[...]
    h = (x_ref[...] * (1.0 + m[4:5]) + m[3:4]).astype(BF16)

    def proj(lo, width):
        return jnp.dot(h, w_ref[:, lo:lo + width], preferred_element_type=F32)

    q_ref[...] = (proj(0, na_w) * (HEAD_DIM ** -0.5)).astype(BF16)
    k_ref[...] = proj(na_w, na_w).astype(BF16)
    v_ref[...] = proj(2 * na_w, na_w).astype(BF16)
    xr_ref[...] = proj(3 * na_w, lru_w)
    gr_ref[...] = proj(3 * na_w + lru_w, lru_w)
    f_ref[...] = proj(3 * na_w + 2 * lru_w, w_ref.shape[1] - 3 * na_w - 2 * lru_w)


def _inproj(x_all, mods_l, w_in, *, l, tm, tiles_per_batch, nbatch, na_w, lru_w):
    T, D = x_all.shape
    cols = w_in.shape[2]
    fno_w = cols - 3 * na_w - 2 * lru_w
    row_idx = lambda i: jnp.minimum(i // tiles_per_batch, nbatch)
    widths = [(na_w, BF16)] * 3 + [(lru_w, F32)] * 2 + [(fno_w, F32)]
    return pl.pallas_call(
        functools.partial(_inproj_kernel, na_w=na_w, lru_w=lru_w),
        grid=(T // tm,),
        in_specs=[pl.BlockSpec((tm, D), lambda i: (i, 0)),
                  pl.BlockSpec((None, None, N_MOD, D), lambda i: (l, row_idx(i), 0, 0)),
                  pl.BlockSpec((None, D, cols), lambda i: (l, 0, 0), pipeline_mode=pl.Buffered(1))],
        out_specs=[pl.BlockSpec((tm, w), lambda i: (i, 0)) for w, _ in widths],
        out_shape=[jax.ShapeDtypeStruct((T, w), dt) for w, dt in widths],
        compiler_params=_cparams(("parallel",)),
        name="inproj",
    )(x_all, mods_l, w_in)


def _na_bias_tables(rpb):
    H = rpb.shape[0]
    W = GRID_W
    c = np.arange(W)
    col0 = np.clip(c - NA_KW // 2, 0, W - NA_KW)
    jj = np.arange(W)
    valid = (jj[None, :] >= col0[:, None]) & (jj[None, :] < col0[:, None] + NA_KW)
    dc = np.clip(jj[None, :] - c[:, None] + (NA_KW - 1), 0, 2 * NA_KW - 2)
    bt = jnp.where(valid[None, None], rpb[:, :, dc], MASK_BIAS)
    case = np.arange(NA_KH)[:, None]
    i = np.arange(NA_KH)[None, :]
    wb = bt[:, i - case + (NA_KH - 1)]
    wb = wb.transpose(0, 1, 3, 2, 4).reshape(H // 2, 2, NA_KH, W, NA_KH * W)
    wb = wb.transpose(0, 2, 1, 3, 4).reshape(H // 2, NA_KH, 2 * W, NA_KH * W)
    return wb.astype(F32)


def _masked_heads(q):
    lane = lax.broadcasted_iota(jnp.int32, q.shape, 1)
    qf = q.astype(F32)
    return (jnp.where(lane < HEAD_DIM, qf, 0.0).astype(BF16),
            jnp.where(lane >= HEAD_DIM, qf, 0.0).astype(BF16))


def _qk(q, k):
    return lax.dot_general(q, k, (((1,), (1,)), ((), ())), preferred_element_type=F32)


def _attn_kernel(q_ref, k_ref, v_ref, kc_ref, vc_ref, wb_ref, o_ref, *, R, G, rows):
    jb = pl.program_id(2)
    kc = kc_ref[...]
    vc = vc_ref[...]
    W = GRID_W
    nwin = NA_KH * W
    lane = lax.broadcasted_iota(jnp.int32, (W, 2 * HEAD_DIM), 1)

    def group(g, carry):
        qoff = pl.multiple_of(g * (R * W), R * W)
        qcat = []
        for rr in range(R):
            qe, qo = _masked_heads(q_ref[pl.ds(qoff + rr * W, W), :])
            qcat += [qe, qo]
        s_ctx_all = _qk(jnp.concatenate(qcat, axis=0), kc)
        o_loc, p_ctx, dens = [], [], []
        for rr in range(R):
            r = (jb * G + g) * R + rr
            rs = jnp.clip(r - NA_KH // 2, 0, rows - NA_KH)
            koff = pl.multiple_of(rs * W, W)
            kw = k_ref[pl.ds(koff, nwin), :]
            vw = v_ref[pl.ds(koff, nwin), :]
            qc = jnp.concatenate(qcat[2 * rr:2 * rr + 2], axis=0)
            s_loc = _qk(qc, kw) + wb_ref[r - rs]
            s_ctx = s_ctx_all[rr * 2 * W:(rr + 1) * 2 * W]
            m = jnp.maximum(jnp.max(s_loc, axis=-1, keepdims=True), jnp.max(s_ctx, axis=-1, keepdims=True))
            p_loc = jnp.exp(s_loc - m)
            pc = jnp.exp(s_ctx - m)
            dens.append(jnp.sum(p_loc, axis=-1, keepdims=True) + jnp.sum(pc, axis=-1, keepdims=True))
            o_loc.append(jnp.dot(p_loc.astype(BF16), vw, preferred_element_type=F32))
            p_ctx.append(pc.astype(BF16))
        o_ctx_all = jnp.dot(jnp.concatenate(p_ctx, axis=0), vc, preferred_element_type=F32)
        outs = []
        for rr in range(R):
            o = (o_loc[rr] + o_ctx_all[rr * 2 * W:(rr + 1) * 2 * W]) / dens[rr]
            outs.append(jnp.where(lane < HEAD_DIM, o[:W], o[W:]).astype(BF16))
        o_ref[pl.ds(qoff, R * W), :] = jnp.concatenate(outs, axis=0)
        return carry

    lax.fori_loop(0, G, group, 0)


def _attn_lat(q, k, v, wb_l, *, B, S, CTX):
    na_w = q.shape[1]
    npair = na_w // (2 * HEAD_DIM)
    rows = S // GRID_W
    R = 8
    G = 4 if rows % (4 * R) == 0 else 1
    nrb = rows // (R * G)
    pw = 2 * HEAD_DIM
    ctx0 = (B * S) // CTX
    qblk = R * G * GRID_W
    return pl.pallas_call(
        functools.partial(_attn_kernel, R=R, G=G, rows=rows),
        grid=(B, npair, nrb),
        in_specs=[pl.BlockSpec((qblk, pw), lambda b, p, j: (b * nrb + j, p)),
                  pl.BlockSpec((S, pw), lambda b, p, j: (b, p)),
                  pl.BlockSpec((S, pw), lambda b, p, j: (b, p)),
                  pl.BlockSpec((CTX, pw), lambda b, p, j: (ctx0 + b, p)),
                  pl.BlockSpec((CTX, pw), lambda b, p, j: (ctx0 + b, p)),
                  pl.BlockSpec((None, NA_KH, 2 * GRID_W, NA_KH * GRID_W), lambda b, p, j: (p, 0, 0, 0))],
        out_specs=pl.BlockSpec((qblk, pw), lambda b, p, j: (b * nrb + j, p)),
        out_shape=jax.ShapeDtypeStruct((B * S, na_w), BF16),
        compiler_params=_cparams(("parallel", "parallel", "parallel")),
        name="attn_lat",
    )(q, k, v, k, v, wb_l)


def _attn_ctx_kernel(q_ref, k_ref, v_ref, o_ref):
    k = k_ref[...]
    v = v_ref[...]
    lane = lax.broadcasted_iota(jnp.int32, o_ref.shape, 1)
    outs = []
    for qm in _masked_heads(q_ref[...]):
        s = _qk(qm, k)
        p = jnp.exp(s - jnp.max(s, axis=-1, keepdims=True))
        den = jnp.sum(p, axis=-1, keepdims=True)
        outs.append(jnp.dot(p.astype(BF16), v, preferred_element_type=F32) / den)
    o_ref[...] = jnp.where(lane < HEAD_DIM, outs[0], outs[1]).astype(BF16)


def _attn_ctx(q, k, v, *, B, S, CTX):
    na_w = q.shape[1]
    pw = 2 * HEAD_DIM
    ctx0 = (B * S) // CTX
    spec = pl.BlockSpec((CTX, pw), lambda b, p: (ctx0 + b, p))
    return pl.pallas_call(
        _attn_ctx_kernel,
        grid=(B, na_w // pw),
        in_specs=[spec, spec, spec],
        out_specs=pl.BlockSpec((CTX, pw), lambda b, p: (b, p)),
        out_shape=jax.ShapeDtypeStruct((B * CTX, na_w), BF16),
        compiler_params=_cparams(("parallel", "parallel")),
        name="attn_ctx",
    )(q, k, v)


def _conv4(x_prev, x_main, x_next, cw):
    n = x_main.shape[0]
    xe = jnp.concatenate([x_prev, x_main, x_next], axis=0)
    ne = n + 2 * SUBLANES
    y = cw[2:3] * x_main
    y = y + cw[0:1] * pltpu.roll(xe, 2, 0)[SUBLANES:SUBLANES + n]
    y = y + cw[1:2] * pltpu.roll(xe, 1, 0)[SUBLANES:SUBLANES + n]
    y = y + cw[3:4] * pltpu.roll(xe, ne - 1, 0)[SUBLANES:SUBLANES + n]
    return y + cw[4:5]


def _softplus(x):
    return jnp.maximum(x, 0.0) + jnp.log1p(jnp.exp(-jnp.abs(x)))


def _lru_coeffs(xl, w, vec):
    C = xl.shape[1]
    gates = jnp.dot(xl.astype(BF16), w, preferred_element_type=F32)
    r = jax.nn.sigmoid(gates[:, :C] + vec[0:1])
    i = jax.nn.sigmoid(gates[:, C:] + vec[1:2])
    log_a = (-LRU_C * r) * _softplus(-vec[2:3])
    a = jnp.exp(log_a)
    b = jnp.sqrt(-jnp.tanh(log_a) * (1.0 + a * a)) * (i * xl)
    return a, b


def _group_scan(a, b, reverse):
    n, C = a.shape
    a = a.reshape(n // SUBLANES, SUBLANES, C)
    b = b.reshape(n // SUBLANES, SUBLANES, C)
    row = lax.broadcasted_iota(jnp.int32, a.shape, 1)
    for k in (1, 2, 4):
        shift = SUBLANES - k if reverse else k
        mask = (row < SUBLANES - k) if reverse else (row >= k)
        a_s = pltpu.roll(a, shift, 1)
        b_s = pltpu.roll(b, shift, 1)
        b = jnp.where(mask, a * b_s + b, b)
        a = jnp.where(mask, a * a_s, a)
    return a.reshape(n, C), b.reshape(n, C)


def _chunk_scan(a, b, h_in, reverse, acum_ref, bcum_ref, out_ref):
    n, C = a.shape
    ac, bc = _group_scan(a, b, reverse)
    acum_ref[...] = ac
    bcum_ref[...] = bc
    ng = n // SUBLANES

    def body(g, hb):
        idx = (ng - 1 - g) if reverse else g
        sl = pl.ds(pl.multiple_of(idx * SUBLANES, SUBLANES), SUBLANES)
        h = acum_ref[sl, :] * hb + bcum_ref[sl, :]
        out_ref[sl, :] = h
        edge = h[0:1] if reverse else h[SUBLANES - 1:SUBLANES]
        return jnp.broadcast_to(edge, (SUBLANES, C))

    return lax.fori_loop(0, ng, body, h_in, unroll=8)


def _lru_ctx_kernel(xr_ref, gr_ref, cw_ref, w_ref, vec_ref, o_ref, hend_ref, acum_ref, bcum_ref, hf_ref, hb_ref):
    xr = xr_ref[...]
    zeros8 = jnp.zeros((SUBLANES, xr.shape[1]), F32)
    xl = _conv4(zeros8, xr, zeros8, cw_ref[...])
    a, b = _lru_coeffs(xl, w_ref[0], vec_ref[0])
    hf_end = _chunk_scan(a, b, zeros8, False, acum_ref, bcum_ref, hf_ref)
    a, b = _lru_coeffs(xl, w_ref[1], vec_ref[1])
    hb_end = _chunk_scan(a, b, zeros8, True, acum_ref, bcum_ref, hb_ref)
    o_ref[...] = ((hf_ref[...] + hb_ref[...]) * jax.nn.gelu(gr_ref[...])).astype(BF16)
    row = lax.broadcasted_iota(jnp.int32, hf_end.shape, 0)
    hend_ref[...] = jnp.where(row == 0, hf_end, hb_end)


def _lru_ctx(xr, gr, cw, w, vec, *, B, S, CTX):
    C = xr.shape[1]
    ctx0 = (B * S) // CTX
    spec = pl.BlockSpec((CTX, C), lambda b: (ctx0 + b, 0))
    return pl.pallas_call(
        _lru_ctx_kernel,
        grid=(B,),
        in_specs=[spec, spec,
                  pl.BlockSpec((SUBLANES, C), lambda b: (0, 0)),
                  pl.BlockSpec((2, C, 2 * C), lambda b: (0, 0, 0)),
                  pl.BlockSpec((2, SUBLANES, C), lambda b: (0, 0, 0))],
        out_specs=[pl.BlockSpec((CTX, C), lambda b: (b, 0)),
                   pl.BlockSpec((None, SUBLANES, C), lambda b: (b, 0, 0))],
        out_shape=[jax.ShapeDtypeStruct((B * CTX, C), BF16),
                   jax.ShapeDtypeStruct((B, SUBLANES, C), F32)],
        scratch_shapes=[pltpu.VMEM((CTX, C), F32)] * 4,
        compiler_params=_cparams(("parallel",)),
        name="lru_ctx",
    )(xr, gr, cw, w, vec)


def _lru_lat_kernel(*refs, reverse, nch):
    if reverse:
        xr_ref, xp_ref, xn_ref, hend_ref, cw_ref, w_ref, vec_ref, o_ref, carry_ref, acum_ref, bcum_ref = refs
    else:
        (xr_ref, xp_ref, xn_ref, hend_ref, cw_ref, w_ref, vec_ref, gr_ref, hb_ref,
         o_ref, carry_ref, acum_ref, bcum_ref, hf_ref) = refs
    i = pl.program_id(1)
    c = (nch - 1 - i) if reverse else i
    C = xr_ref.shape[1]

    @pl.when(i == 0)
    def _():
        row = hend_ref[1:2, :] if reverse else hend_ref[0:1, :]
        carry_ref[...] = jnp.broadcast_to(row, (SUBLANES, C))

    xp = jnp.where(c == 0, 0.0, xp_ref[...])
    xn = jnp.where(c == nch - 1, 0.0, xn_ref[...])
    xl = _conv4(xp, xr_ref[...], xn, cw_ref[...])
    a, b = _lru_coeffs(xl, w_ref[...], vec_ref[...])
    if reverse:
        carry_ref[...] = _chunk_scan(a, b, carry_ref[...], True, acum_ref, bcum_ref, o_ref)
    else:
        carry_ref[...] = _chunk_scan(a, b, carry_ref[...], False, acum_ref, bcum_ref, hf_ref)
        o_ref[...] = ((hf_ref[...] + hb_ref[...]) * jax.nn.gelu(gr_ref[...])).astype(BF16)


def _lru_lat(xr, gr, hb, hend, cw, w, vec, *, reverse, B, S, tt):
    C = xr.shape[1]
    nch = S // tt
    d = 1 if reverse else 0
    chunk = (lambda i: nch - 1 - i) if reverse else (lambda i: i)
    nblk8 = (B * S) // SUBLANES
    main = pl.BlockSpec((tt, C), lambda b, i: (b * nch + chunk(i), 0))
    in_specs = [
        main,
        pl.BlockSpec((SUBLANES, C), lambda b, i: (jnp.maximum((b * nch + chunk(i)) * (tt // SUBLANES) - 1, 0), 0)),
        pl.BlockSpec((SUBLANES, C), lambda b, i: (jnp.minimum((b * nch + chunk(i) + 1) * (tt // SUBLANES), nblk8 - 1), 0)),
        pl.BlockSpec((None, SUBLANES, C), lambda b, i: (b, 0, 0)),
        pl.BlockSpec((SUBLANES, C), lambda b, i: (0, 0)),
        pl.BlockSpec((None, C, 2 * C), lambda b, i: (d, 0, 0)),
        pl.BlockSpec((None, SUBLANES, C), lambda b, i: (d, 0, 0))]
    args = [xr, xr, xr, hend, cw, w, vec]
    scratch = [pltpu.VMEM((SUBLANES, C), F32), pltpu.VMEM((tt, C), F32), pltpu.VMEM((tt, C), F32)]
    if reverse:
        out_dtype = F32
    else:
        in_specs += [main, main]
        args += [gr, hb]
        scratch.append(pltpu.VMEM((tt, C), F32))
        out_dtype = BF16
    return pl.pallas_call(
        functools.partial(_lru_lat_kernel, reverse=reverse, nch=nch),
        grid=(B, nch),
        in_specs=in_specs,
        out_specs=main,
        out_shape=jax.ShapeDtypeStruct((B * S, C), out_dtype),
        scratch_shapes=scratch,
        compiler_params=_cparams(("parallel", "arbitrary")),
        name="lru_bwd" if reverse else "lru_fwd",
    )(*args)


def _fno_w_kernel(fw_ref, cs_ref, o_ref):
    fw = fw_ref[...]
    o_ref[0] = jnp.dot(cs_ref[0], fw, preferred_element_type=F32, precision=HIGHEST).astype(BF16)
    o_ref[1] = jnp.dot(cs_ref[1], fw, preferred_element_type=F32, precision=HIGHEST).astype(BF16)


def _fno_weights(fw_bd, cs_bd):
    L, C, _ = fw_bd.shape
    return pl.pallas_call(
        _fno_w_kernel,
        grid=(L,),
        in_specs=[pl.BlockSpec((None, C, C), lambda l: (l, 0, 0)),
                  pl.BlockSpec((2, C, C), lambda l: (0, 0, 0))],
        out_specs=pl.BlockSpec((None, 2, C, C), lambda l: (l, 0, 0, 0)),
        out_shape=jax.ShapeDtypeStruct((L, 2, C, C), BF16),
        compiler_params=_cparams(("parallel",)),
        name="fno_w",
    )(fw_bd, cs_bd)


def _fno_a_kernel(x_ref, w_ref, m_ref, o_ref):
    n1, s, C = x_ref.shape
    x = x_ref[...].reshape(n1 * s, C).astype(BF16)
    u = jnp.dot(x, w_ref[0], preferred_element_type=F32).astype(BF16)
    v = jnp.dot(x, w_ref[1], preferred_element_type=F32).astype(BF16)
    uv = jnp.concatenate([u, v], axis=0)
    a = jnp.dot(m_ref[...], uv, preferred_element_type=F32).astype(BF16)
    rows = 2 * s
    for k1 in range(n1):
        o_ref[:, k1 * C:(k1 + 1) * C] = a[k1 * rows:(k1 + 1) * rows, :]


def _fno_c_kernel(a_ref, f_ref, o_ref, *, scale):
    o_ref[...] = (jnp.dot(f_ref[...], a_ref[...], preferred_element_type=F32) * scale).astype(BF16)


def _fno_lat(f, w_l, m_a, f_c, *, B, S):
    T, C = f.shape
    n1 = FNO_N1
    n2 = S // n1
    nj = n2 // SUBLANES
    f4 = f.reshape(T // n2, nj, SUBLANES, C)
    rows_a = 2 * SUBLANES * n1
    a2 = pl.pallas_call(
        _fno_a_kernel,
        grid=(nj, B),
        in_specs=[pl.BlockSpec((n1, None, SUBLANES, C), lambda j, b: (b, j, 0, 0)),
                  pl.BlockSpec((2, C, C), lambda j, b: (0, 0, 0)),
                  pl.BlockSpec((None, rows_a, rows_a), lambda j, b: (j, 0, 0))],
        out_specs=pl.BlockSpec((None, 2 * SUBLANES, n1 * C), lambda j, b: (b, j, 0)),
        out_shape=jax.ShapeDtypeStruct((B, 2 * n2, n1 * C), BF16),
        compiler_params=_cparams(("parallel", "parallel")),
        name="fno_a",
    )(f4, w_l, m_a)
    ncol = n1 * C
    tc = min(ncol, 2048)
    scale = 1.0 / math.sqrt(S * (C // FNO_GROUPS))
    out = pl.pallas_call(
        functools.partial(_fno_c_kernel, scale=scale),
        grid=(B, ncol // tc),
        in_specs=[pl.BlockSpec((None, 2 * n2, tc), lambda b, t: (b, 0, t)),
                  pl.BlockSpec((n2, 2 * n2), lambda b, t: (0, 0))],
        out_specs=pl.BlockSpec((None, n2, tc), lambda b, t: (b, 0, t)),
        out_shape=jax.ShapeDtypeStruct((B, n2, ncol), BF16),
        compiler_params=_cparams(("parallel", "parallel")),
        name="fno_c",
    )(a2, f_c)
    return out.reshape(B * S, C)


def _fno_ctx_kernel(x_ref, w_ref, f_ref, o_ref, *, scale):
    x = x_ref[...].astype(BF16)
    u = jnp.dot(x, w_ref[0], preferred_element_type=F32).astype(BF16)
    v = jnp.dot(x, w_ref[1], preferred_element_type=F32).astype(BF16)
    uv = jnp.concatenate([u, v], axis=0)
    o_ref[...] = (jnp.dot(f_ref[...], uv, preferred_element_type=F32) * scale).astype(BF16)


def _fno_ctx(f, w_l, f_ctx, *, B, S, CTX):
    C = f.shape[1]
    ctx0 = (B * S) // CTX
    scale = 1.0 / math.sqrt(CTX * (C // FNO_GROUPS))
    return pl.pallas_call(
        functools.partial(_fno_ctx_kernel, scale=scale),
        grid=(B,),
        in_specs=[pl.BlockSpec((CTX, C), lambda b: (ctx0 + b, 0)),
                  pl.BlockSpec((2, C, C), lambda b: (0, 0, 0)),
                  pl.BlockSpec((CTX, 2 * CTX), lambda b: (0, 0))],
        out_specs=pl.BlockSpec((CTX, C), lambda b: (b, 0)),
        out_shape=jax.ShapeDtypeStruct((B * CTX, C), BF16),
        compiler_params=_cparams(("parallel",)),
        name="fno_ctx",
    )(f, w_l, f_ctx)


def _dft_constants(S, CTX, C):
    gd = C // FNO_GROUPS
    n1, n2 = FNO_N1, S // FNO_N1
    nj = n2 // SUBLANES
    idx = jnp.arange(gd, dtype=jnp.int32)
    ang = (2.0 * math.pi / gd) * ((idx[:, None] * idx[None, :]) % gd).astype(F32)
    eye_g = jnp.eye(FNO_GROUPS, dtype=F32)
    cs_bd = jnp.stack([jnp.kron(eye_g, jnp.cos(ang)), jnp.kron(eye_g, jnp.sin(ang))])
    k1 = jnp.arange(n1, dtype=jnp.int32)[:, None, None]
    nn1 = jnp.arange(n1, dtype=jnp.int32)[None, :, None]
    nn2 = jnp.arange(n2, dtype=jnp.int32)[None, None, :]
    ph = (2.0 * math.pi / S) * ((k1 * (n2 * nn1 + nn2)) % S).astype(F32)
    cph = jnp.cos(ph).reshape(n1, n1, nj, SUBLANES)
    sph = jnp.sin(ph).reshape(n1, n1, nj, SUBLANES)
    eye_s = jnp.eye(SUBLANES, dtype=F32)

    def block(t):
        return jnp.einsum('knjs,st->jsknt', t, eye_s)

    cb, sb = block(cph), block(sph)
    top = jnp.stack([cb, -sb], axis=3)
    bot = jnp.stack([-sb, -cb], axis=3)
    m_a = jnp.stack([top, bot], axis=1)
    m_a = m_a.transpose(0, 3, 1, 2, 4, 5, 6)
    m_a = m_a.reshape(nj, 2 * SUBLANES * n1, 2 * n1 * SUBLANES).astype(BF16)
    i2 = jnp.arange(n2, dtype=jnp.int32)
    a2 = (2.0 * math.pi / n2) * ((i2[:, None] * i2[None, :]) % n2).astype(F32)
    f_c = jnp.stack([jnp.cos(a2).reshape(n2, nj, SUBLANES), jnp.sin(a2).reshape(n2, nj, SUBLANES)], axis=2)
    f_c = f_c.reshape(n2, 2 * n2).astype(BF16)
    ic = jnp.arange(CTX, dtype=jnp.int32)
    ac = (2.0 * math.pi / CTX) * ((ic[:, None] * ic[None, :]) % CTX).astype(F32)
    f_ctx = jnp.concatenate([jnp.cos(ac), -jnp.sin(ac)], axis=1).astype(BF16)
    return cs_bd, m_a, f_c, f_ctx


def _outproj_kernel(*refs, alpha, nlat, has_ctx, na_w, lru_w):
    if has_ctx:
        x_ref, m_ref, nal, lrul, fnol, nac, lruc, fnoc, w_ref, g_ref, b_ref, o_ref = refs
        is_lat = pl.program_id(0) < nlat
        na = jnp.where(is_lat, nal[...], nac[...])
        lru = jnp.where(is_lat, lrul[...], lruc[...])
        fno = jnp.where(is_lat, fnol[...], fnoc[...])
    else:
        x_ref, m_ref, nal, lrul, fnol, w_ref, g_ref, b_ref, o_ref = refs
        na, lru, fno = nal[...], lrul[...], fnol[...]
    y = jnp.dot(na, w_ref[0:na_w, :], preferred_element_type=F32)
    y = y + jnp.dot(lru, w_ref[na_w:na_w + lru_w, :], preferred_element_type=F32)
    y = y + jnp.dot(fno, w_ref[na_w + lru_w:, :], preferred_element_type=F32)
    m = m_ref[...]
    z = alpha * x_ref[...] + m[5:6] * y
    o_ref[...] = _layer_norm(z, g_ref[...], b_ref[...])


def _outproj(x_all, mods_l, lat, ctx, w_out, ln_g, ln_b, *, l, alpha, tm, tiles_per_batch, nbatch, nlat, ntiles):
    D = x_all.shape[1]
    has_ctx = ctx is not None
    row_idx = lambda i: jnp.minimum(i // tiles_per_batch, nbatch)
    lat_specs = [pl.BlockSpec((tm, a.shape[1]), lambda i: (jnp.minimum(i, nlat - 1), 0)) for a in lat]
    ctx_specs, ctx_args = [], []
    if has_ctx:
        nctx = ntiles - nlat
        ctx_specs = [pl.BlockSpec((tm, a.shape[1]), lambda i: (jnp.clip(i - nlat, 0, nctx - 1), 0)) for a in ctx]
        ctx_args = list(ctx)
    return pl.pallas_call(
        functools.partial(_outproj_kernel, alpha=alpha, nlat=nlat, has_ctx=has_ctx,
                          na_w=lat[0].shape[1], lru_w=lat[1].shape[1]),
        grid=(ntiles,),
        in_specs=[pl.BlockSpec((tm, D), lambda i: (i, 0)),
                  pl.BlockSpec((None, None, N_MOD, D), lambda i: (l, row_idx(i), 0, 0))]
        + lat_specs + ctx_specs
        + [pl.BlockSpec((None, D, D), lambda i: (l, 0, 0), pipeline_mode=pl.Buffered(1)),
           pl.BlockSpec((None, 1, D), lambda i: (l, 0, 0)),
           pl.BlockSpec((None, 1, D), lambda i: (l, 0, 0))],
        out_specs=pl.BlockSpec((tm, D), lambda i: (i, 0)),
        out_shape=jax.ShapeDtypeStruct((ntiles * tm, D), F32),
        compiler_params=_cparams(("parallel",)),
        name="outproj",
    )(x_all, mods_l, *lat, *ctx_args, w_out, ln_g, ln_b)


def _block_diag(w):
    G, n = w.shape[-3], w.shape[-1]
    eye = jnp.eye(G, dtype=w.dtype)
    out = jnp.einsum('...gij,gh->...gihj', w, eye)
    return out.reshape(w.shape[:-3] + (G * n, G * n))


def kernel(x, c, ctx, c_ctx, w_ada, b_ada, ln_g, ln_b, ff1_gate, ff1_up, ff1_down, ff2_gate, ff2_up, ff2_down,
           w_in, w_out, na_rpb, lru_conv_w, lru_conv_b, lru_wa, lru_ba, lru_wx, lru_bx, lru_lambda, fno_w):
    B, S, D = x.shape
    CTX = ctx.shape[1]
    L = w_ada.shape[0]
    na_w = na_rpb.shape[1] * HEAD_DIM
    lru_w = lru_conv_w.shape[2]
    alpha = float((2 * L) ** 0.25)
    tm = min(1024, B * CTX)
    tiles_per_batch = S // tm
    nlat = (B * S) // tm
    ntiles = nlat + (B * CTX) // tm
    assert B < 16 and S % tm == 0 and (B * CTX) % tm == 0 and S % (GRID_W * NA_KH) == 0

    cc = jnp.zeros((16, D), F32).at[:B].set(c).at[B].set(c_ctx)
    mods = _ada_params(cc, w_ada, b_ada)

    bf = lambda w: w.astype(BF16)
    ff1 = (bf(ff1_gate), bf(ff1_up), bf(ff1_down))
    ff2 = (bf(ff2_gate), bf(ff2_up), bf(ff2_down))
    w_in_b, w_out_b = bf(w_in), bf(w_out)
    ln_g4 = ln_g[:, :, None, :]
    ln_b4 = ln_b[:, :, None, :]
    wb = jax.vmap(_na_bias_tables)(na_rpb)
    lru_gate_w = bf(jnp.concatenate([_block_diag(lru_wa), _block_diag(lru_wx)], axis=-1))
    zrow = jnp.zeros((L, 2, SUBLANES - 3, lru_w), F32)
    lru_vec = jnp.concatenate([lru_ba[:, :, None], lru_bx[:, :, None], lru_lambda[:, :, None], zrow], axis=2)
    lru_cw = jnp.concatenate([lru_conv_w, lru_conv_b[:, None], jnp.zeros((L, SUBLANES - LRU_CONV - 1, lru_w), F32)], axis=1)
    cs_bd, m_a, f_c, f_ctx = _dft_constants(S, CTX, D - na_w - lru_w)
    fno_wcs = _fno_weights(_block_diag(fno_w), cs_bd)

    common = dict(tm=tm, tiles_per_batch=tiles_per_batch, nbatch=B)
    xs = (x.reshape(B * S, D), ctx.reshape(B * CTX, D))
    for l in range(L):
        last = l == L - 1
        x1 = _ffn(xs, mods, *ff1, ln_g4[:, 0], ln_b4[:, 0], j=0, l=l, alpha=alpha, nlat=nlat, ntiles=ntiles, **common)
        q, k, v, xr, gr, f = _inproj(x1, mods, w_in_b, l=l, na_w=na_w, lru_w=lru_w, **common)
        na_lat = _attn_lat(q, k, v, wb[l], B=B, S=S, CTX=CTX)
        lru_ctx, hend = _lru_ctx(xr, gr, lru_cw[l], lru_gate_w[l], lru_vec[l], B=B, S=S, CTX=CTX)
        tt = min(S, 1024)
        hb = _lru_lat(xr, None, None, hend, lru_cw[l], lru_gate_w[l], lru_vec[l], reverse=True, B=B, S=S, tt=tt)
        lru_lat = _lru_lat(xr, gr, hb, hend, lru_cw[l], lru_gate_w[l], lru_vec[l], reverse=False, B=B, S=S, tt=tt)
        fno_lat = _fno_lat(f, fno_wcs[l], m_a, f_c, B=B, S=S)
        if last:
            ctx_parts = None
            nt = nlat
        else:
            ctx_parts = (_attn_ctx(q, k, v, B=B, S=S, CTX=CTX), lru_ctx, _fno_ctx(f, fno_wcs[l], f_ctx, B=B, S=S, CTX=CTX))
            nt = ntiles
        x2 = _outproj(x1, mods, (na_lat, lru_lat, fno_lat), ctx_parts, w_out_b, ln_g4[:, 1], ln_b4[:, 1],
                      l=l, alpha=alpha, nlat=nlat, ntiles=nt, **common)
        x3 = _ffn((x2,), mods, *ff2, ln_g4[:, 2], ln_b4[:, 2], j=2, l=l, alpha=alpha, nlat=nlat, ntiles=nt, **common)
        xs = (x3,)
    return xs[0].reshape(B, S, D)
```

```python
import functools
import math

import numpy as np
import jax
import jax.numpy as jnp
from jax import lax
from jax.experimental import pallas as pl
from jax.experimental.pallas import tpu as pltpu

F32 = jnp.float32
BF16 = jnp.bfloat16
HIGHEST = lax.Precision.HIGHEST

HEAD_DIM = 64
GRID_W = 64
NA_KH = 8
NA_KW = 16
LRU_HEADS = 4
LRU_CONV = 4
LRU_C = 8.0
FNO_GROUPS = 4
N_MOD = 9
MACARON = 0.5
LN_EPS = 1e-5
MASK_BIAS = -1e30
LOG2E = math.log2(math.e)
SUBLANES = 8
FNO_N1 = 64
VMEM_LIMIT = 56 * 1024 * 1024


def _cparams(sem):
    return pltpu.CompilerParams(dimension_semantics=sem, vmem_limit_bytes=VMEM_LIMIT)


def _layer_norm(z, g, b):
    mu = jnp.mean(z, axis=-1, keepdims=True)
    zc = z - mu
    var = jnp.mean(zc * zc, axis=-1, keepdims=True)
    return zc * lax.rsqrt(var + LN_EPS) * g + b


def _ada_kernel(c_ref, w_ref, b_ref, o_ref):
    c = c_ref[...]
    a = c * jax.nn.sigmoid(c)
    o_ref[...] = jnp.dot(a, w_ref[...], preferred_element_type=F32, precision=HIGHEST) + b_ref[...]


def _ada_params(cc, w_ada, b_ada):
    L, D, ND = w_ada.shape
    tn = D
    out = pl.pallas_call(
        _ada_kernel,
        grid=(L, ND // tn),
        in_specs=[pl.BlockSpec((16, D), lambda l, n: (0, 0)),
                  pl.BlockSpec((None, D, tn), lambda l, n: (l, 0, n)),
                  pl.BlockSpec((None, 1, tn), lambda l, n: (l, 0, n))],
        out_specs=pl.BlockSpec((None, 16, tn), lambda l, n: (l, 0, n)),
        out_shape=jax.ShapeDtypeStruct((L, 16, ND), F32),
        compiler_params=_cparams(("parallel", "parallel")),
        name="ada",
    )(cc, w_ada, b_ada.reshape(L, 1, ND))
    return out.reshape(L, 16, N_MOD, D)


def _ffn_kernel(*refs, j, tf, alpha, nlat, mode):
    m = refs[2 if mode == "two_x" else 1][...]
    if mode == "two_x":
        xl_ref, xc_ref, _, wg_ref, wu_ref, wd_ref, g_ref, b_ref, o_ref, h_ref, a_ref = refs
        x = jnp.where(pl.program_id(0) < nlat, xl_ref[...], xc_ref[...])
    elif mode == "plain":
        x_ref, _, wg_ref, wu_ref, wd_ref, g_ref, b_ref, o_ref, h_ref, a_ref = refs
        x = x_ref[...]
    else:
        if mode == "mix_ctx":
            (x_ref, _, na_ref, lru_ref, fl_ref, fc_ref, wo_ref, g1_ref, b1_ref,
             wg_ref, wu_ref, wd_ref, g_ref, b_ref, o_ref, h_ref, a_ref) = refs
            fno = jnp.where(pl.program_id(0) < nlat, fl_ref[...], fc_ref[...])
        else:
            (x_ref, _, na_ref, lru_ref, fl_ref, wo_ref, g1_ref, b1_ref,
             wg_ref, wu_ref, wd_ref, g_ref, b_ref, o_ref, h_ref, a_ref) = refs
            fno = fl_ref[...]
        na_w, lru_w = na_ref.shape[1], lru_ref.shape[1]
        y = jnp.dot(na_ref[...], wo_ref[0:na_w, :], preferred_element_type=F32)
        y = y + jnp.dot(lru_ref[...], wo_ref[na_w:na_w + lru_w, :], preferred_element_type=F32)
        y = y + jnp.dot(fno, wo_ref[na_w + lru_w:, :], preferred_element_type=F32)
        o_ref[...] = _layer_norm(alpha * x_ref[...] + m[5:6] * y, g1_ref[...], b1_ref[...])
        x = o_ref[...]
    shift, scale, gate = m[3 * j:3 * j + 1], m[3 * j + 1:3 * j + 2], m[3 * j + 2:3 * j + 3]
    h_ref[...] = (x * (1.0 + scale) + shift).astype(BF16)
    for c in range(wg_ref.shape[1] // tf):
        cols = slice(c * tf, (c + 1) * tf)
        h = h_ref[...]
        g = jnp.dot(h, wg_ref[:, cols], preferred_element_type=F32)
        u = jnp.dot(h, wu_ref[:, cols], preferred_element_type=F32)
        a_ref[:, cols] = (g * jax.nn.sigmoid(g) * u).astype(BF16)
    y = jnp.dot(a_ref[...], wd_ref[...], preferred_element_type=F32)
    if mode.startswith("mix"):
        x = o_ref[...]
    z = alpha * x + (MACARON * gate) * y
    o_ref[...] = _layer_norm(z, g_ref[...], b_ref[...])


def _ffn(xs, mods_l, wg, wu, wd, ln_g, ln_b, *, j, l, alpha, tm, tiles_per_batch, nlat, ntiles, nbatch, mix=None):
    two_x = len(xs) == 2
    mode = "two_x" if two_x else "plain"
    D = xs[0].shape[1]
    Fd = wg.shape[2]
    tf = 256
    row_idx = lambda i: jnp.minimum(i // tiles_per_batch, nbatch)
    if two_x:
        nctx = ntiles - nlat
        x_specs = [pl.BlockSpec((tm, D), lambda i: (jnp.minimum(i, nlat - 1), 0)),
                   pl.BlockSpec((tm, D), lambda i: (jnp.clip(i - nlat, 0, nctx - 1), 0))]
    else:
        x_specs = [pl.BlockSpec((tm, D), lambda i: (i, 0))]
    const = pl.Buffered(1)
    mix_specs, mix_args = [], []
    if mix is not None:
        na, lru, fno_l, fno_c, w_out, g1, b1 = mix
        mode = "mix" if fno_c is None else "mix_ctx"
        tile = lambda a: pl.BlockSpec((tm, a.shape[1]), lambda i: (i, 0))
        mix_specs = [tile(na), tile(lru), pl.BlockSpec((tm, fno_l.shape[1]), lambda i: (jnp.minimum(i, nlat - 1), 0))]
        mix_args = [na, lru, fno_l]
        if fno_c is not None:
            mix_specs.append(pl.BlockSpec((tm, fno_c.shape[1]), lambda i: (jnp.clip(i - nlat, 0, ntiles - nlat - 1), 0)))
            mix_args.append(fno_c)
        mix_specs += [pl.BlockSpec((None, D, D), lambda i: (l, 0, 0), pipeline_mode=const),
                      pl.BlockSpec((None, 1, D), lambda i: (l, 0, 0)),
                      pl.BlockSpec((None, 1, D), lambda i: (l, 0, 0))]
        mix_args += [w_out, g1, b1]
    return pl.pallas_call(
        functools.partial(_ffn_kernel, j=j, tf=tf, alpha=alpha, nlat=nlat, mode=mode),
        grid=(ntiles,),
        in_specs=x_specs + [
            pl.BlockSpec((None, None, N_MOD, D), lambda i: (l, row_idx(i), 0, 0))] + mix_specs + [
            pl.BlockSpec((None, D, Fd), lambda i: (l, 0, 0), pipeline_mode=const),
            pl.BlockSpec((None, D, Fd), lambda i: (l, 0, 0), pipeline_mode=const),
            pl.BlockSpec((None, Fd, D), lambda i: (l, 0, 0), pipeline_mode=const),
            pl.BlockSpec((None, 1, D), lambda i: (l, 0, 0)),
            pl.BlockSpec((None, 1, D), lambda i: (l, 0, 0))],
        out_specs=pl.BlockSpec((tm, D), lambda i: (i, 0)),
        out_shape=jax.ShapeDtypeStruct((ntiles * tm, D), F32),
        scratch_shapes=[pltpu.VMEM((tm, D), BF16), pltpu.VMEM((tm, Fd), BF16)],
        compiler_params=_cparams(("parallel",)),
        name=f"ffn{j}",
    )(*xs, mods_l, *mix_args, wg, wu, wd, ln_g, ln_b)


def _inproj_kernel(x_ref, m_ref, w_ref, q_ref, k_ref, v_ref, xr_ref, gr_ref, f_ref, *, na_w, lru_w):
    m = m_ref[...]
    h = (x_ref[...] * (1.0 + m[4:5]) + m[3:4]).astype(BF16)

    def proj(lo, width):
        return jnp.dot(h, w_ref[:, lo:lo + width], preferred_element_type=F32)

    q_ref[...] = (proj(0, na_w) * (HEAD_DIM ** -0.5 * LOG2E)).astype(BF16)
    k_ref[...] = proj(na_w, na_w).astype(BF16)
    v_ref[...] = proj(2 * na_w, na_w).astype(BF16)
    xr_ref[...] = proj(3 * na_w, lru_w)
    gr_ref[...] = proj(3 * na_w + lru_w, lru_w)
    f_ref[...] = proj(3 * na_w + 2 * lru_w, w_ref.shape[1] - 3 * na_w - 2 * lru_w)


def _inproj(x_all, mods_l, w_in, *, l, tm, tiles_per_batch, nbatch, na_w, lru_w):
    T, D = x_all.shape
    cols = w_in.shape[2]
    fno_w = cols - 3 * na_w - 2 * lru_w
    row_idx = lambda i: jnp.minimum(i // tiles_per_batch, nbatch)
    widths = [(na_w, BF16)] * 3 + [(lru_w, F32)] * 2 + [(fno_w, F32)]
    return pl.pallas_call(
        functools.partial(_inproj_kernel, na_w=na_w, lru_w=lru_w),
        grid=(T // tm,),
        in_specs=[pl.BlockSpec((tm, D), lambda i: (i, 0)),
                  pl.BlockSpec((None, None, N_MOD, D), lambda i: (l, row_idx(i), 0, 0)),
                  pl.BlockSpec((None, D, cols), lambda i: (l, 0, 0), pipeline_mode=pl.Buffered(1))],
        out_specs=[pl.BlockSpec((tm, w), lambda i: (i, 0)) for w, _ in widths],
        out_shape=[jax.ShapeDtypeStruct((T, w), dt) for w, dt in widths],
        compiler_params=_cparams(("parallel",)),
        name="inproj",
    )(x_all, mods_l, w_in)


def _na_bias_tables(rpb):
    H = rpb.shape[0]
    W = GRID_W
    c = np.arange(W)
    col0 = np.clip(c - NA_KW // 2, 0, W - NA_KW)
    jj = np.arange(W)
    valid = (jj[None, :] >= col0[:, None]) & (jj[None, :] < col0[:, None] + NA_KW)
    dc = np.clip(jj[None, :] - c[:, None] + (NA_KW - 1), 0, 2 * NA_KW - 2)
    bt = jnp.where(valid[None, None], rpb[:, :, dc] * LOG2E, MASK_BIAS)
    case = np.arange(NA_KH)[:, None]
    i = np.arange(NA_KH)[None, :]
    wb = bt[:, i - case + (NA_KH - 1)]
    wb = wb.transpose(0, 1, 3, 2, 4).reshape(H // 2, 2, NA_KH, W, NA_KH * W)
    wb = wb.transpose(0, 2, 1, 3, 4).reshape(H // 2, NA_KH, 2 * W, NA_KH * W)
    return wb.astype(F32)


def _masked_heads(q):
    lane = lax.broadcasted_iota(jnp.int32, q.shape, 1)
    qf = q.astype(F32)
    return (jnp.where(lane < HEAD_DIM, qf, 0.0).astype(BF16),
            jnp.where(lane >= HEAD_DIM, qf, 0.0).astype(BF16))


def _qk(q, k):
    return lax.dot_general(q, k, (((1,), (1,)), ((), ())), preferred_element_type=F32)


def _attn_kernel(q_ref, k_ref, v_ref, kc_ref, vc_ref, wb_ref, o_ref, *, R, G, rows):
    jb = pl.program_id(2)
    kc = kc_ref[...]
    vc = vc_ref[...]
    W = GRID_W
    nwin = NA_KH * W
    lane = lax.broadcasted_iota(jnp.int32, (W, 2 * HEAD_DIM), 1)

    def group(g, carry):
        qoff = pl.multiple_of(g * (R * W), R * W)
        qcat = []
        for rr in range(R):
            qe, qo = _masked_heads(q_ref[pl.ds(qoff + rr * W, W), :])
            qcat += [qe, qo]
        s_ctx_all = _qk(jnp.concatenate(qcat, axis=0), kc)
        o_loc, p_ctx, dens = [], [], []
        for rr in range(R):
            r = (jb * G + g) * R + rr
            rs = jnp.clip(r - NA_KH // 2, 0, rows - NA_KH)
            koff = pl.multiple_of(rs * W, W)
            kw = k_ref[pl.ds(koff, nwin), :]
            vw = v_ref[pl.ds(koff, nwin), :]
            qc = jnp.concatenate(qcat[2 * rr:2 * rr + 2], axis=0)
            s_loc = _qk(qc, kw) + wb_ref[r - rs]
            s_ctx = s_ctx_all[rr * 2 * W:(rr + 1) * 2 * W]
            m = jnp.maximum(jnp.max(s_loc, axis=-1, keepdims=True), jnp.max(s_ctx, axis=-1, keepdims=True))
            p_loc = jnp.exp2(s_loc - m)
            pc = jnp.exp2(s_ctx - m)
            dens.append(jnp.sum(p_loc, axis=-1, keepdims=True) + jnp.sum(pc, axis=-1, keepdims=True))
            o_loc.append(jnp.dot(p_loc.astype(BF16), vw, preferred_element_type=F32))
            p_ctx.append(pc.astype(BF16))
        o_ctx_all = jnp.dot(jnp.concatenate(p_ctx, axis=0), vc, preferred_element_type=F32)
        outs = []
        for rr in range(R):
            o = (o_loc[rr] + o_ctx_all[rr * 2 * W:(rr + 1) * 2 * W]) / dens[rr]
            outs.append(jnp.where(lane < HEAD_DIM, o[:W], o[W:]).astype(BF16))
        o_ref[pl.ds(qoff, R * W), :] = jnp.concatenate(outs, axis=0)
        return carry

    lax.fori_loop(0, G, group, 0)


def _attn_lat(q, k, v, wb_l, *, B, S, CTX):
    T, na_w = q.shape
    npair = na_w // (2 * HEAD_DIM)
    rows = S // GRID_W
    R = 8
    G = 4 if rows % (4 * R) == 0 else 1
    nrb = rows // (R * G)
    pw = 2 * HEAD_DIM
    ctx0 = (B * S) // CTX
    qblk = R * G * GRID_W
    return pl.pallas_call(
        functools.partial(_attn_kernel, R=R, G=G, rows=rows),
        grid=(B, npair, nrb),
        in_specs=[pl.BlockSpec((qblk, pw), lambda b, p, j: (b * nrb + j, p)),
                  pl.BlockSpec((S, pw), lambda b, p, j: (b, p)),
                  pl.BlockSpec((S, pw), lambda b, p, j: (b, p)),
                  pl.BlockSpec((CTX, pw), lambda b, p, j: (ctx0 + b, p)),
                  pl.BlockSpec((CTX, pw), lambda b, p, j: (ctx0 + b, p)),
                  pl.BlockSpec((None, NA_KH, 2 * GRID_W, NA_KH * GRID_W), lambda b, p, j: (p, 0, 0, 0))],
        out_specs=pl.BlockSpec((qblk, pw), lambda b, p, j: (b * nrb + j, p)),
        out_shape=jax.ShapeDtypeStruct((T, na_w), BF16),
        compiler_params=_cparams(("parallel", "parallel", "parallel")),
        name="attn_lat",
    )(q, k, v, k, v, wb_l)


def _attn_ctx_kernel(q_ref, k_ref, v_ref, na_hbm_ref, o_ref):
    del na_hbm_ref
    k = k_ref[...]
    v = v_ref[...]
    lane = lax.broadcasted_iota(jnp.int32, o_ref.shape, 1)
    outs = []
    for qm in _masked_heads(q_ref[...]):
        s = _qk(qm, k)
        p = jnp.exp2(s - jnp.max(s, axis=-1, keepdims=True))
        den = jnp.sum(p, axis=-1, keepdims=True)
        outs.append(jnp.dot(p.astype(BF16), v, preferred_element_type=F32) / den)
    o_ref[...] = jnp.where(lane < HEAD_DIM, outs[0], outs[1]).astype(BF16)


def _attn_ctx(q, k, v, na, *, B, S, CTX):
    T, na_w = q.shape
    pw = 2 * HEAD_DIM
    ctx0 = (B * S) // CTX
    spec = pl.BlockSpec((CTX, pw), lambda b, p: (ctx0 + b, p))
    return pl.pallas_call(
        _attn_ctx_kernel,
        grid=(B, na_w // pw),
        in_specs=[spec, spec, spec, pl.BlockSpec(memory_space=pl.ANY)],
        out_specs=spec,
        out_shape=jax.ShapeDtypeStruct((T, na_w), BF16),
        input_output_aliases={3: 0},
        compiler_params=_cparams(("parallel", "parallel")),
        name="attn_ctx",
    )(q, k, v, na)


def _conv4(x_prev, x_main, x_next, cw):
    n = x_main.shape[0]
    xe = jnp.concatenate([x_prev, x_main, x_next], axis=0)
    ne = n + 2 * SUBLANES
    y = cw[2:3] * x_main
    y = y + cw[0:1] * pltpu.roll(xe, 2, 0)[SUBLANES:SUBLANES + n]
    y = y + cw[1:2] * pltpu.roll(xe, 1, 0)[SUBLANES:SUBLANES + n]
    y = y + cw[3:4] * pltpu.roll(xe, ne - 1, 0)[SUBLANES:SUBLANES + n]
    return y + cw[4:5]


def _softplus(x):
    return jnp.maximum(x, 0.0) + jnp.log1p(jnp.exp(-jnp.abs(x)))


def _lru_coeffs(xl, w, vec):
    C = xl.shape[1]
    gates = jnp.dot(xl.astype(BF16), w, preferred_element_type=F32)
    r = jax.nn.sigmoid(gates[:, :C] + vec[0:1])
    i = jax.nn.sigmoid(gates[:, C:] + vec[1:2])
    log_a = (-LRU_C * r) * _softplus(-vec[2:3])
    a = jnp.exp(log_a)
    b = jnp.sqrt(-jnp.tanh(log_a) * (1.0 + a * a)) * (i * xl)
    return a, b


def _group_scan(a, b, reverse):
    n, C = a.shape
    a = a.reshape(n // SUBLANES, SUBLANES, C)
    b = b.reshape(n // SUBLANES, SUBLANES, C)
    row = lax.broadcasted_iota(jnp.int32, a.shape, 1)
    for k in (1, 2, 4):
        shift = SUBLANES - k if reverse else k
        mask = (row < SUBLANES - k) if reverse else (row >= k)
        a_s = pltpu.roll(a, shift, 1)
        b_s = pltpu.roll(b, shift, 1)
        b = jnp.where(mask, a * b_s + b, b)
        a = jnp.where(mask, a * a_s, a)
    return a.reshape(n, C), b.reshape(n, C)


def _chunk_scan(a, b, h_in, reverse, acum_ref, bcum_ref, out_ref):
    n, C = a.shape
    ac, bc = _group_scan(a, b, reverse)
    acum_ref[...] = ac
    bcum_ref[...] = bc
    ng = n // SUBLANES

    def body(g, hb):
        idx = (ng - 1 - g) if reverse else g
        sl = pl.ds(pl.multiple_of(idx * SUBLANES, SUBLANES), SUBLANES)
        h = acum_ref[sl, :] * hb + bcum_ref[sl, :]
        out_ref[sl, :] = h
        edge = h[0:1] if reverse else h[SUBLANES - 1:SUBLANES]
        return jnp.broadcast_to(edge, (SUBLANES, C))

    return lax.fori_loop(0, ng, body, h_in, unroll=8)


def _lru_ctx_kernel(xr_ref, gr_ref, cw_ref, w_ref, vec_ref, o_ref, hend_ref, acum_ref, bcum_ref, hf_ref, hb_ref):
    xr = xr_ref[...]
    zeros8 = jnp.zeros((SUBLANES, xr.shape[1]), F32)
    xl = _conv4(zeros8, xr, zeros8, cw_ref[...])
    a, b = _lru_coeffs(xl, w_ref[0], vec_ref[0])
    hf_end = _chunk_scan(a, b, zeros8, False, acum_ref, bcum_ref, hf_ref)
    a, b = _lru_coeffs(xl, w_ref[1], vec_ref[1])
    hb_end = _chunk_scan(a, b, zeros8, True, acum_ref, bcum_ref, hb_ref)
    o_ref[...] = ((hf_ref[...] + hb_ref[...]) * jax.nn.gelu(gr_ref[...])).astype(BF16)
    row = lax.broadcasted_iota(jnp.int32, hf_end.shape, 0)
    hend_ref[...] = jnp.where(row == 0, hf_end, hb_end)


def _lru_ctx(xr, gr, cw, w, vec, *, B, S, CTX):
    C = xr.shape[1]
    ctx0 = (B * S) // CTX
    spec = pl.BlockSpec((CTX, C), lambda b: (ctx0 + b, 0))
    return pl.pallas_call(
        _lru_ctx_kernel,
        grid=(B,),
        in_specs=[spec, spec,
                  pl.BlockSpec((SUBLANES, C), lambda b: (0, 0)),
                  pl.BlockSpec((2, C, 2 * C), lambda b: (0, 0, 0)),
                  pl.BlockSpec((2, SUBLANES, C), lambda b: (0, 0, 0))],
        out_specs=[spec,
                   pl.BlockSpec((None, SUBLANES, C), lambda b: (b, 0, 0))],
        out_shape=[jax.ShapeDtypeStruct((xr.shape[0], C), BF16),
                   jax.ShapeDtypeStruct((B, SUBLANES, C), F32)],
        scratch_shapes=[pltpu.VMEM((CTX, C), F32)] * 4,
        compiler_params=_cparams(("parallel",)),
        name="lru_ctx",
    )(xr, gr, cw, w, vec)


def _lru_lat_kernel(*refs, reverse, nch):
    if reverse:
        xr_ref, xp_ref, xn_ref, hend_ref, cw_ref, w_ref, vec_ref, o_ref, carry_ref, acum_ref, bcum_ref = refs
    else:
        (xr_ref, xp_ref, xn_ref, hend_ref, cw_ref, w_ref, vec_ref, gr_ref, hb_ref, _lru_hbm_ref,
         o_ref, carry_ref, acum_ref, bcum_ref, hf_ref) = refs
    i = pl.program_id(1)
    c = (nch - 1 - i) if reverse else i
    C = xr_ref.shape[1]

    @pl.when(i == 0)
    def _():
        row = hend_ref[1:2, :] if reverse else hend_ref[0:1, :]
        carry_ref[...] = jnp.broadcast_to(row, (SUBLANES, C))

    xp = jnp.where(c == 0, 0.0, xp_ref[...])
    xn = jnp.where(c == nch - 1, 0.0, xn_ref[...])
    xl = _conv4(xp, xr_ref[...], xn, cw_ref[...])
    a, b = _lru_coeffs(xl, w_ref[...], vec_ref[...])
    if reverse:
        carry_ref[...] = _chunk_scan(a, b, carry_ref[...], True, acum_ref, bcum_ref, o_ref)
    else:
        carry_ref[...] = _chunk_scan(a, b, carry_ref[...], False, acum_ref, bcum_ref, hf_ref)
        o_ref[...] = ((hf_ref[...] + hb_ref[...]) * jax.nn.gelu(gr_ref[...])).astype(BF16)


def _lru_lat(xr, gr, hb, lru, hend, cw, w, vec, *, reverse, B, S, tt):
    C = xr.shape[1]
    nch = S // tt
    d = 1 if reverse else 0
    chunk = (lambda i: nch - 1 - i) if reverse else (lambda i: i)
    nblk8 = (B * S) // SUBLANES
    main = pl.BlockSpec((tt, C), lambda b, i: (b * nch + chunk(i), 0))
    in_specs = [
        main,
        pl.BlockSpec((SUBLANES, C), lambda b, i: (jnp.maximum((b * nch + chunk(i)) * (tt // SUBLANES) - 1, 0), 0)),
        pl.BlockSpec((SUBLANES, C), lambda b, i: (jnp.minimum((b * nch + chunk(i) + 1) * (tt // SUBLANES), nblk8 - 1), 0)),
        pl.BlockSpec((None, SUBLANES, C), lambda b, i: (b, 0, 0)),
        pl.BlockSpec((SUBLANES, C), lambda b, i: (0, 0)),
        pl.BlockSpec((None, C, 2 * C), lambda b, i: (d, 0, 0)),
        pl.BlockSpec((None, SUBLANES, C), lambda b, i: (d, 0, 0))]
    args = [xr, xr, xr, hend, cw, w, vec]
    scratch = [pltpu.VMEM((SUBLANES, C), F32), pltpu.VMEM((tt, C), F32), pltpu.VMEM((tt, C), F32)]
    if reverse:
        out_shape, aliases = jax.ShapeDtypeStruct((B * S, C), F32), {}
    else:
        in_specs += [main, main, pl.BlockSpec(memory_space=pl.ANY)]
        args += [gr, hb, lru]
        scratch.append(pltpu.VMEM((tt, C), F32))
        out_shape, aliases = jax.ShapeDtypeStruct(lru.shape, BF16), {len(args) - 1: 0}
    return pl.pallas_call(
        functools.partial(_lru_lat_kernel, reverse=reverse, nch=nch),
        grid=(B, nch),
        in_specs=in_specs,
        out_specs=main,
        out_shape=out_shape,
        input_output_aliases=aliases,
        scratch_shapes=scratch,
        compiler_params=_cparams(("parallel", "arbitrary")),
        name="lru_bwd" if reverse else "lru_fwd",
    )(*args)


def _fno_w_kernel(fw_ref, cs_ref, o_ref):
    fw = fw_ref[...]
    o_ref[0] = jnp.dot(cs_ref[0], fw, preferred_element_type=F32, precision=HIGHEST).astype(BF16)
    o_ref[1] = jnp.dot(cs_ref[1], fw, preferred_element_type=F32, precision=HIGHEST).astype(BF16)


def _fno_weights(fw_bd, cs_bd):
    L, C, _ = fw_bd.shape
    return pl.pallas_call(
        _fno_w_kernel,
        grid=(L,),
        in_specs=[pl.BlockSpec((None, C, C), lambda l: (l, 0, 0)),
                  pl.BlockSpec((2, C, C), lambda l: (0, 0, 0))],
        out_specs=pl.BlockSpec((None, 2, C, C), lambda l: (l, 0, 0, 0)),
        out_shape=jax.ShapeDtypeStruct((L, 2, C, C), BF16),
        compiler_params=_cparams(("parallel",)),
        name="fno_w",
    )(fw_bd, cs_bd)


def _fno_a_kernel(x_ref, w_ref, m_ref, o_ref, *, nb):
    n1 = x_ref.shape[0] // nb
    _, s, C = x_ref.shape
    uvs = []
    for bb in range(nb):
        x = x_ref[bb * n1:(bb + 1) * n1].reshape(n1 * s, C).astype(BF16)
        u = jnp.dot(x, w_ref[0], preferred_element_type=F32).astype(BF16)
        v = jnp.dot(x, w_ref[1], preferred_element_type=F32).astype(BF16)
        uvs.append(jnp.concatenate([u, v], axis=0))
    uv = jnp.concatenate(uvs, axis=1)
    a = jnp.dot(m_ref[...], uv, preferred_element_type=F32).astype(BF16)
    rows = 2 * s
    for bb in range(nb):
        for k1 in range(n1):
            o_ref[bb, :, k1 * C:(k1 + 1) * C] = a[k1 * rows:(k1 + 1) * rows, bb * C:(bb + 1) * C]


def _fno_c_kernel(a_ref, f_ref, o_ref, *, scale):
    o_ref[...] = (jnp.dot(f_ref[...], a_ref[...], preferred_element_type=F32) * scale).astype(BF16)


def _fno_lat(f, w_l, m_a, f_c, *, B, S):
    T, C = f.shape
    n1 = FNO_N1
    n2 = S // n1
    nj = n2 // SUBLANES
    nb = 4 if B % 4 == 0 else 1
    f4 = f.reshape(T // n2, nj, SUBLANES, C)
    rows_a = 2 * SUBLANES * n1
    a2 = pl.pallas_call(
        functools.partial(_fno_a_kernel, nb=nb),
        grid=(nj, B // nb),
        in_specs=[pl.BlockSpec((nb * n1, None, SUBLANES, C), lambda j, b: (b, j, 0, 0)),
                  pl.BlockSpec((2, C, C), lambda j, b: (0, 0, 0)),
                  pl.BlockSpec((None, rows_a, rows_a), lambda j, b: (j, 0, 0))],
        out_specs=pl.BlockSpec((nb, 2 * SUBLANES, n1 * C), lambda j, b: (b, j, 0)),
        out_shape=jax.ShapeDtypeStruct((B, 2 * n2, n1 * C), BF16),
        compiler_params=_cparams(("parallel", "parallel")),
        name="fno_a",
    )(f4, w_l, m_a)
    ncol = n1 * C
    tc = min(ncol, 2048)
    scale = 1.0 / math.sqrt(S * (C // FNO_GROUPS))
    out = pl.pallas_call(
        functools.partial(_fno_c_kernel, scale=scale),
        grid=(B, ncol // tc),
        in_specs=[pl.BlockSpec((None, 2 * n2, tc), lambda b, t: (b, 0, t)),
                  pl.BlockSpec((n2, 2 * n2), lambda b, t: (0, 0))],
        out_specs=pl.BlockSpec((None, n2, tc), lambda b, t: (b, 0, t)),
        out_shape=jax.ShapeDtypeStruct((B, n2, ncol), BF16),
        compiler_params=_cparams(("parallel", "parallel")),
        name="fno_c",
    )(a2, f_c)
    return out.reshape(B * S, C)


def _fno_ctx_kernel(x_ref, w_ref, f_ref, o_ref, *, scale):
    x = x_ref[...].astype(BF16)
    u = jnp.dot(x, w_ref[0], preferred_element_type=F32).astype(BF16)
    v = jnp.dot(x, w_ref[1], preferred_element_type=F32).astype(BF16)
    uv = jnp.concatenate([u, v], axis=0)
    o_ref[...] = (jnp.dot(f_ref[...], uv, preferred_element_type=F32) * scale).astype(BF16)


def _fno_ctx(f, w_l, f_ctx, *, B, S, CTX):
    C = f.shape[1]
    ctx0 = (B * S) // CTX
    scale = 1.0 / math.sqrt(CTX * (C // FNO_GROUPS))
    return pl.pallas_call(
        functools.partial(_fno_ctx_kernel, scale=scale),
        grid=(B,),
        in_specs=[pl.BlockSpec((CTX, C), lambda b: (ctx0 + b, 0)),
                  pl.BlockSpec((2, C, C), lambda b: (0, 0, 0)),
                  pl.BlockSpec((CTX, 2 * CTX), lambda b: (0, 0))],
        out_specs=pl.BlockSpec((CTX, C), lambda b: (b, 0)),
        out_shape=jax.ShapeDtypeStruct((B * CTX, C), BF16),
        compiler_params=_cparams(("parallel",)),
        name="fno_ctx",
    )(f, w_l, f_ctx)


def _dft_constants(S, CTX, C):
    gd = C // FNO_GROUPS
    n1, n2 = FNO_N1, S // FNO_N1
    nj = n2 // SUBLANES
    idx = jnp.arange(gd, dtype=jnp.int32)
    ang = (2.0 * math.pi / gd) * ((idx[:, None] * idx[None, :]) % gd).astype(F32)
    eye_g = jnp.eye(FNO_GROUPS, dtype=F32)
    cs_bd = jnp.stack([jnp.kron(eye_g, jnp.cos(ang)), jnp.kron(eye_g, jnp.sin(ang))])
    k1 = jnp.arange(n1, dtype=jnp.int32)[:, None, None]
    nn1 = jnp.arange(n1, dtype=jnp.int32)[None, :, None]
    nn2 = jnp.arange(n2, dtype=jnp.int32)[None, None, :]
    ph = (2.0 * math.pi / S) * ((k1 * (n2 * nn1 + nn2)) % S).astype(F32)
    cph = jnp.cos(ph).reshape(n1, n1, nj, SUBLANES).transpose(2, 0, 3, 1)
    sph = jnp.sin(ph).reshape(n1, n1, nj, SUBLANES).transpose(2, 0, 3, 1)
    t4 = jnp.stack([jnp.stack([cph, -sph], axis=3), jnp.stack([-sph, -cph], axis=3)], axis=2)
    eye_s = jnp.eye(SUBLANES, dtype=bool)[None, None, None, :, None, None, :]
    m_a = jnp.where(eye_s, t4[..., None], 0.0).astype(BF16)
    m_a = m_a.reshape(nj, 2 * SUBLANES * n1, 2 * n1 * SUBLANES)
    i2 = jnp.arange(n2, dtype=jnp.int32)
    a2 = (2.0 * math.pi / n2) * ((i2[:, None] * i2[None, :]) % n2).astype(F32)
    f_c = jnp.stack([jnp.cos(a2).reshape(n2, nj, SUBLANES), jnp.sin(a2).reshape(n2, nj, SUBLANES)], axis=2)
    f_c = f_c.reshape(n2, 2 * n2).astype(BF16)
    ic = jnp.arange(CTX, dtype=jnp.int32)
    ac = (2.0 * math.pi / CTX) * ((ic[:, None] * ic[None, :]) % CTX).astype(F32)
    f_ctx = jnp.concatenate([jnp.cos(ac), -jnp.sin(ac)], axis=1).astype(BF16)
    return cs_bd, m_a, f_c, f_ctx


def _block_diag(w):
    G, n = w.shape[-3], w.shape[-1]
    eye = jnp.eye(G, dtype=w.dtype)
    out = jnp.einsum('...gij,gh->...gihj', w, eye)
    return out.reshape(w.shape[:-3] + (G * n, G * n))


def kernel(x, c, ctx, c_ctx, w_ada, b_ada, ln_g, ln_b, ff1_gate, ff1_up, ff1_down, ff2_gate, ff2_up, ff2_down,
           w_in, w_out, na_rpb, lru_conv_w, lru_conv_b, lru_wa, lru_ba, lru_wx, lru_bx, lru_lambda, fno_w):
    B, S, D = x.shape
    CTX = ctx.shape[1]
    L = w_ada.shape[0]
    na_w = na_rpb.shape[1] * HEAD_DIM
    lru_w = lru_conv_w.shape[2]
    alpha = float((2 * L) ** 0.25)
    tm = min(1024, B * CTX)
    tiles_per_batch = S // tm
    nlat = (B * S) // tm
    ntiles = nlat + (B * CTX) // tm
    assert B < 16 and S % tm == 0 and (B * CTX) % tm == 0 and S % (GRID_W * NA_KH) == 0

    cc = jnp.zeros((16, D), F32).at[:B].set(c).at[B].set(c_ctx)
    mods = _ada_params(cc, w_ada, b_ada)

    bf = lambda w: w.astype(BF16)
    ff1 = (bf(ff1_gate), bf(ff1_up), bf(ff1_down))
    ff2 = (bf(ff2_gate), bf(ff2_up), bf(ff2_down))
    w_in_b, w_out_b = bf(w_in), bf(w_out)
    ln_g4 = ln_g[:, :, None, :]
    ln_b4 = ln_b[:, :, None, :]
    wb = jax.vmap(_na_bias_tables)(na_rpb)
    lru_gate_w = bf(jnp.concatenate([_block_diag(lru_wa), _block_diag(lru_wx)], axis=-1))
    zrow = jnp.zeros((L, 2, SUBLANES - 3, lru_w), F32)
    lru_vec = jnp.concatenate([lru_ba[:, :, None], lru_bx[:, :, None], lru_lambda[:, :, None], zrow], axis=2)
    lru_cw = jnp.concatenate([lru_conv_w, lru_conv_b[:, None], jnp.zeros((L, SUBLANES - LRU_CONV - 1, lru_w), F32)], axis=1)
    cs_bd, m_a, f_c, f_ctx = _dft_constants(S, CTX, D - na_w - lru_w)
    fno_wcs = _fno_weights(_block_diag(fno_w), cs_bd)

    common = dict(tm=tm, tiles_per_batch=tiles_per_batch, nbatch=B)
    xs = (x.reshape(B * S, D), ctx.reshape(B * CTX, D))
    for l in range(L):
        last = l == L - 1
        x1 = _ffn(xs, mods, *ff1, ln_g4[:, 0], ln_b4[:, 0], j=0, l=l, alpha=alpha, nlat=nlat, ntiles=ntiles, **common)
        q, k, v, xr, gr, f = _inproj(x1, mods, w_in_b, l=l, na_w=na_w, lru_w=lru_w, **common)
        na = _attn_lat(q, k, v, wb[l], B=B, S=S, CTX=CTX)
        lru, hend = _lru_ctx(xr, gr, lru_cw[l], lru_gate_w[l], lru_vec[l], B=B, S=S, CTX=CTX)
        tt = min(S, 1024)
        lru_p = (lru_cw[l], lru_gate_w[l], lru_vec[l])
        hb = _lru_lat(xr, None, None, None, hend, *lru_p, reverse=True, B=B, S=S, tt=tt)
        lru = _lru_lat(xr, gr, hb, lru, hend, *lru_p, reverse=False, B=B, S=S, tt=tt)
        fno_lat = _fno_lat(f, fno_wcs[l], m_a, f_c, B=B, S=S)
        if last:
            fno_ctx, nt = None, nlat
        else:
            na = _attn_ctx(q, k, v, na, B=B, S=S, CTX=CTX)
            fno_ctx, nt = _fno_ctx(f, fno_wcs[l], f_ctx, B=B, S=S, CTX=CTX), ntiles
        x3 = _ffn((x1,), mods, *ff2, ln_g4[:, 2], ln_b4[:, 2], j=2, l=l, alpha=alpha, nlat=nlat, ntiles=nt,
                  mix=(na, lru, fno_lat, fno_ctx, w_out_b, ln_g4[:, 1], ln_b4[:, 1]), **common)
        xs = (x3,)
    return xs[0].reshape(B, S, D)
```

```python
import functools
import math

import numpy as np
import jax
import jax.numpy as jnp
from jax import lax
from jax.experimental import pallas as pl
from jax.experimental.pallas import tpu as pltpu

F32 = jnp.float32
BF16 = jnp.bfloat16
HIGHEST = lax.Precision.HIGHEST

HEAD_DIM = 64
GRID_W = 64
NA_KH = 8
NA_KW = 16
LRU_HEADS = 4
LRU_CONV = 4
LRU_C = 8.0
FNO_GROUPS = 4
N_MOD = 9
MACARON = 0.5
LN_EPS = 1e-5
MASK_BIAS = -1e30
LOG2E = math.log2(math.e)
SUBLANES = 8
FNO_N1 = 64
VMEM_LIMIT = 56 * 1024 * 1024


def _cparams(sem):
    return pltpu.CompilerParams(dimension_semantics=sem, vmem_limit_bytes=VMEM_LIMIT)


def _layer_norm(z, g, b):
    mu = jnp.mean(z, axis=-1, keepdims=True)
    zc = z - mu
    var = jnp.mean(zc * zc, axis=-1, keepdims=True)
    return zc * lax.rsqrt(var + LN_EPS) * g + b


def _ada_kernel(c_ref, w_ref, b_ref, o_ref):
    c = c_ref[...]
    a = c * jax.nn.sigmoid(c)
    o_ref[...] = jnp.dot(a, w_ref[...], preferred_element_type=F32, precision=HIGHEST) + b_ref[...]


def _ada_params(cc, w_ada, b_ada):
    L, D, ND = w_ada.shape
    tn = D
    out = pl.pallas_call(
        _ada_kernel,
        grid=(L, ND // tn),
        in_specs=[pl.BlockSpec((16, D), lambda l, n: (0, 0)),
                  pl.BlockSpec((None, D, tn), lambda l, n: (l, 0, n)),
                  pl.BlockSpec((None, 1, tn), lambda l, n: (l, 0, n))],
        out_specs=pl.BlockSpec((None, 16, tn), lambda l, n: (l, 0, n)),
        out_shape=jax.ShapeDtypeStruct((L, 16, ND), F32),
        compiler_params=_cparams(("parallel", "parallel")),
        name="ada",
    )(cc, w_ada, b_ada.reshape(L, 1, ND))
    return out.reshape(L, 16, N_MOD, D)


def _ffn_kernel(*refs, j, tf, alpha, nlat, mode):
    m = refs[2 if mode == "two_x" else 1][...]
    if mode == "two_x":
        xl_ref, xc_ref, _, wg_ref, wu_ref, wd_ref, g_ref, b_ref, o_ref, h_ref, a_ref = refs
        x = jnp.where(pl.program_id(0) < nlat, xl_ref[...], xc_ref[...])
    elif mode == "plain":
        x_ref, _, wg_ref, wu_ref, wd_ref, g_ref, b_ref, o_ref, h_ref, a_ref = refs
        x = x_ref[...]
    else:
        if mode == "mix_ctx":
            (x_ref, _, na_ref, lru_ref, fl_ref, fc_ref, wo_ref, g1_ref, b1_ref,
             wg_ref, wu_ref, wd_ref, g_ref, b_ref, o_ref, h_ref, a_ref) = refs
            fno = jnp.where(pl.program_id(0) < nlat, fl_ref[...], fc_ref[...])
        else:
            (x_ref, _, na_ref, lru_ref, fl_ref, wo_ref, g1_ref, b1_ref,
             wg_ref, wu_ref, wd_ref, g_ref, b_ref, o_ref, h_ref, a_ref) = refs
            fno = fl_ref[...]
        na_w, lru_w = na_ref.shape[1], lru_ref.shape[1]
        y = jnp.dot(na_ref[...], wo_ref[0:na_w, :], preferred_element_type=F32)
        y = y + jnp.dot(lru_ref[...], wo_ref[na_w:na_w + lru_w, :], preferred_element_type=F32)
        y = y + jnp.dot(fno, wo_ref[na_w + lru_w:, :], preferred_element_type=F32)
        o_ref[...] = _layer_norm(alpha * x_ref[...] + m[5:6] * y, g1_ref[...], b1_ref[...])
        x = o_ref[...]
    shift, scale, gate = m[3 * j:3 * j + 1], m[3 * j + 1:3 * j + 2], m[3 * j + 2:3 * j + 3]
    h_ref[...] = (x * (1.0 + scale) + shift).astype(BF16)
    for c in range(wg_ref.shape[1] // tf):
        cols = slice(c * tf, (c + 1) * tf)
        h = h_ref[...]
        g = jnp.dot(h, wg_ref[:, cols], preferred_element_type=F32)
        u = jnp.dot(h, wu_ref[:, cols], preferred_element_type=F32)
        a_ref[:, cols] = (g * jax.nn.sigmoid(g) * u).astype(BF16)
    y = jnp.dot(a_ref[...], wd_ref[...], preferred_element_type=F32)
    if mode.startswith("mix"):
        x = o_ref[...]
    z = alpha * x + (MACARON * gate) * y
    o_ref[...] = _layer_norm(z, g_ref[...], b_ref[...])


def _ffn(xs, mods_l, wg, wu, wd, ln_g, ln_b, *, j, l, alpha, tm, tiles_per_batch, nlat, ntiles, nbatch, mix=None):
    two_x = len(xs) == 2
    mode = "two_x" if two_x else "plain"
    D = xs[0].shape[1]
    Fd = wg.shape[2]
    tf = 256
    row_idx = lambda i: jnp.minimum(i // tiles_per_batch, nbatch)
    if two_x:
        nctx = ntiles - nlat
        x_specs = [pl.BlockSpec((tm, D), lambda i: (jnp.minimum(i, nlat - 1), 0)),
                   pl.BlockSpec((tm, D), lambda i: (jnp.clip(i - nlat, 0, nctx - 1), 0))]
    else:
        x_specs = [pl.BlockSpec((tm, D), lambda i: (i, 0))]
    const = pl.Buffered(1)
    mix_specs, mix_args = [], []
    if mix is not None:
        na, lru, fno_l, fno_c, w_out, g1, b1 = mix
        mode = "mix" if fno_c is None else "mix_ctx"
        tile = lambda a: pl.BlockSpec((tm, a.shape[1]), lambda i: (i, 0))
        mix_specs = [tile(na), tile(lru), pl.BlockSpec((tm, fno_l.shape[1]), lambda i: (jnp.minimum(i, nlat - 1), 0))]
        mix_args = [na, lru, fno_l]
        if fno_c is not None:
            mix_specs.append(pl.BlockSpec((tm, fno_c.shape[1]), lambda i: (jnp.clip(i - nlat, 0, ntiles - nlat - 1), 0)))
            mix_args.append(fno_c)
        mix_specs += [pl.BlockSpec((None, D, D), lambda i: (l, 0, 0), pipeline_mode=const),
                      pl.BlockSpec((None, 1, D), lambda i: (l, 0, 0)),
                      pl.BlockSpec((None, 1, D), lambda i: (l, 0, 0))]
        mix_args += [w_out, g1, b1]
    return pl.pallas_call(
        functools.partial(_ffn_kernel, j=j, tf=tf, alpha=alpha, nlat=nlat, mode=mode),
        grid=(ntiles,),
        in_specs=x_specs + [
            pl.BlockSpec((None, None, N_MOD, D), lambda i: (l, row_idx(i), 0, 0))] + mix_specs + [
            pl.BlockSpec((None, D, Fd), lambda i: (l, 0, 0), pipeline_mode=const),
            pl.BlockSpec((None, D, Fd), lambda i: (l, 0, 0), pipeline_mode=const),
            pl.BlockSpec((None, Fd, D), lambda i: (l, 0, 0), pipeline_mode=const),
            pl.BlockSpec((None, 1, D), lambda i: (l, 0, 0)),
            pl.BlockSpec((None, 1, D), lambda i: (l, 0, 0))],
        out_specs=pl.BlockSpec((tm, D), lambda i: (i, 0)),
        out_shape=jax.ShapeDtypeStruct((ntiles * tm, D), F32),
        scratch_shapes=[pltpu.VMEM((tm, D), BF16), pltpu.VMEM((tm, Fd), BF16)],
        compiler_params=_cparams(("parallel",)),
        name=f"ffn{j}",
    )(*xs, mods_l, *mix_args, wg, wu, wd, ln_g, ln_b)


def _inproj_kernel(x_ref, m_ref, w_ref, q_ref, k_ref, v_ref, xr_ref, gr_ref, f_ref, *, na_w, lru_w):
    m = m_ref[...]
    h = (x_ref[...] * (1.0 + m[4:5]) + m[3:4]).astype(BF16)

    def proj(lo, width):
        return jnp.dot(h, w_ref[:, lo:lo + width], preferred_element_type=F32)

    q_ref[...] = (proj(0, na_w) * (HEAD_DIM ** -0.5 * LOG2E)).astype(BF16)
    k_ref[...] = proj(na_w, na_w).astype(BF16)
    v_ref[...] = proj(2 * na_w, na_w).astype(BF16)
    xr_ref[...] = proj(3 * na_w, lru_w)
    gr_ref[...] = proj(3 * na_w + lru_w, lru_w)
    f_ref[...] = proj(3 * na_w + 2 * lru_w, w_ref.shape[1] - 3 * na_w - 2 * lru_w)


def _inproj(x_all, mods_l, w_in, *, l, tm, tiles_per_batch, nbatch, na_w, lru_w):
    T, D = x_all.shape
    cols = w_in.shape[2]
    fno_w = cols - 3 * na_w - 2 * lru_w
    row_idx = lambda i: jnp.minimum(i // tiles_per_batch, nbatch)
    widths = [(na_w, BF16)] * 3 + [(lru_w, F32)] * 2 + [(fno_w, F32)]
    return pl.pallas_call(
        functools.partial(_inproj_kernel, na_w=na_w, lru_w=lru_w),
        grid=(T // tm,),
        in_specs=[pl.BlockSpec((tm, D), lambda i: (i, 0)),
                  pl.BlockSpec((None, None, N_MOD, D), lambda i: (l, row_idx(i), 0, 0)),
                  pl.BlockSpec((None, D, cols), lambda i: (l, 0, 0), pipeline_mode=pl.Buffered(1))],
        out_specs=[pl.BlockSpec((tm, w), lambda i: (i, 0)) for w, _ in widths],
        out_shape=[jax.ShapeDtypeStruct((T, w), dt) for w, dt in widths],
        compiler_params=_cparams(("parallel",)),
        name="inproj",
    )(x_all, mods_l, w_in)


def _na_bias_tables(rpb):
    L, H = rpb.shape[:2]
    W, ndr, ndc = GRID_W, 2 * NA_KH - 1, 2 * NA_KW - 1
    c = np.arange(W)[:, None, None]
    dr = np.arange(ndr)[None, :, None]
    j = np.arange(W)[None, None, :]
    col0 = np.clip(c - NA_KW // 2, 0, W - NA_KW)
    valid = (j >= col0) & (j < col0 + NA_KW)
    idx = np.where(valid, dr * ndc + np.clip(j - c + (NA_KW - 1), 0, ndc - 1), ndr * ndc).reshape(W, ndr * W)
    ext = jnp.concatenate([rpb.reshape(L, H, ndr * ndc) * LOG2E, jnp.full((L, H, 1), MASK_BIAS, F32)], axis=-1)
    bt = jnp.take(ext, jnp.asarray(idx, jnp.int32), axis=2)
    wb = jnp.stack([bt[..., (NA_KH - 1 - case) * W:(2 * NA_KH - 1 - case) * W] for case in range(NA_KH)], axis=2)
    return wb.reshape(L, H // 2, 2, NA_KH, W, NA_KH * W)


def _masked_heads(q):
    lane = lax.broadcasted_iota(jnp.int32, q.shape, 1)
    qf = q.astype(F32)
    return (jnp.where(lane < HEAD_DIM, qf, 0.0).astype(BF16),
            jnp.where(lane >= HEAD_DIM, qf, 0.0).astype(BF16))


def _qk(q, k):
    return lax.dot_general(q, k, (((1,), (1,)), ((), ())), preferred_element_type=F32)


def _attn_kernel(q_ref, k_ref, v_ref, kc_ref, vc_ref, wb_ref, o_ref, *, R, G, rows):
    jb = pl.program_id(2)
    kc = kc_ref[...]
    vc = vc_ref[...]
    W = GRID_W
    nwin = NA_KH * W
    lane = lax.broadcasted_iota(jnp.int32, (W, 2 * HEAD_DIM), 1)

    def group(g, carry):
        qoff = pl.multiple_of(g * (R * W), R * W)
        qcat = []
        for rr in range(R):
            qe, qo = _masked_heads(q_ref[pl.ds(qoff + rr * W, W), :])
            qcat += [qe, qo]
        s_ctx_all = _qk(jnp.concatenate(qcat, axis=0), kc)
        o_loc, p_ctx, dens = [], [], []
        for rr in range(R):
            r = (jb * G + g) * R + rr
            rs = jnp.clip(r - NA_KH // 2, 0, rows - NA_KH)
            koff = pl.multiple_of(rs * W, W)
            kw = k_ref[pl.ds(koff, nwin), :]
            vw = v_ref[pl.ds(koff, nwin), :]
            qc = jnp.concatenate(qcat[2 * rr:2 * rr + 2], axis=0)
            s_loc = _qk(qc, kw) + jnp.concatenate([wb_ref[0, r - rs], wb_ref[1, r - rs]], axis=0)
            s_ctx = s_ctx_all[rr * 2 * W:(rr + 1) * 2 * W]
            m = jnp.maximum(jnp.max(s_loc, axis=-1, keepdims=True), jnp.max(s_ctx, axis=-1, keepdims=True))
            p_loc = jnp.exp2(s_loc - m)
            pc = jnp.exp2(s_ctx - m)
            dens.append(jnp.sum(p_loc, axis=-1, keepdims=True) + jnp.sum(pc, axis=-1, keepdims=True))
            o_loc.append(jnp.dot(p_loc.astype(BF16), vw, preferred_element_type=F32))
            p_ctx.append(pc.astype(BF16))
        o_ctx_all = jnp.dot(jnp.concatenate(p_ctx, axis=0), vc, preferred_element_type=F32)
        outs = []
        for rr in range(R):
            o = (o_loc[rr] + o_ctx_all[rr * 2 * W:(rr + 1) * 2 * W]) / dens[rr]
            outs.append(jnp.where(lane < HEAD_DIM, o[:W], o[W:]).astype(BF16))
        o_ref[pl.ds(qoff, R * W), :] = jnp.concatenate(outs, axis=0)
        return carry

    lax.fori_loop(0, G, group, 0, unroll=True)


def _attn_lat(q, k, v, wb, *, l, B, S, CTX):
    T, na_w = q.shape
    npair = na_w // (2 * HEAD_DIM)
    rows = S // GRID_W
    R = 8
    G = 4 if rows % (4 * R) == 0 else 1
    nrb = rows // (R * G)
    pw = 2 * HEAD_DIM
    ctx0 = (B * S) // CTX
    qblk = R * G * GRID_W
    return pl.pallas_call(
        functools.partial(_attn_kernel, R=R, G=G, rows=rows),
        grid=(B, npair, nrb),
        in_specs=[pl.BlockSpec((qblk, pw), lambda b, p, j: (b * nrb + j, p)),
                  pl.BlockSpec((S, pw), lambda b, p, j: (b, p)),
                  pl.BlockSpec((S, pw), lambda b, p, j: (b, p)),
                  pl.BlockSpec((CTX, pw), lambda b, p, j: (ctx0 + b, p)),
                  pl.BlockSpec((CTX, pw), lambda b, p, j: (ctx0 + b, p)),
                  pl.BlockSpec((None, None, 2, NA_KH, GRID_W, NA_KH * GRID_W), lambda b, p, j: (l, p, 0, 0, 0, 0))],
        out_specs=pl.BlockSpec((qblk, pw), lambda b, p, j: (b * nrb + j, p)),
        out_shape=jax.ShapeDtypeStruct((T, na_w), BF16),
        compiler_params=_cparams(("parallel", "parallel", "parallel")),
        name="attn_lat",
    )(q, k, v, k, v, wb)


def _attn_ctx_kernel(q_ref, k_ref, v_ref, na_hbm_ref, o_ref):
    del na_hbm_ref
    k = k_ref[...]
    v = v_ref[...]
    lane = lax.broadcasted_iota(jnp.int32, o_ref.shape, 1)
    outs = []
    for qm in _masked_heads(q_ref[...]):
        s = _qk(qm, k)
        p = jnp.exp2(s - jnp.max(s, axis=-1, keepdims=True))
        den = jnp.sum(p, axis=-1, keepdims=True)
        outs.append(jnp.dot(p.astype(BF16), v, preferred_element_type=F32) / den)
    o_ref[...] = jnp.where(lane < HEAD_DIM, outs[0], outs[1]).astype(BF16)


def _attn_ctx(q, k, v, na, *, B, S, CTX):
    T, na_w = q.shape
    pw = 2 * HEAD_DIM
    ctx0 = (B * S) // CTX
    spec = pl.BlockSpec((CTX, pw), lambda b, p: (ctx0 + b, p))
    return pl.pallas_call(
        _attn_ctx_kernel,
        grid=(B, na_w // pw),
        in_specs=[spec, spec, spec, pl.BlockSpec(memory_space=pl.ANY)],
        out_specs=spec,
        out_shape=jax.ShapeDtypeStruct((T, na_w), BF16),
        input_output_aliases={3: 0},
        compiler_params=_cparams(("parallel", "parallel")),
        name="attn_ctx",
    )(q, k, v, na)


def _conv4(x_prev, x_main, x_next, cw):
    n = x_main.shape[0]
    xe = jnp.concatenate([x_prev, x_main, x_next], axis=0)
    ne = n + 2 * SUBLANES
    y = cw[2:3] * x_main
    y = y + cw[0:1] * pltpu.roll(xe, 2, 0)[SUBLANES:SUBLANES + n]
    y = y + cw[1:2] * pltpu.roll(xe, 1, 0)[SUBLANES:SUBLANES + n]
    y = y + cw[3:4] * pltpu.roll(xe, ne - 1, 0)[SUBLANES:SUBLANES + n]
    return y + cw[4:5]


def _softplus(x):
    return jnp.maximum(x, 0.0) + jnp.log1p(jnp.exp(-jnp.abs(x)))


def _lru_coeffs(xl, w, vec):
    C = xl.shape[1]
    gates = jnp.dot(xl.astype(BF16), w, preferred_element_type=F32)
    r = jax.nn.sigmoid(gates[:, :C] + vec[0:1])
    i = jax.nn.sigmoid(gates[:, C:] + vec[1:2])
    log_a = (-LRU_C * r) * _softplus(-vec[2:3])
    a = jnp.exp(log_a)
    b = jnp.sqrt(-jnp.tanh(log_a) * (1.0 + a * a)) * (i * xl)
    return a, b


def _group_scan(a, b, reverse):
    n, C = a.shape
    a = a.reshape(n // SUBLANES, SUBLANES, C)
    b = b.reshape(n // SUBLANES, SUBLANES, C)
    row = lax.broadcasted_iota(jnp.int32, a.shape, 1)
    for k in (1, 2, 4):
        shift = SUBLANES - k if reverse else k
        mask = (row < SUBLANES - k) if reverse else (row >= k)
        a_s = pltpu.roll(a, shift, 1)
        b_s = pltpu.roll(b, shift, 1)
        b = jnp.where(mask, a * b_s + b, b)
        a = jnp.where(mask, a * a_s, a)
    return a.reshape(n, C), b.reshape(n, C)


def _chunk_scan(a, b, h_in, reverse, acum_ref, bcum_ref, out_ref):
    n, C = a.shape
    ac, bc = _group_scan(a, b, reverse)
    acum_ref[...] = ac
    bcum_ref[...] = bc
    ng = n // SUBLANES

    def body(g, hb):
        idx = (ng - 1 - g) if reverse else g
        sl = pl.ds(pl.multiple_of(idx * SUBLANES, SUBLANES), SUBLANES)
        h = acum_ref[sl, :] * hb + bcum_ref[sl, :]
        out_ref[sl, :] = h
        edge = h[0:1] if reverse else h[SUBLANES - 1:SUBLANES]
        return jnp.broadcast_to(edge, (SUBLANES, C))

    return lax.fori_loop(0, ng, body, h_in, unroll=8)


def _lru_ctx_kernel(xr_ref, gr_ref, cw_ref, w_ref, vec_ref, o_ref, hend_ref, acum_ref, bcum_ref, hf_ref, hb_ref):
    xr = xr_ref[...]
    zeros8 = jnp.zeros((SUBLANES, xr.shape[1]), F32)
    xl = _conv4(zeros8, xr, zeros8, cw_ref[...])
    a, b = _lru_coeffs(xl, w_ref[0], vec_ref[0])
    hf_end = _chunk_scan(a, b, zeros8, False, acum_ref, bcum_ref, hf_ref)
    a, b = _lru_coeffs(xl, w_ref[1], vec_ref[1])
    hb_end = _chunk_scan(a, b, zeros8, True, acum_ref, bcum_ref, hb_ref)
    o_ref[...] = ((hf_ref[...] + hb_ref[...]) * jax.nn.gelu(gr_ref[...])).astype(BF16)
    row = lax.broadcasted_iota(jnp.int32, hf_end.shape, 0)
    hend_ref[...] = jnp.where(row == 0, hf_end, hb_end)


def _lru_ctx(xr, gr, cw, w, vec, *, B, S, CTX):
    C = xr.shape[1]
    ctx0 = (B * S) // CTX
    spec = pl.BlockSpec((CTX, C), lambda b: (ctx0 + b, 0))
    return pl.pallas_call(
        _lru_ctx_kernel,
        grid=(B,),
        in_specs=[spec, spec,
                  pl.BlockSpec((SUBLANES, C), lambda b: (0, 0)),
                  pl.BlockSpec((2, C, 2 * C), lambda b: (0, 0, 0)),
                  pl.BlockSpec((2, SUBLANES, C), lambda b: (0, 0, 0))],
        out_specs=[spec,
                   pl.BlockSpec((None, SUBLANES, C), lambda b: (b, 0, 0))],
        out_shape=[jax.ShapeDtypeStruct((xr.shape[0], C), BF16),
                   jax.ShapeDtypeStruct((B, SUBLANES, C), F32)],
        scratch_shapes=[pltpu.VMEM((CTX, C), F32)] * 4,
        compiler_params=_cparams(("parallel",)),
        name="lru_ctx",
    )(xr, gr, cw, w, vec)


def _lru_lat_kernel(*refs, reverse, nch):
    if reverse:
        xr_ref, xp_ref, xn_ref, hend_ref, cw_ref, w_ref, vec_ref, o_ref, carry_ref, acum_ref, bcum_ref = refs
    else:
        (xr_ref, xp_ref, xn_ref, hend_ref, cw_ref, w_ref, vec_ref, gr_ref, hb_ref, _lru_hbm_ref,
         o_ref, carry_ref, acum_ref, bcum_ref, hf_ref) = refs
    i = pl.program_id(1)
    c = (nch - 1 - i) if reverse else i
    C = xr_ref.shape[1]

    @pl.when(i == 0)
    def _():
        row = hend_ref[1:2, :] if reverse else hend_ref[0:1, :]
        carry_ref[...] = jnp.broadcast_to(row, (SUBLANES, C))

    xp = jnp.where(c == 0, 0.0, xp_ref[...])
    xn = jnp.where(c == nch - 1, 0.0, xn_ref[...])
    xl = _conv4(xp, xr_ref[...], xn, cw_ref[...])
    a, b = _lru_coeffs(xl, w_ref[...], vec_ref[...])
    if reverse:
        carry_ref[...] = _chunk_scan(a, b, carry_ref[...], True, acum_ref, bcum_ref, o_ref)
    else:
        carry_ref[...] = _chunk_scan(a, b, carry_ref[...], False, acum_ref, bcum_ref, hf_ref)
        o_ref[...] = ((hf_ref[...] + hb_ref[...]) * jax.nn.gelu(gr_ref[...])).astype(BF16)


def _lru_lat(xr, gr, hb, lru, hend, cw, w, vec, *, reverse, B, S, tt):
    C = xr.shape[1]
    nch = S // tt
    d = 1 if reverse else 0
    chunk = (lambda i: nch - 1 - i) if reverse else (lambda i: i)
    nblk8 = (B * S) // SUBLANES
    main = pl.BlockSpec((tt, C), lambda b, i: (b * nch + chunk(i), 0))
    in_specs = [
        main,
        pl.BlockSpec((SUBLANES, C), lambda b, i: (jnp.maximum((b * nch + chunk(i)) * (tt // SUBLANES) - 1, 0), 0)),
        pl.BlockSpec((SUBLANES, C), lambda b, i: (jnp.minimum((b * nch + chunk(i) + 1) * (tt // SUBLANES), nblk8 - 1), 0)),
        pl.BlockSpec((None, SUBLANES, C), lambda b, i: (b, 0, 0)),
        pl.BlockSpec((SUBLANES, C), lambda b, i: (0, 0)),
        pl.BlockSpec((None, C, 2 * C), lambda b, i: (d, 0, 0)),
        pl.BlockSpec((None, SUBLANES, C), lambda b, i: (d, 0, 0))]
    args = [xr, xr, xr, hend, cw, w, vec]
    scratch = [pltpu.VMEM((SUBLANES, C), F32), pltpu.VMEM((tt, C), F32), pltpu.VMEM((tt, C), F32)]
    if reverse:
        out_shape, aliases = jax.ShapeDtypeStruct((B * S, C), F32), {}
    else:
        in_specs += [main, main, pl.BlockSpec(memory_space=pl.ANY)]
        args += [gr, hb, lru]
        scratch.append(pltpu.VMEM((tt, C), F32))
        out_shape, aliases = jax.ShapeDtypeStruct(lru.shape, BF16), {len(args) - 1: 0}
    return pl.pallas_call(
        functools.partial(_lru_lat_kernel, reverse=reverse, nch=nch),
        grid=(B, nch),
        in_specs=in_specs,
        out_specs=main,
        out_shape=out_shape,
        input_output_aliases=aliases,
        scratch_shapes=scratch,
        compiler_params=_cparams(("parallel", "arbitrary")),
        name="lru_bwd" if reverse else "lru_fwd",
    )(*args)


def _fno_w_kernel(fw_ref, cs_ref, o_ref):
    fw = fw_ref[...]
    o_ref[0] = jnp.dot(cs_ref[0], fw, preferred_element_type=F32, precision=HIGHEST).astype(BF16)
    o_ref[1] = jnp.dot(cs_ref[1], fw, preferred_element_type=F32, precision=HIGHEST).astype(BF16)


def _fno_weights(fw_bd, cs_bd):
    L, C, _ = fw_bd.shape
    return pl.pallas_call(
        _fno_w_kernel,
        grid=(L,),
        in_specs=[pl.BlockSpec((None, C, C), lambda l: (l, 0, 0)),
                  pl.BlockSpec((2, C, C), lambda l: (0, 0, 0))],
        out_specs=pl.BlockSpec((None, 2, C, C), lambda l: (l, 0, 0, 0)),
        out_shape=jax.ShapeDtypeStruct((L, 2, C, C), BF16),
        compiler_params=_cparams(("parallel",)),
        name="fno_w",
    )(fw_bd, cs_bd)


def _fno_a_kernel(x_ref, w_ref, m_ref, o_ref, *, nb):
    n1 = x_ref.shape[0] // nb
    _, s, C = x_ref.shape
    uvs = []
    for bb in range(nb):
        x = x_ref[bb * n1:(bb + 1) * n1].reshape(n1 * s, C).astype(BF16)
        u = jnp.dot(x, w_ref[0], preferred_element_type=F32).astype(BF16)
        v = jnp.dot(x, w_ref[1], preferred_element_type=F32).astype(BF16)
        uvs.append(jnp.concatenate([u, v], axis=0))
    uv = jnp.concatenate(uvs, axis=1)
    a = jnp.dot(m_ref[...], uv, preferred_element_type=F32).astype(BF16)
    rows = 2 * s
    for bb in range(nb):
        for k1 in range(n1):
            o_ref[bb, :, k1 * C:(k1 + 1) * C] = a[k1 * rows:(k1 + 1) * rows, bb * C:(bb + 1) * C]


def _fno_c_kernel(a_ref, f_ref, o_ref, *, scale):
    o_ref[...] = (jnp.dot(f_ref[...], a_ref[...], preferred_element_type=F32) * scale).astype(BF16)


def _fno_lat(f, w_l, m_a, f_c, *, B, S):
    T, C = f.shape
    n1 = FNO_N1
    n2 = S // n1
    nj = n2 // SUBLANES
    nb = 4 if B % 4 == 0 else 1
    f4 = f.reshape(T // n2, nj, SUBLANES, C)
    rows_a = 2 * SUBLANES * n1
    a2 = pl.pallas_call(
        functools.partial(_fno_a_kernel, nb=nb),
        grid=(nj, B // nb),
        in_specs=[pl.BlockSpec((nb * n1, None, SUBLANES, C), lambda j, b: (b, j, 0, 0)),
                  pl.BlockSpec((2, C, C), lambda j, b: (0, 0, 0)),
                  pl.BlockSpec((None, rows_a, rows_a), lambda j, b: (j, 0, 0))],
        out_specs=pl.BlockSpec((nb, 2 * SUBLANES, n1 * C), lambda j, b: (b, j, 0)),
        out_shape=jax.ShapeDtypeStruct((B, 2 * n2, n1 * C), BF16),
        compiler_params=_cparams(("parallel", "parallel")),
        name="fno_a",
    )(f4, w_l, m_a)
    ncol = n1 * C
    tc = min(ncol, 2048)
    scale = 1.0 / math.sqrt(S * (C // FNO_GROUPS))
    out = pl.pallas_call(
        functools.partial(_fno_c_kernel, scale=scale),
        grid=(B, ncol // tc),
        in_specs=[pl.BlockSpec((None, 2 * n2, tc), lambda b, t: (b, 0, t)),
                  pl.BlockSpec((n2, 2 * n2), lambda b, t: (0, 0))],
        out_specs=pl.BlockSpec((None, n2, tc), lambda b, t: (b, 0, t)),
        out_shape=jax.ShapeDtypeStruct((B, n2, ncol), BF16),
        compiler_params=_cparams(("parallel", "parallel")),
        name="fno_c",
    )(a2, f_c)
    return out.reshape(B * S, C)


def _fno_ctx_kernel(x_ref, w_ref, f_ref, o_ref, *, scale):
    x = x_ref[...].astype(BF16)
    u = jnp.dot(x, w_ref[0], preferred_element_type=F32).astype(BF16)
    v = jnp.dot(x, w_ref[1], preferred_element_type=F32).astype(BF16)
    uv = jnp.concatenate([u, v], axis=0)
    o_ref[...] = (jnp.dot(f_ref[...], uv, preferred_element_type=F32) * scale).astype(BF16)


def _fno_ctx(f, w_l, f_ctx, *, B, S, CTX):
    C = f.shape[1]
    ctx0 = (B * S) // CTX
    scale = 1.0 / math.sqrt(CTX * (C // FNO_GROUPS))
    return pl.pallas_call(
        functools.partial(_fno_ctx_kernel, scale=scale),
        grid=(B,),
        in_specs=[pl.BlockSpec((CTX, C), lambda b: (ctx0 + b, 0)),
                  pl.BlockSpec((2, C, C), lambda b: (0, 0, 0)),
                  pl.BlockSpec((CTX, 2 * CTX), lambda b: (0, 0))],
        out_specs=pl.BlockSpec((CTX, C), lambda b: (b, 0)),
        out_shape=jax.ShapeDtypeStruct((B * CTX, C), BF16),
        compiler_params=_cparams(("parallel",)),
        name="fno_ctx",
    )(f, w_l, f_ctx)


def _dft_constants(S, CTX, C):
    gd = C // FNO_GROUPS
    n1, n2 = FNO_N1, S // FNO_N1
    nj = n2 // SUBLANES
    idx = jnp.arange(gd, dtype=jnp.int32)
    ang = (2.0 * math.pi / gd) * ((idx[:, None] * idx[None, :]) % gd).astype(F32)
    eye_g = jnp.eye(FNO_GROUPS, dtype=F32)
    cs_bd = jnp.stack([jnp.kron(eye_g, jnp.cos(ang)), jnp.kron(eye_g, jnp.sin(ang))])
    k1 = jnp.arange(n1, dtype=jnp.int32)[:, None, None]
    nn1 = jnp.arange(n1, dtype=jnp.int32)[None, :, None]
    nn2 = jnp.arange(n2, dtype=jnp.int32)[None, None, :]
    ph = (2.0 * math.pi / S) * ((k1 * (n2 * nn1 + nn2)) % S).astype(F32)
    cph = jnp.cos(ph).reshape(n1, n1, nj, SUBLANES).transpose(2, 0, 3, 1)
    sph = jnp.sin(ph).reshape(n1, n1, nj, SUBLANES).transpose(2, 0, 3, 1)
    t4 = jnp.stack([jnp.stack([cph, -sph], axis=3), jnp.stack([-sph, -cph], axis=3)], axis=2)
    nr, nq = 2 * SUBLANES * n1, 2 * n1
    expand = (jnp.arange(nq * SUBLANES, dtype=jnp.int32)[None, :] // SUBLANES
              == jnp.arange(nq, dtype=jnp.int32)[:, None]).astype(BF16)
    m_a = jnp.einsum('jrq,qc->jrc', t4.reshape(nj, nr, nq).astype(BF16), expand, preferred_element_type=F32)
    same_s = (jnp.arange(nr, dtype=jnp.int32)[:, None] % SUBLANES
              == jnp.arange(nq * SUBLANES, dtype=jnp.int32)[None, :] % SUBLANES)
    m_a = jnp.where(same_s[None], m_a, 0.0).astype(BF16)
    i2 = jnp.arange(n2, dtype=jnp.int32)
    a2 = (2.0 * math.pi / n2) * ((i2[:, None] * i2[None, :]) % n2).astype(F32)
    f_c = jnp.stack([jnp.cos(a2).reshape(n2, nj, SUBLANES), jnp.sin(a2).reshape(n2, nj, SUBLANES)], axis=2)
    f_c = f_c.reshape(n2, 2 * n2).astype(BF16)
    ic = jnp.arange(CTX, dtype=jnp.int32)
    ac = (2.0 * math.pi / CTX) * ((ic[:, None] * ic[None, :]) % CTX).astype(F32)
    f_ctx = jnp.concatenate([jnp.cos(ac), -jnp.sin(ac)], axis=1).astype(BF16)
    return cs_bd, m_a, f_c, f_ctx


def _block_diag(w):
    G, n = w.shape[-3], w.shape[-1]
    eye = jnp.eye(G, dtype=w.dtype)
    out = jnp.einsum('...gij,gh->...gihj', w, eye)
    return out.reshape(w.shape[:-3] + (G * n, G * n))


def kernel(x, c, ctx, c_ctx, w_ada, b_ada, ln_g, ln_b, ff1_gate, ff1_up, ff1_down, ff2_gate, ff2_up, ff2_down,
           w_in, w_out, na_rpb, lru_conv_w, lru_conv_b, lru_wa, lru_ba, lru_wx, lru_bx, lru_lambda, fno_w):
    B, S, D = x.shape
    CTX = ctx.shape[1]
    L = w_ada.shape[0]
    na_w = na_rpb.shape[1] * HEAD_DIM
    lru_w = lru_conv_w.shape[2]
    alpha = float((2 * L) ** 0.25)
    tm = min(1024, B * CTX)
    tiles_per_batch = S // tm
    nlat = (B * S) // tm
    ntiles = nlat + (B * CTX) // tm
    assert B < 16 and S % tm == 0 and (B * CTX) % tm == 0 and S % (GRID_W * NA_KH) == 0

    cc = jnp.zeros((16, D), F32).at[:B].set(c).at[B].set(c_ctx)
    mods = _ada_params(cc, w_ada, b_ada)

    bf = lambda w: w.astype(BF16)
    ff1 = (bf(ff1_gate), bf(ff1_up), bf(ff1_down))
    ff2 = (bf(ff2_gate), bf(ff2_up), bf(ff2_down))
    w_in_b, w_out_b = bf(w_in), bf(w_out)
    ln_g4 = ln_g[:, :, None, :]
    ln_b4 = ln_b[:, :, None, :]
    wb = _na_bias_tables(na_rpb)
    lru_gate_w = bf(jnp.concatenate([_block_diag(lru_wa), _block_diag(lru_wx)], axis=-1))
    zrow = jnp.zeros((L, 2, SUBLANES - 3, lru_w), F32)
    lru_vec = jnp.concatenate([lru_ba[:, :, None], lru_bx[:, :, None], lru_lambda[:, :, None], zrow], axis=2)
    lru_cw = jnp.concatenate([lru_conv_w, lru_conv_b[:, None], jnp.zeros((L, SUBLANES - LRU_CONV - 1, lru_w), F32)], axis=1)
    cs_bd, m_a, f_c, f_ctx = _dft_constants(S, CTX, D - na_w - lru_w)
    fno_wcs = _fno_weights(_block_diag(fno_w), cs_bd)

    common = dict(tm=tm, tiles_per_batch=tiles_per_batch, nbatch=B)
    xs = (x.reshape(B * S, D), ctx.reshape(B * CTX, D))
    for l in range(L):
        last = l == L - 1
        x1 = _ffn(xs, mods, *ff1, ln_g4[:, 0], ln_b4[:, 0], j=0, l=l, alpha=alpha, nlat=nlat, ntiles=ntiles, **common)
        q, k, v, xr, gr, f = _inproj(x1, mods, w_in_b, l=l, na_w=na_w, lru_w=lru_w, **common)
        na = _attn_lat(q, k, v, wb, l=l, B=B, S=S, CTX=CTX)
        lru, hend = _lru_ctx(xr, gr, lru_cw[l], lru_gate_w[l], lru_vec[l], B=B, S=S, CTX=CTX)
        tt = min(S, 1024)
        lru_p = (lru_cw[l], lru_gate_w[l], lru_vec[l])
        hb = _lru_lat(xr, None, None, None, hend, *lru_p, reverse=True, B=B, S=S, tt=tt)
        lru = _lru_lat(xr, gr, hb, lru, hend, *lru_p, reverse=False, B=B, S=S, tt=tt)
        fno_lat = _fno_lat(f, fno_wcs[l], m_a, f_c, B=B, S=S)
        if last:
            fno_ctx, nt = None, nlat
        else:
            na = _attn_ctx(q, k, v, na, B=B, S=S, CTX=CTX)
            fno_ctx, nt = _fno_ctx(f, fno_wcs[l], f_ctx, B=B, S=S, CTX=CTX), ntiles
        x3 = _ffn((x1,), mods, *ff2, ln_g4[:, 2], ln_b4[:, 2], j=2, l=l, alpha=alpha, nlat=nlat, ntiles=nt,
                  mix=(na, lru, fno_lat, fno_ctx, w_out_b, ln_g4[:, 1], ln_b4[:, 1]), **common)
        xs = (x3,)
    return xs[0].reshape(B, S, D)
```

```python
import functools
import math

import numpy as np
import jax
import jax.numpy as jnp
from jax import lax
from jax.experimental import pallas as pl
from jax.experimental.pallas import tpu as pltpu

F32 = jnp.float32
BF16 = jnp.bfloat16
HIGHEST = lax.Precision.HIGHEST

HEAD_DIM = 64
GRID_W = 64
NA_KH = 8
NA_KW = 16
LRU_HEADS = 4
LRU_CONV = 4
LRU_C = 8.0
FNO_GROUPS = 4
N_MOD = 9
MACARON = 0.5
LN_EPS = 1e-5
MASK_BIAS = -1e30
LOG2E = math.log2(math.e)
SUBLANES = 8
FNO_N1 = 64
VMEM_LIMIT = 56 * 1024 * 1024


def _cparams(sem):
    return pltpu.CompilerParams(dimension_semantics=sem, vmem_limit_bytes=VMEM_LIMIT)


def _layer_norm(z, g, b):
    mu = jnp.mean(z, axis=-1, keepdims=True)
    zc = z - mu
    var = jnp.mean(zc * zc, axis=-1, keepdims=True)
    return zc * lax.rsqrt(var + LN_EPS) * g + b


def _ada_kernel(c_ref, w_ref, b_ref, o_ref):
    c = c_ref[...]
    a = c * jax.nn.sigmoid(c)
    o_ref[...] = jnp.dot(a, w_ref[...], preferred_element_type=F32, precision=HIGHEST) + b_ref[...]


def _ada_params(cc, w_ada, b_ada):
    L, D, ND = w_ada.shape
    tn = D
    out = pl.pallas_call(
        _ada_kernel,
        grid=(L, ND // tn),
        in_specs=[pl.BlockSpec((16, D), lambda l, n: (0, 0)),
                  pl.BlockSpec((None, D, tn), lambda l, n: (l, 0, n)),
                  pl.BlockSpec((None, 1, tn), lambda l, n: (l, 0, n))],
        out_specs=pl.BlockSpec((None, 16, tn), lambda l, n: (l, 0, n)),
        out_shape=jax.ShapeDtypeStruct((L, 16, ND), F32),
        compiler_params=_cparams(("parallel", "parallel")),
        name="ada",
    )(cc, w_ada, b_ada.reshape(L, 1, ND))
    return out.reshape(L, 16, N_MOD, D)


def _ffn_kernel(*refs, j, tf, alpha, nlat, mode):
    m = refs[2 if mode == "two_x" else 1][...]
    if mode == "two_x":
        xl_ref, xc_ref, _, wg_ref, wu_ref, wd_ref, g_ref, b_ref, o_ref, h_ref, a_ref = refs
        x = jnp.where(pl.program_id(0) < nlat, xl_ref[...], xc_ref[...])
    elif mode == "plain":
        x_ref, _, wg_ref, wu_ref, wd_ref, g_ref, b_ref, o_ref, h_ref, a_ref = refs
        x = x_ref[...]
    else:
        if mode == "mix_ctx":
            (x_ref, _, na_ref, lru_ref, fl_ref, fc_ref, wo_ref, g1_ref, b1_ref,
             wg_ref, wu_ref, wd_ref, g_ref, b_ref, o_ref, h_ref, a_ref) = refs
            fno = jnp.where(pl.program_id(0) < nlat, fl_ref[...], fc_ref[...])
        else:
            (x_ref, _, na_ref, lru_ref, fl_ref, wo_ref, g1_ref, b1_ref,
             wg_ref, wu_ref, wd_ref, g_ref, b_ref, o_ref, h_ref, a_ref) = refs
            fno = fl_ref[...]
        na_w, lru_w = na_ref.shape[1], lru_ref.shape[1]
        y = jnp.dot(na_ref[...], wo_ref[0:na_w, :], preferred_element_type=F32)
        y = y + jnp.dot(lru_ref[...], wo_ref[na_w:na_w + lru_w, :], preferred_element_type=F32)
        y = y + jnp.dot(fno, wo_ref[na_w + lru_w:, :], preferred_element_type=F32)
        o_ref[...] = _layer_norm(alpha * x_ref[...] + m[5:6] * y, g1_ref[...], b1_ref[...])
        x = o_ref[...]
    shift, scale, gate = m[3 * j:3 * j + 1], m[3 * j + 1:3 * j + 2], m[3 * j + 2:3 * j + 3]
    h_ref[...] = (x * (1.0 + scale) + shift).astype(BF16)
    for c in range(wg_ref.shape[1] // tf):
        cols = slice(c * tf, (c + 1) * tf)
        h = h_ref[...]
        g = jnp.dot(h, wg_ref[:, cols], preferred_element_type=F32)
        u = jnp.dot(h, wu_ref[:, cols], preferred_element_type=F32)
        a_ref[:, cols] = (g * jax.nn.sigmoid(g) * u).astype(BF16)
    y = jnp.dot(a_ref[...], wd_ref[...], preferred_element_type=F32)
    if mode.startswith("mix"):
        x = o_ref[...]
    z = alpha * x + (MACARON * gate) * y
    o_ref[...] = _layer_norm(z, g_ref[...], b_ref[...])


def _ffn(xs, mods_l, wg, wu, wd, ln_g, ln_b, *, j, l, alpha, tm, tiles_per_batch, nlat, ntiles, nbatch, mix=None):
    two_x = len(xs) == 2
    mode = "two_x" if two_x else "plain"
    D = xs[0].shape[1]
    Fd = wg.shape[2]
    tf = 256
    row_idx = lambda i: jnp.minimum(i // tiles_per_batch, nbatch)
    if two_x:
        nctx = ntiles - nlat
        x_specs = [pl.BlockSpec((tm, D), lambda i: (jnp.minimum(i, nlat - 1), 0)),
                   pl.BlockSpec((tm, D), lambda i: (jnp.clip(i - nlat, 0, nctx - 1), 0))]
    else:
        x_specs = [pl.BlockSpec((tm, D), lambda i: (i, 0))]
    const = pl.Buffered(1)
    mix_specs, mix_args = [], []
    if mix is not None:
        na, lru, fno_l, fno_c, w_out, g1, b1 = mix
        mode = "mix" if fno_c is None else "mix_ctx"
        tile = lambda a: pl.BlockSpec((tm, a.shape[1]), lambda i: (i, 0))
        mix_specs = [tile(na), tile(lru), pl.BlockSpec((tm, fno_l.shape[1]), lambda i: (jnp.minimum(i, nlat - 1), 0))]
        mix_args = [na, lru, fno_l]
        if fno_c is not None:
            mix_specs.append(pl.BlockSpec((tm, fno_c.shape[1]), lambda i: (jnp.clip(i - nlat, 0, ntiles - nlat - 1), 0)))
            mix_args.append(fno_c)
        mix_specs += [pl.BlockSpec((None, D, D), lambda i: (l, 0, 0), pipeline_mode=const),
                      pl.BlockSpec((None, 1, D), lambda i: (l, 0, 0)),
                      pl.BlockSpec((None, 1, D), lambda i: (l, 0, 0))]
        mix_args += [w_out, g1, b1]
    return pl.pallas_call(
        functools.partial(_ffn_kernel, j=j, tf=tf, alpha=alpha, nlat=nlat, mode=mode),
        grid=(ntiles,),
        in_specs=x_specs + [
            pl.BlockSpec((None, None, N_MOD, D), lambda i: (l, row_idx(i), 0, 0))] + mix_specs + [
            pl.BlockSpec((None, D, Fd), lambda i: (l, 0, 0), pipeline_mode=const),
            pl.BlockSpec((None, D, Fd), lambda i: (l, 0, 0), pipeline_mode=const),
            pl.BlockSpec((None, Fd, D), lambda i: (l, 0, 0), pipeline_mode=const),
            pl.BlockSpec((None, 1, D), lambda i: (l, 0, 0)),
            pl.BlockSpec((None, 1, D), lambda i: (l, 0, 0))],
        out_specs=pl.BlockSpec((tm, D), lambda i: (i, 0)),
        out_shape=jax.ShapeDtypeStruct((ntiles * tm, D), F32),
        scratch_shapes=[pltpu.VMEM((tm, D), BF16), pltpu.VMEM((tm, Fd), BF16)],
        compiler_params=_cparams(("parallel",)),
        name=f"ffn{j}",
    )(*xs, mods_l, *mix_args, wg, wu, wd, ln_g, ln_b)


def _inproj_kernel(x_ref, m_ref, w_ref, q_ref, k_ref, v_ref, xr_ref, gr_ref, f_ref, *, na_w, lru_w):
    m = m_ref[...]
    h = (x_ref[...] * (1.0 + m[4:5]) + m[3:4]).astype(BF16)

    def proj(lo, width):
        return jnp.dot(h, w_ref[:, lo:lo + width], preferred_element_type=F32)

    q_ref[...] = (proj(0, na_w) * (HEAD_DIM ** -0.5 * LOG2E)).astype(BF16)
    k_ref[...] = proj(na_w, na_w).astype(BF16)
    v_ref[...] = proj(2 * na_w, na_w).astype(BF16)
    xr_ref[...] = proj(3 * na_w, lru_w)
    gr_ref[...] = proj(3 * na_w + lru_w, lru_w)
    f_ref[...] = proj(3 * na_w + 2 * lru_w, w_ref.shape[1] - 3 * na_w - 2 * lru_w)


def _inproj(x_all, mods_l, w_in, *, l, tm, tiles_per_batch, nbatch, na_w, lru_w):
    T, D = x_all.shape
    cols = w_in.shape[2]
    fno_w = cols - 3 * na_w - 2 * lru_w
    row_idx = lambda i: jnp.minimum(i // tiles_per_batch, nbatch)
    widths = [(na_w, BF16)] * 3 + [(lru_w, F32)] * 2 + [(fno_w, F32)]
    return pl.pallas_call(
        functools.partial(_inproj_kernel, na_w=na_w, lru_w=lru_w),
        grid=(T // tm,),
        in_specs=[pl.BlockSpec((tm, D), lambda i: (i, 0)),
                  pl.BlockSpec((None, None, N_MOD, D), lambda i: (l, row_idx(i), 0, 0)),
                  pl.BlockSpec((None, D, cols), lambda i: (l, 0, 0), pipeline_mode=pl.Buffered(1))],
        out_specs=[pl.BlockSpec((tm, w), lambda i: (i, 0)) for w, _ in widths],
        out_shape=[jax.ShapeDtypeStruct((T, w), dt) for w, dt in widths],
        compiler_params=_cparams(("parallel",)),
        name="inproj",
    )(x_all, mods_l, w_in)


def _na_bias_tables(rpb):
    L, H = rpb.shape[:2]
    W, ndr, ndc = GRID_W, 2 * NA_KH - 1, 2 * NA_KW - 1
    c = np.arange(W)[:, None]
    j = np.arange(W)[None, :]
    col0 = np.clip(c - NA_KW // 2, 0, W - NA_KW)
    valid = (j >= col0) & (j < col0 + NA_KW)
    entry = np.where(valid, np.clip(j - c + (NA_KW - 1), 0, ndc - 1), ndc)
    onehot = (np.arange(ndc + 1)[:, None, None] == entry[None]).astype(np.float32)
    ext = jnp.concatenate([rpb * LOG2E, jnp.full((L, H, ndr, 1), MASK_BIAS, F32)], axis=-1)
    bt = jnp.einsum('lhde,ecj->lhcdj', ext, onehot, precision=HIGHEST).reshape(L, H, W, ndr * W)
    wb = jnp.stack([bt[..., (NA_KH - 1 - case) * W:(2 * NA_KH - 1 - case) * W] for case in range(NA_KH)], axis=2)
    return wb.reshape(L, H // 2, 2, NA_KH, W, NA_KH * W)


def _masked_heads(q):
    lane = lax.broadcasted_iota(jnp.int32, q.shape, 1)
    qf = q.astype(F32)
    return (jnp.where(lane < HEAD_DIM, qf, 0.0).astype(BF16),
            jnp.where(lane >= HEAD_DIM, qf, 0.0).astype(BF16))


def _qk(q, k):
    return lax.dot_general(q, k, (((1,), (1,)), ((), ())), preferred_element_type=F32)


def _attn_kernel(q_ref, k_ref, v_ref, kc_ref, vc_ref, wb_ref, o_ref, *, R, G, rows):
    jb = pl.program_id(2)
    kc = kc_ref[...]
    vc = vc_ref[...]
    W = GRID_W
    nwin = NA_KH * W
    lane = lax.broadcasted_iota(jnp.int32, (W, 2 * HEAD_DIM), 1)

    def group(g, carry):
        qoff = pl.multiple_of(g * (R * W), R * W)
        qcat = []
        for rr in range(R):
            qe, qo = _masked_heads(q_ref[pl.ds(qoff + rr * W, W), :])
            qcat += [qe, qo]
        s_ctx_all = _qk(jnp.concatenate(qcat, axis=0), kc)
        o_loc, p_ctx, dens = [], [], []
        for rr in range(R):
            r = (jb * G + g) * R + rr
            rs = jnp.clip(r - NA_KH // 2, 0, rows - NA_KH)
            koff = pl.multiple_of(rs * W, W)
            kw = k_ref[pl.ds(koff, nwin), :]
            vw = v_ref[pl.ds(koff, nwin), :]
            qc = jnp.concatenate(qcat[2 * rr:2 * rr + 2], axis=0)
            s_loc = _qk(qc, kw) + jnp.concatenate([wb_ref[0, r - rs], wb_ref[1, r - rs]], axis=0)
            s_ctx = s_ctx_all[rr * 2 * W:(rr + 1) * 2 * W]
            m = jnp.maximum(jnp.max(s_loc, axis=-1, keepdims=True), jnp.max(s_ctx, axis=-1, keepdims=True))
            p_loc = jnp.exp2(s_loc - m)
            pc = jnp.exp2(s_ctx - m)
            dens.append(jnp.sum(p_loc, axis=-1, keepdims=True) + jnp.sum(pc, axis=-1, keepdims=True))
            o_loc.append(jnp.dot(p_loc.astype(BF16), vw, preferred_element_type=F32))
            p_ctx.append(pc.astype(BF16))
        o_ctx_all = jnp.dot(jnp.concatenate(p_ctx, axis=0), vc, preferred_element_type=F32)
        outs = []
        for rr in range(R):
            o = (o_loc[rr] + o_ctx_all[rr * 2 * W:(rr + 1) * 2 * W]) / dens[rr]
            outs.append(jnp.where(lane < HEAD_DIM, o[:W], o[W:]).astype(BF16))
        o_ref[pl.ds(qoff, R * W), :] = jnp.concatenate(outs, axis=0)
        return carry

    lax.fori_loop(0, G, group, 0, unroll=True)


def _attn_lat(q, k, v, wb, *, l, B, S, CTX):
    T, na_w = q.shape
    npair = na_w // (2 * HEAD_DIM)
    rows = S // GRID_W
    R = 8
    G = 8 if rows % (8 * R) == 0 else 1
    nrb = rows // (R * G)
    pw = 2 * HEAD_DIM
    ctx0 = (B * S) // CTX
    qblk = R * G * GRID_W
    return pl.pallas_call(
        functools.partial(_attn_kernel, R=R, G=G, rows=rows),
        grid=(B, npair, nrb),
        in_specs=[pl.BlockSpec((qblk, pw), lambda b, p, j: (b * nrb + j, p)),
                  pl.BlockSpec((S, pw), lambda b, p, j: (b, p)),
                  pl.BlockSpec((S, pw), lambda b, p, j: (b, p)),
                  pl.BlockSpec((CTX, pw), lambda b, p, j: (ctx0 + b, p)),
                  pl.BlockSpec((CTX, pw), lambda b, p, j: (ctx0 + b, p)),
                  pl.BlockSpec((None, None, 2, NA_KH, GRID_W, NA_KH * GRID_W), lambda b, p, j: (l, p, 0, 0, 0, 0))],
        out_specs=pl.BlockSpec((qblk, pw), lambda b, p, j: (b * nrb + j, p)),
        out_shape=jax.ShapeDtypeStruct((T, na_w), BF16),
        compiler_params=_cparams(("parallel", "parallel", "parallel")),
        name="attn_lat",
    )(q, k, v, k, v, wb)


def _attn_ctx_kernel(q_ref, k_ref, v_ref, na_hbm_ref, o_ref):
    del na_hbm_ref
    k = k_ref[...]
    v = v_ref[...]
    lane = lax.broadcasted_iota(jnp.int32, o_ref.shape, 1)
    outs = []
    for qm in _masked_heads(q_ref[...]):
        s = _qk(qm, k)
        p = jnp.exp2(s - jnp.max(s, axis=-1, keepdims=True))
        den = jnp.sum(p, axis=-1, keepdims=True)
        outs.append(jnp.dot(p.astype(BF16), v, preferred_element_type=F32) / den)
    o_ref[...] = jnp.where(lane < HEAD_DIM, outs[0], outs[1]).astype(BF16)


def _attn_ctx(q, k, v, na, *, B, S, CTX):
    T, na_w = q.shape
    pw = 2 * HEAD_DIM
    ctx0 = (B * S) // CTX
    spec = pl.BlockSpec((CTX, pw), lambda b, p: (ctx0 + b, p))
    return pl.pallas_call(
        _attn_ctx_kernel,
        grid=(B, na_w // pw),
        in_specs=[spec, spec, spec, pl.BlockSpec(memory_space=pl.ANY)],
        out_specs=spec,
        out_shape=jax.ShapeDtypeStruct((T, na_w), BF16),
        input_output_aliases={3: 0},
        compiler_params=_cparams(("parallel", "parallel")),
        name="attn_ctx",
    )(q, k, v, na)


def _conv4(x_prev, x_main, x_next, cw):
    n = x_main.shape[0]
    xe = jnp.concatenate([x_prev, x_main, x_next], axis=0)
    ne = n + 2 * SUBLANES
    y = cw[2:3] * x_main
    y = y + cw[0:1] * pltpu.roll(xe, 2, 0)[SUBLANES:SUBLANES + n]
    y = y + cw[1:2] * pltpu.roll(xe, 1, 0)[SUBLANES:SUBLANES + n]
    y = y + cw[3:4] * pltpu.roll(xe, ne - 1, 0)[SUBLANES:SUBLANES + n]
    return y + cw[4:5]


def _softplus(x):
    return jnp.maximum(x, 0.0) + jnp.log1p(jnp.exp(-jnp.abs(x)))


def _lru_coeffs(xl, w, vec):
    C = xl.shape[1]
    gates = jnp.dot(xl.astype(BF16), w, preferred_element_type=F32)
    r = jax.nn.sigmoid(gates[:, :C] + vec[0:1])
    i = jax.nn.sigmoid(gates[:, C:] + vec[1:2])
    log_a = (-LRU_C * r) * _softplus(-vec[2:3])
    a = jnp.exp(log_a)
    b = jnp.sqrt(-jnp.tanh(log_a) * (1.0 + a * a)) * (i * xl)
    return a, b


def _group_scan(a, b, reverse):
    n, C = a.shape
    a = a.reshape(n // SUBLANES, SUBLANES, C)
    b = b.reshape(n // SUBLANES, SUBLANES, C)
    row = lax.broadcasted_iota(jnp.int32, a.shape, 1)
    for k in (1, 2, 4):
        shift = SUBLANES - k if reverse else k
        mask = (row < SUBLANES - k) if reverse else (row >= k)
        a_s = pltpu.roll(a, shift, 1)
        b_s = pltpu.roll(b, shift, 1)
        b = jnp.where(mask, a * b_s + b, b)
        a = jnp.where(mask, a * a_s, a)
    return a.reshape(n, C), b.reshape(n, C)


def _chunk_scan(a, b, h_in, reverse, acum_ref, bcum_ref, out_ref):
    n, C = a.shape
    ac, bc = _group_scan(a, b, reverse)
    acum_ref[...] = ac
    bcum_ref[...] = bc
    ng = n // SUBLANES

    def body(g, hb):
        idx = (ng - 1 - g) if reverse else g
        sl = pl.ds(pl.multiple_of(idx * SUBLANES, SUBLANES), SUBLANES)
        h = acum_ref[sl, :] * hb + bcum_ref[sl, :]
        out_ref[sl, :] = h
        edge = h[0:1] if reverse else h[SUBLANES - 1:SUBLANES]
        return jnp.broadcast_to(edge, (SUBLANES, C))

    return lax.fori_loop(0, ng, body, h_in, unroll=8)


def _lru_ctx_kernel(xr_ref, gr_ref, cw_ref, w_ref, vec_ref, o_ref, hend_ref, acum_ref, bcum_ref, hf_ref, hb_ref):
    xr = xr_ref[...]
    zeros8 = jnp.zeros((SUBLANES, xr.shape[1]), F32)
    xl = _conv4(zeros8, xr, zeros8, cw_ref[...])
    a, b = _lru_coeffs(xl, w_ref[0], vec_ref[0])
    hf_end = _chunk_scan(a, b, zeros8, False, acum_ref, bcum_ref, hf_ref)
    a, b = _lru_coeffs(xl, w_ref[1], vec_ref[1])
    hb_end = _chunk_scan(a, b, zeros8, True, acum_ref, bcum_ref, hb_ref)
    o_ref[...] = ((hf_ref[...] + hb_ref[...]) * jax.nn.gelu(gr_ref[...])).astype(BF16)
    row = lax.broadcasted_iota(jnp.int32, hf_end.shape, 0)
    hend_ref[...] = jnp.where(row == 0, hf_end, hb_end)


def _lru_ctx(xr, gr, cw, w, vec, *, B, S, CTX):
    C = xr.shape[1]
    ctx0 = (B * S) // CTX
    spec = pl.BlockSpec((CTX, C), lambda b: (ctx0 + b, 0))
    return pl.pallas_call(
        _lru_ctx_kernel,
        grid=(B,),
        in_specs=[spec, spec,
                  pl.BlockSpec((SUBLANES, C), lambda b: (0, 0)),
                  pl.BlockSpec((2, C, 2 * C), lambda b: (0, 0, 0)),
                  pl.BlockSpec((2, SUBLANES, C), lambda b: (0, 0, 0))],
        out_specs=[spec,
                   pl.BlockSpec((None, SUBLANES, C), lambda b: (b, 0, 0))],
        out_shape=[jax.ShapeDtypeStruct((xr.shape[0], C), BF16),
                   jax.ShapeDtypeStruct((B, SUBLANES, C), F32)],
        scratch_shapes=[pltpu.VMEM((CTX, C), F32)] * 4,
        compiler_params=_cparams(("parallel",)),
        name="lru_ctx",
    )(xr, gr, cw, w, vec)


def _lru_lat_kernel(*refs, reverse, nch):
    if reverse:
        xr_ref, xp_ref, xn_ref, hend_ref, cw_ref, w_ref, vec_ref, o_ref, carry_ref, acum_ref, bcum_ref = refs
    else:
        (xr_ref, xp_ref, xn_ref, hend_ref, cw_ref, w_ref, vec_ref, gr_ref, hb_ref, _lru_hbm_ref,
         o_ref, carry_ref, acum_ref, bcum_ref, hf_ref) = refs
    i = pl.program_id(1)
    c = (nch - 1 - i) if reverse else i
    C = xr_ref.shape[1]

    @pl.when(i == 0)
    def _():
        row = hend_ref[1:2, :] if reverse else hend_ref[0:1, :]
        carry_ref[...] = jnp.broadcast_to(row, (SUBLANES, C))

    xp = jnp.where(c == 0, 0.0, xp_ref[...])
    xn = jnp.where(c == nch - 1, 0.0, xn_ref[...])
    xl = _conv4(xp, xr_ref[...], xn, cw_ref[...])
    a, b = _lru_coeffs(xl, w_ref[...], vec_ref[...])
    if reverse:
        carry_ref[...] = _chunk_scan(a, b, carry_ref[...], True, acum_ref, bcum_ref, o_ref)
    else:
        carry_ref[...] = _chunk_scan(a, b, carry_ref[...], False, acum_ref, bcum_ref, hf_ref)
        o_ref[...] = ((hf_ref[...] + hb_ref[...]) * jax.nn.gelu(gr_ref[...])).astype(BF16)


def _lru_lat(xr, gr, hb, lru, hend, cw, w, vec, *, reverse, B, S, tt):
    C = xr.shape[1]
    nch = S // tt
    d = 1 if reverse else 0
    chunk = (lambda i: nch - 1 - i) if reverse else (lambda i: i)
    nblk8 = (B * S) // SUBLANES
    main = pl.BlockSpec((tt, C), lambda b, i: (b * nch + chunk(i), 0))
    in_specs = [
        main,
        pl.BlockSpec((SUBLANES, C), lambda b, i: (jnp.maximum((b * nch + chunk(i)) * (tt // SUBLANES) - 1, 0), 0)),
        pl.BlockSpec((SUBLANES, C), lambda b, i: (jnp.minimum((b * nch + chunk(i) + 1) * (tt // SUBLANES), nblk8 - 1), 0)),
        pl.BlockSpec((None, SUBLANES, C), lambda b, i: (b, 0, 0)),
        pl.BlockSpec((SUBLANES, C), lambda b, i: (0, 0)),
        pl.BlockSpec((None, C, 2 * C), lambda b, i: (d, 0, 0)),
        pl.BlockSpec((None, SUBLANES, C), lambda b, i: (d, 0, 0))]
    args = [xr, xr, xr, hend, cw, w, vec]
    scratch = [pltpu.VMEM((SUBLANES, C), F32), pltpu.VMEM((tt, C), F32), pltpu.VMEM((tt, C), F32)]
    if reverse:
        out_shape, aliases = jax.ShapeDtypeStruct((B * S, C), F32), {}
    else:
        in_specs += [main, main, pl.BlockSpec(memory_space=pl.ANY)]
        args += [gr, hb, lru]
        scratch.append(pltpu.VMEM((tt, C), F32))
        out_shape, aliases = jax.ShapeDtypeStruct(lru.shape, BF16), {len(args) - 1: 0}
    return pl.pallas_call(
        functools.partial(_lru_lat_kernel, reverse=reverse, nch=nch),
        grid=(B, nch),
        in_specs=in_specs,
        out_specs=main,
        out_shape=out_shape,
        input_output_aliases=aliases,
        scratch_shapes=scratch,
        compiler_params=_cparams(("parallel", "arbitrary")),
        name="lru_bwd" if reverse else "lru_fwd",
    )(*args)


def _fno_w_kernel(fw_ref, cs_ref, o_ref):
    fw = fw_ref[...]
    o_ref[0] = jnp.dot(cs_ref[0], fw, preferred_element_type=F32, precision=HIGHEST).astype(BF16)
    o_ref[1] = jnp.dot(cs_ref[1], fw, preferred_element_type=F32, precision=HIGHEST).astype(BF16)


def _fno_weights(fw_bd, cs_bd):
    L, C, _ = fw_bd.shape
    return pl.pallas_call(
        _fno_w_kernel,
        grid=(L,),
        in_specs=[pl.BlockSpec((None, C, C), lambda l: (l, 0, 0)),
                  pl.BlockSpec((2, C, C), lambda l: (0, 0, 0))],
        out_specs=pl.BlockSpec((None, 2, C, C), lambda l: (l, 0, 0, 0)),
        out_shape=jax.ShapeDtypeStruct((L, 2, C, C), BF16),
        compiler_params=_cparams(("parallel",)),
        name="fno_w",
    )(fw_bd, cs_bd)


def _fno_a_kernel(x_ref, w_ref, m_ref, o_ref, *, nb):
    n1 = x_ref.shape[0] // nb
    _, s, C = x_ref.shape
    uvs = []
    for bb in range(nb):
        x = x_ref[bb * n1:(bb + 1) * n1].reshape(n1 * s, C).astype(BF16)
        u = jnp.dot(x, w_ref[0], preferred_element_type=F32).astype(BF16)
        v = jnp.dot(x, w_ref[1], preferred_element_type=F32).astype(BF16)
        uvs.append(jnp.concatenate([u, v], axis=0))
    uv = jnp.concatenate(uvs, axis=1)
    a = jnp.dot(m_ref[...], uv, preferred_element_type=F32).astype(BF16)
    rows = 2 * s
    for bb in range(nb):
        for k1 in range(n1):
            o_ref[bb, :, k1 * C:(k1 + 1) * C] = a[k1 * rows:(k1 + 1) * rows, bb * C:(bb + 1) * C]


def _fno_c_kernel(a_ref, f_ref, o_ref, *, scale):
    res = jnp.dot(f_ref[...], a_ref[...], preferred_element_type=F32) * scale
    o_ref[...] = res.reshape(o_ref.shape).astype(BF16)


def _fno_lat(f, w_l, m_a, f_c, *, B, S):
    T, C = f.shape
    n1 = FNO_N1
    n2 = S // n1
    nj = n2 // SUBLANES
    nb = 4 if B % 4 == 0 else 1
    f4 = f.reshape(T // n2, nj, SUBLANES, C)
    rows_a = 2 * SUBLANES * n1
    a2 = pl.pallas_call(
        functools.partial(_fno_a_kernel, nb=nb),
        grid=(nj, B // nb),
        in_specs=[pl.BlockSpec((nb * n1, None, SUBLANES, C), lambda j, b: (b, j, 0, 0)),
                  pl.BlockSpec((2, C, C), lambda j, b: (0, 0, 0)),
                  pl.BlockSpec((None, rows_a, rows_a), lambda j, b: (j, 0, 0))],
        out_specs=pl.BlockSpec((nb, 2 * SUBLANES, n1 * C), lambda j, b: (b, j, 0)),
        out_shape=jax.ShapeDtypeStruct((B, 2 * n2, n1 * C), BF16),
        compiler_params=_cparams(("parallel", "parallel")),
        name="fno_a",
    )(f4, w_l, m_a)
    ncol = n1 * C
    tc = min(ncol, 4096)
    scale = 1.0 / math.sqrt(S * (C // FNO_GROUPS))
    out = pl.pallas_call(
        functools.partial(_fno_c_kernel, scale=scale),
        grid=(B, ncol // tc),
        in_specs=[pl.BlockSpec((None, 2 * n2, tc), lambda b, t: (b, 0, t)),
                  pl.BlockSpec((n2, 2 * n2), lambda b, t: (0, 0))],
        out_specs=pl.BlockSpec((None, n2, tc // C, C), lambda b, t: (b, 0, t, 0)),
        out_shape=jax.ShapeDtypeStruct((B, n2, n1, C), BF16),
        compiler_params=_cparams(("parallel", "parallel")),
        name="fno_c",
    )(a2, f_c)
    return out.reshape(B * S, C)


def _fno_ctx_kernel(x_ref, w_ref, f_ref, o_ref, *, scale):
    x = x_ref[...].astype(BF16)
    u = jnp.dot(x, w_ref[0], preferred_element_type=F32).astype(BF16)
    v = jnp.dot(x, w_ref[1], preferred_element_type=F32).astype(BF16)
    uv = jnp.concatenate([u, v], axis=0)
    o_ref[...] = (jnp.dot(f_ref[...], uv, preferred_element_type=F32) * scale).astype(BF16)


def _fno_ctx(f, w_l, f_ctx, *, B, S, CTX):
    C = f.shape[1]
    ctx0 = (B * S) // CTX
    scale = 1.0 / math.sqrt(CTX * (C // FNO_GROUPS))
    return pl.pallas_call(
        functools.partial(_fno_ctx_kernel, scale=scale),
        grid=(B,),
        in_specs=[pl.BlockSpec((CTX, C), lambda b: (ctx0 + b, 0)),
                  pl.BlockSpec((2, C, C), lambda b: (0, 0, 0)),
                  pl.BlockSpec((CTX, 2 * CTX), lambda b: (0, 0))],
        out_specs=pl.BlockSpec((CTX, C), lambda b: (b, 0)),
        out_shape=jax.ShapeDtypeStruct((B * CTX, C), BF16),
        compiler_params=_cparams(("parallel",)),
        name="fno_ctx",
    )(f, w_l, f_ctx)


def _dft_constants(S, CTX, C):
    gd = C // FNO_GROUPS
    n1, n2 = FNO_N1, S // FNO_N1
    nj = n2 // SUBLANES
    idx = jnp.arange(gd, dtype=jnp.int32)
    ang = (2.0 * math.pi / gd) * ((idx[:, None] * idx[None, :]) % gd).astype(F32)
    eye_g = jnp.eye(FNO_GROUPS, dtype=F32)
    cs_bd = jnp.stack([jnp.kron(eye_g, jnp.cos(ang)), jnp.kron(eye_g, jnp.sin(ang))])
    k1 = jnp.arange(n1, dtype=jnp.int32)[:, None, None]
    nn1 = jnp.arange(n1, dtype=jnp.int32)[None, :, None]
    nn2 = jnp.arange(n2, dtype=jnp.int32)[None, None, :]
    ph = (2.0 * math.pi / S) * ((k1 * (n2 * nn1 + nn2)) % S).astype(F32)
    cph = jnp.cos(ph).reshape(n1, n1, nj, SUBLANES).transpose(2, 0, 3, 1)
    sph = jnp.sin(ph).reshape(n1, n1, nj, SUBLANES).transpose(2, 0, 3, 1)
    t4 = jnp.stack([jnp.stack([cph, -sph], axis=3), jnp.stack([-sph, -cph], axis=3)], axis=2)
    nr, nq = 2 * SUBLANES * n1, 2 * n1
    expand = (jnp.arange(nq * SUBLANES, dtype=jnp.int32)[None, :] // SUBLANES
              == jnp.arange(nq, dtype=jnp.int32)[:, None]).astype(BF16)
    m_a = jnp.einsum('jrq,qc->jrc', t4.reshape(nj, nr, nq).astype(BF16), expand, preferred_element_type=F32)
    same_s = (jnp.arange(nr, dtype=jnp.int32)[:, None] % SUBLANES
              == jnp.arange(nq * SUBLANES, dtype=jnp.int32)[None, :] % SUBLANES)
    m_a = jnp.where(same_s[None], m_a, 0.0).astype(BF16)
    i2 = jnp.arange(n2, dtype=jnp.int32)
    a2 = (2.0 * math.pi / n2) * ((i2[:, None] * i2[None, :]) % n2).astype(F32)
    f_c = jnp.stack([jnp.cos(a2).reshape(n2, nj, SUBLANES), jnp.sin(a2).reshape(n2, nj, SUBLANES)], axis=2)
    f_c = f_c.reshape(n2, 2 * n2).astype(BF16)
    ic = jnp.arange(CTX, dtype=jnp.int32)
    ac = (2.0 * math.pi / CTX) * ((ic[:, None] * ic[None, :]) % CTX).astype(F32)
    f_ctx = jnp.concatenate([jnp.cos(ac), -jnp.sin(ac)], axis=1).astype(BF16)
    return cs_bd, m_a, f_c, f_ctx


def _block_diag(w):
    G, n = w.shape[-3], w.shape[-1]
    eye = jnp.eye(G, dtype=w.dtype)
    out = jnp.einsum('...gij,gh->...gihj', w, eye)
    return out.reshape(w.shape[:-3] + (G * n, G * n))


def kernel(x, c, ctx, c_ctx, w_ada, b_ada, ln_g, ln_b, ff1_gate, ff1_up, ff1_down, ff2_gate, ff2_up, ff2_down,
           w_in, w_out, na_rpb, lru_conv_w, lru_conv_b, lru_wa, lru_ba, lru_wx, lru_bx, lru_lambda, fno_w):
    B, S, D = x.shape
    CTX = ctx.shape[1]
    L = w_ada.shape[0]
    na_w = na_rpb.shape[1] * HEAD_DIM
    lru_w = lru_conv_w.shape[2]
    alpha = float((2 * L) ** 0.25)
    tm = min(1024, B * CTX)
    tiles_per_batch = S // tm
    nlat = (B * S) // tm
    ntiles = nlat + (B * CTX) // tm
    assert B < 16 and S % tm == 0 and (B * CTX) % tm == 0 and S % (GRID_W * NA_KH) == 0

    cc = jnp.zeros((16, D), F32).at[:B].set(c).at[B].set(c_ctx)
    mods = _ada_params(cc, w_ada, b_ada)

    bf = lambda w: w.astype(BF16)
    ff1 = (bf(ff1_gate), bf(ff1_up), bf(ff1_down))
    ff2 = (bf(ff2_gate), bf(ff2_up), bf(ff2_down))
    w_in_b, w_out_b = bf(w_in), bf(w_out)
    ln_g4 = ln_g[:, :, None, :]
    ln_b4 = ln_b[:, :, None, :]
    wb = _na_bias_tables(na_rpb)
    lru_gate_w = bf(jnp.concatenate([_block_diag(lru_wa), _block_diag(lru_wx)], axis=-1))
    zrow = jnp.zeros((L, 2, SUBLANES - 3, lru_w), F32)
    lru_vec = jnp.concatenate([lru_ba[:, :, None], lru_bx[:, :, None], lru_lambda[:, :, None], zrow], axis=2)
    lru_cw = jnp.concatenate([lru_conv_w, lru_conv_b[:, None], jnp.zeros((L, SUBLANES - LRU_CONV - 1, lru_w), F32)], axis=1)
    cs_bd, m_a, f_c, f_ctx = _dft_constants(S, CTX, D - na_w - lru_w)
    fno_wcs = _fno_weights(_block_diag(fno_w), cs_bd)

    common = dict(tm=tm, tiles_per_batch=tiles_per_batch, nbatch=B)
    xs = (x.reshape(B * S, D), ctx.reshape(B * CTX, D))
    for l in range(L):
        last = l == L - 1
        x1 = _ffn(xs, mods, *ff1, ln_g4[:, 0], ln_b4[:, 0], j=0, l=l, alpha=alpha, nlat=nlat, ntiles=ntiles, **common)
        q, k, v, xr, gr, f = _inproj(x1, mods, w_in_b, l=l, na_w=na_w, lru_w=lru_w, **common)
        na = _attn_lat(q, k, v, wb, l=l, B=B, S=S, CTX=CTX)
        lru, hend = _lru_ctx(xr, gr, lru_cw[l], lru_gate_w[l], lru_vec[l], B=B, S=S, CTX=CTX)
        tt = min(S, 2048)
        lru_p = (lru_cw[l], lru_gate_w[l], lru_vec[l])
        hb = _lru_lat(xr, None, None, None, hend, *lru_p, reverse=True, B=B, S=S, tt=tt)
        lru = _lru_lat(xr, gr, hb, lru, hend, *lru_p, reverse=False, B=B, S=S, tt=tt)
        fno_lat = _fno_lat(f, fno_wcs[l], m_a, f_c, B=B, S=S)
        if last:
            fno_ctx, nt = None, nlat
        else:
            na = _attn_ctx(q, k, v, na, B=B, S=S, CTX=CTX)
            fno_ctx, nt = _fno_ctx(f, fno_wcs[l], f_ctx, B=B, S=S, CTX=CTX), ntiles
        x3 = _ffn((x1,), mods, *ff2, ln_g4[:, 2], ln_b4[:, 2], j=2, l=l, alpha=alpha, nlat=nlat, ntiles=nt,
                  mix=(na, lru, fno_lat, fno_ctx, w_out_b, ln_g4[:, 1], ln_b4[:, 1]), **common)
        xs = (x3,)
    return xs[0].reshape(B, S, D)
```

```python
import functools
import math

import numpy as np
import jax
import jax.numpy as jnp
from jax import lax
from jax.experimental import pallas as pl
from jax.experimental.pallas import tpu as pltpu

F32 = jnp.float32
BF16 = jnp.bfloat16
HIGHEST = lax.Precision.HIGHEST

HEAD_DIM = 64
GRID_W = 64
NA_KH = 8
NA_KW = 16
LRU_HEADS = 4
LRU_CONV = 4
LRU_C = 8.0
FNO_GROUPS = 4
N_MOD = 9
MACARON = 0.5
LN_EPS = 1e-5
MASK_BIAS = -1e30
LOG2E = math.log2(math.e)
SUBLANES = 8
FNO_N1 = 64
VMEM_LIMIT = 56 * 1024 * 1024


def _cparams(sem):
    return pltpu.CompilerParams(dimension_semantics=sem, vmem_limit_bytes=VMEM_LIMIT)


def _layer_norm(z, g, b):
    mu = jnp.mean(z, axis=-1, keepdims=True)
    zc = z - mu
    var = jnp.mean(zc * zc, axis=-1, keepdims=True)
    return zc * lax.rsqrt(var + LN_EPS) * g + b


def _ada_kernel(c_ref, w_ref, b_ref, o_ref):
    c = c_ref[...]
    a = c * jax.nn.sigmoid(c)
    o_ref[...] = jnp.dot(a, w_ref[...], preferred_element_type=F32, precision=HIGHEST) + b_ref[...]


def _ada_params(cc, w_ada, b_ada):
    L, D, ND = w_ada.shape
    tn = D
    out = pl.pallas_call(
        _ada_kernel,
        grid=(L, ND // tn),
        in_specs=[pl.BlockSpec((16, D), lambda l, n: (0, 0)),
                  pl.BlockSpec((None, D, tn), lambda l, n: (l, 0, n)),
                  pl.BlockSpec((None, 1, tn), lambda l, n: (l, 0, n))],
        out_specs=pl.BlockSpec((None, 16, tn), lambda l, n: (l, 0, n)),
        out_shape=jax.ShapeDtypeStruct((L, 16, ND), F32),
        compiler_params=_cparams(("parallel", "parallel")),
        name="ada",
    )(cc, w_ada, b_ada.reshape(L, 1, ND))
    return out.reshape(L, 16, N_MOD, D)


def _ffn_kernel(*refs, j, tf, alpha, nlat, mode):
    m = refs[2 if mode == "two_x" else 1][...]
    if mode == "two_x":
        xl_ref, xc_ref, _, wg_ref, wu_ref, wd_ref, g_ref, b_ref, o_ref, h_ref, a_ref = refs
        x = jnp.where(pl.program_id(0) < nlat, xl_ref[...], xc_ref[...])
    elif mode == "plain":
        x_ref, _, wg_ref, wu_ref, wd_ref, g_ref, b_ref, o_ref, h_ref, a_ref = refs
        x = x_ref[...]
    else:
        if mode == "mix_ctx":
            (x_ref, _, na_ref, lru_ref, fl_ref, fc_ref, wo_ref, g1_ref, b1_ref,
             wg_ref, wu_ref, wd_ref, g_ref, b_ref, o_ref, h_ref, a_ref) = refs
            fno = jnp.where(pl.program_id(0) < nlat, fl_ref[...], fc_ref[...])
        else:
            (x_ref, _, na_ref, lru_ref, fl_ref, wo_ref, g1_ref, b1_ref,
             wg_ref, wu_ref, wd_ref, g_ref, b_ref, o_ref, h_ref, a_ref) = refs
            fno = fl_ref[...]
        na_w, lru_w = na_ref.shape[1], lru_ref.shape[1]
        y = jnp.dot(na_ref[...], wo_ref[0:na_w, :], preferred_element_type=F32)
        y = y + jnp.dot(lru_ref[...], wo_ref[na_w:na_w + lru_w, :], preferred_element_type=F32)
        y = y + jnp.dot(fno, wo_ref[na_w + lru_w:, :], preferred_element_type=F32)
        o_ref[...] = _layer_norm(alpha * x_ref[...] + m[5:6] * y, g1_ref[...], b1_ref[...])
        x = o_ref[...]
    shift, scale, gate = m[3 * j:3 * j + 1], m[3 * j + 1:3 * j + 2], m[3 * j + 2:3 * j + 3]
    h_ref[...] = (x * (1.0 + scale) + shift).astype(BF16)
    for c in range(wg_ref.shape[1] // tf):
        cols = slice(c * tf, (c + 1) * tf)
        h = h_ref[...]
        g = jnp.dot(h, wg_ref[:, cols], preferred_element_type=F32)
        u = jnp.dot(h, wu_ref[:, cols], preferred_element_type=F32)
        a_ref[:, cols] = (g * jax.nn.sigmoid(g) * u).astype(BF16)
    y = jnp.dot(a_ref[...], wd_ref[...], preferred_element_type=F32)
    if mode.startswith("mix"):
        x = o_ref[...]
    z = alpha * x + (MACARON * gate) * y
    o_ref[...] = _layer_norm(z, g_ref[...], b_ref[...])


def _ffn(xs, mods_l, wg, wu, wd, ln_g, ln_b, *, j, l, alpha, tm, tiles_per_batch, nlat, ntiles, nbatch, mix=None):
    two_x = len(xs) == 2
    mode = "two_x" if two_x else "plain"
    D = xs[0].shape[1]
    Fd = wg.shape[2]
    tf = 256
    row_idx = lambda i: jnp.minimum(i // tiles_per_batch, nbatch)
    if two_x:
        nctx = ntiles - nlat
        x_specs = [pl.BlockSpec((tm, D), lambda i: (jnp.minimum(i, nlat - 1), 0)),
                   pl.BlockSpec((tm, D), lambda i: (jnp.clip(i - nlat, 0, nctx - 1), 0))]
    else:
        x_specs = [pl.BlockSpec((tm, D), lambda i: (i, 0))]
    const = pl.Buffered(1)
    mix_specs, mix_args = [], []
    if mix is not None:
        na, lru, fno_l, fno_c, w_out, g1, b1 = mix
        mode = "mix" if fno_c is None else "mix_ctx"
        tile = lambda a: pl.BlockSpec((tm, a.shape[1]), lambda i: (i, 0))
        mix_specs = [tile(na), tile(lru), pl.BlockSpec((tm, fno_l.shape[1]), lambda i: (jnp.minimum(i, nlat - 1), 0))]
        mix_args = [na, lru, fno_l]
        if fno_c is not None:
            mix_specs.append(pl.BlockSpec((tm, fno_c.shape[1]), lambda i: (jnp.clip(i - nlat, 0, ntiles - nlat - 1), 0)))
            mix_args.append(fno_c)
        mix_specs += [pl.BlockSpec((None, D, D), lambda i: (l, 0, 0), pipeline_mode=const),
                      pl.BlockSpec((None, 1, D), lambda i: (l, 0, 0)),
                      pl.BlockSpec((None, 1, D), lambda i: (l, 0, 0))]
        mix_args += [w_out, g1, b1]
    return pl.pallas_call(
        functools.partial(_ffn_kernel, j=j, tf=tf, alpha=alpha, nlat=nlat, mode=mode),
        grid=(ntiles,),
        in_specs=x_specs + [
            pl.BlockSpec((None, None, N_MOD, D), lambda i: (l, row_idx(i), 0, 0))] + mix_specs + [
            pl.BlockSpec((None, D, Fd), lambda i: (l, 0, 0), pipeline_mode=const),
            pl.BlockSpec((None, D, Fd), lambda i: (l, 0, 0), pipeline_mode=const),
            pl.BlockSpec((None, Fd, D), lambda i: (l, 0, 0), pipeline_mode=const),
            pl.BlockSpec((None, 1, D), lambda i: (l, 0, 0)),
            pl.BlockSpec((None, 1, D), lambda i: (l, 0, 0))],
        out_specs=pl.BlockSpec((tm, D), lambda i: (i, 0)),
        out_shape=jax.ShapeDtypeStruct((ntiles * tm, D), F32),
        scratch_shapes=[pltpu.VMEM((tm, D), BF16), pltpu.VMEM((tm, Fd), BF16)],
        compiler_params=_cparams(("parallel",)),
        name=f"ffn{j}",
    )(*xs, mods_l, *mix_args, wg, wu, wd, ln_g, ln_b)


def _inproj_kernel(x_ref, xp_ref, xn_ref, m_ref, w_ref, cw_ref, wg_ref, vec_ref,
                   q_ref, k_ref, v_ref, xr_ref, gg_ref, f_ref, af_ref, bf_ref, ab_ref, bb_ref,
                   h_ref, xl_ref, gates_ref, *, na_w, lru_w, tiles_per_batch):
    m = m_ref[...]
    modulate = lambda xs: (xs * (1.0 + m[4:5]) + m[3:4]).astype(BF16)
    h_ref[...] = modulate(x_ref[...])

    def proj(lo, width, hs=None):
        return jnp.dot(h_ref[...] if hs is None else hs, w_ref[:, lo:lo + width], preferred_element_type=F32)

    xr_lo = 3 * na_w
    xr = proj(xr_lo, lru_w)
    xr_ref[...] = xr
    pos = pl.program_id(0) % tiles_per_batch
    xrp = jnp.where(pos == 0, 0.0, proj(xr_lo, lru_w, modulate(xp_ref[...])))
    xrn = jnp.where(pos == tiles_per_batch - 1, 0.0, proj(xr_lo, lru_w, modulate(xn_ref[...])))
    xl_ref[...] = _conv4(xrp, xr, xrn, cw_ref[...])
    gates_ref[...] = jnp.dot(xl_ref[...].astype(BF16), wg_ref[...], preferred_element_type=F32)
    q_ref[...] = (proj(0, na_w) * (HEAD_DIM ** -0.5 * LOG2E)).astype(BF16)
    a, b = _lru_coeffs(xl_ref[...], gates_ref[:, 0:2 * lru_w], vec_ref[0])
    af_ref[...] = a
    bf_ref[...] = b
    k_ref[...] = proj(na_w, na_w).astype(BF16)
    a, b = _lru_coeffs(xl_ref[...], gates_ref[:, 2 * lru_w:4 * lru_w], vec_ref[1])
    ab_ref[...] = a
    bb_ref[...] = b
    v_ref[...] = proj(2 * na_w, na_w).astype(BF16)
    gg_ref[...] = jax.nn.gelu(proj(3 * na_w + lru_w, lru_w)).astype(BF16)
    f_ref[...] = proj(3 * na_w + 2 * lru_w, w_ref.shape[1] - 3 * na_w - 2 * lru_w)


def _inproj(x_all, mods_l, w_in, lru_cw, lru_gate_w, lru_vec, *, l, tm, tiles_per_batch, nbatch, na_w, lru_w):
    T, D = x_all.shape
    cols = w_in.shape[2]
    fno_w = cols - 3 * na_w - 2 * lru_w
    row_idx = lambda i: jnp.minimum(i // tiles_per_batch, nbatch)
    nblk8 = T // SUBLANES
    widths = ([(na_w, BF16)] * 3 + [(lru_w, F32), (lru_w, BF16), (fno_w, F32)] + [(lru_w, F32)] * 4)
    const = pl.Buffered(1)
    return pl.pallas_call(
        functools.partial(_inproj_kernel, na_w=na_w, lru_w=lru_w, tiles_per_batch=tiles_per_batch),
        grid=(T // tm,),
        in_specs=[pl.BlockSpec((tm, D), lambda i: (i, 0)),
                  pl.BlockSpec((SUBLANES, D), lambda i: (jnp.maximum(i * (tm // SUBLANES) - 1, 0), 0)),
                  pl.BlockSpec((SUBLANES, D), lambda i: (jnp.minimum((i + 1) * (tm // SUBLANES), nblk8 - 1), 0)),
                  pl.BlockSpec((None, None, N_MOD, D), lambda i: (l, row_idx(i), 0, 0)),
                  pl.BlockSpec((None, D, cols), lambda i: (l, 0, 0), pipeline_mode=const),
                  pl.BlockSpec((None, SUBLANES, lru_w), lambda i: (l, 0, 0)),
                  pl.BlockSpec((None, lru_w, 4 * lru_w), lambda i: (l, 0, 0), pipeline_mode=const),
                  pl.BlockSpec((None, 2, SUBLANES, lru_w), lambda i: (l, 0, 0, 0))],
        out_specs=[pl.BlockSpec((tm, w), lambda i: (i, 0)) for w, _ in widths],
        out_shape=[jax.ShapeDtypeStruct((T, w), dt) for w, dt in widths],
        scratch_shapes=[pltpu.VMEM((tm, D), BF16), pltpu.VMEM((tm, lru_w), F32), pltpu.VMEM((tm, 4 * lru_w), F32)],
        compiler_params=_cparams(("parallel",)),
        name="inproj",
    )(x_all, x_all, x_all, mods_l, w_in, lru_cw, lru_gate_w, lru_vec)


def _na_bias_tables(rpb):
    L, H = rpb.shape[:2]
    W, ndr, ndc = GRID_W, 2 * NA_KH - 1, 2 * NA_KW - 1
    c = np.arange(W)[:, None]
    j = np.arange(W)[None, :]
    col0 = np.clip(c - NA_KW // 2, 0, W - NA_KW)
    valid = (j >= col0) & (j < col0 + NA_KW)
    entry = np.where(valid, np.clip(j - c + (NA_KW - 1), 0, ndc - 1), ndc)
    onehot = (np.arange(ndc + 1)[:, None, None] == entry[None]).astype(np.float32)
    ext = jnp.concatenate([rpb * LOG2E, jnp.full((L, H, ndr, 1), MASK_BIAS, F32)], axis=-1)
    bt = jnp.einsum('lhde,ecj->lhcdj', ext, onehot, precision=HIGHEST).reshape(L, H, W, ndr * W)
    wb = jnp.stack([bt[..., (NA_KH - 1 - case) * W:(2 * NA_KH - 1 - case) * W] for case in range(NA_KH)], axis=2)
    return wb.reshape(L, H // 2, 2, NA_KH, W, NA_KH * W)


def _masked_heads(q):
    lane = lax.broadcasted_iota(jnp.int32, q.shape, 1)
    qf = q.astype(F32)
    return (jnp.where(lane < HEAD_DIM, qf, 0.0).astype(BF16),
            jnp.where(lane >= HEAD_DIM, qf, 0.0).astype(BF16))


def _qk(q, k):
    return lax.dot_general(q, k, (((1,), (1,)), ((), ())), preferred_element_type=F32)


def _attn_kernel(q_ref, k_ref, v_ref, kc_ref, vc_ref, wb_ref, o_ref, *, R, G, rows):
    jb = pl.program_id(2)
    kc = kc_ref[...]
    vc = vc_ref[...]
    W = GRID_W
    nwin = NA_KH * W
    lane = lax.broadcasted_iota(jnp.int32, (W, 2 * HEAD_DIM), 1)

    def group(g, carry):
        qoff = pl.multiple_of(g * (R * W), R * W)
        qcat = []
        for rr in range(R):
            qe, qo = _masked_heads(q_ref[pl.ds(qoff + rr * W, W), :])
            qcat += [qe, qo]
        s_ctx_all = _qk(jnp.concatenate(qcat, axis=0), kc)
        o_loc, p_ctx, dens = [], [], []
        for rr in range(R):
            r = (jb * G + g) * R + rr
            rs = jnp.clip(r - NA_KH // 2, 0, rows - NA_KH)
            koff = pl.multiple_of(rs * W, W)
            kw = k_ref[pl.ds(koff, nwin), :]
            vw = v_ref[pl.ds(koff, nwin), :]
            qc = jnp.concatenate(qcat[2 * rr:2 * rr + 2], axis=0)
            s_loc = _qk(qc, kw) + jnp.concatenate([wb_ref[0, r - rs], wb_ref[1, r - rs]], axis=0)
            s_ctx = s_ctx_all[rr * 2 * W:(rr + 1) * 2 * W]
            m = jnp.maximum(jnp.max(s_loc, axis=-1, keepdims=True), jnp.max(s_ctx, axis=-1, keepdims=True))
            p_loc = jnp.exp2(s_loc - m)
            pc = jnp.exp2(s_ctx - m)
            dens.append(jnp.sum(p_loc, axis=-1, keepdims=True) + jnp.sum(pc, axis=-1, keepdims=True))
            o_loc.append(jnp.dot(p_loc.astype(BF16), vw, preferred_element_type=F32))
            p_ctx.append(pc.astype(BF16))
        o_ctx_all = jnp.dot(jnp.concatenate(p_ctx, axis=0), vc, preferred_element_type=F32)
        outs = []
        for rr in range(R):
            o = (o_loc[rr] + o_ctx_all[rr * 2 * W:(rr + 1) * 2 * W]) / dens[rr]
            outs.append(jnp.where(lane < HEAD_DIM, o[:W], o[W:]).astype(BF16))
        o_ref[pl.ds(qoff, R * W), :] = jnp.concatenate(outs, axis=0)
        return carry

    lax.fori_loop(0, G, group, 0, unroll=True)


def _attn_lat(q, k, v, wb, *, l, B, S, CTX):
    T, na_w = q.shape
    npair = na_w // (2 * HEAD_DIM)
    rows = S // GRID_W
    R = 8
    G = 8 if rows % (8 * R) == 0 else 1
    nrb = rows // (R * G)
    pw = 2 * HEAD_DIM
    ctx0 = (B * S) // CTX
    qblk = R * G * GRID_W
    return pl.pallas_call(
        functools.partial(_attn_kernel, R=R, G=G, rows=rows),
        grid=(B, npair, nrb),
        in_specs=[pl.BlockSpec((qblk, pw), lambda b, p, j: (b * nrb + j, p)),
                  pl.BlockSpec((S, pw), lambda b, p, j: (b, p)),
                  pl.BlockSpec((S, pw), lambda b, p, j: (b, p)),
                  pl.BlockSpec((CTX, pw), lambda b, p, j: (ctx0 + b, p)),
                  pl.BlockSpec((CTX, pw), lambda b, p, j: (ctx0 + b, p)),
                  pl.BlockSpec((None, None, 2, NA_KH, GRID_W, NA_KH * GRID_W), lambda b, p, j: (l, p, 0, 0, 0, 0))],
        out_specs=pl.BlockSpec((qblk, pw), lambda b, p, j: (b * nrb + j, p)),
        out_shape=jax.ShapeDtypeStruct((T, na_w), BF16),
        compiler_params=_cparams(("parallel", "parallel", "parallel")),
        name="attn_lat",
    )(q, k, v, k, v, wb)


def _attn_ctx_kernel(q_ref, k_ref, v_ref, na_hbm_ref, o_ref):
    del na_hbm_ref
    k = k_ref[...]
    v = v_ref[...]
    lane = lax.broadcasted_iota(jnp.int32, o_ref.shape, 1)
    outs = []
    for qm in _masked_heads(q_ref[...]):
        s = _qk(qm, k)
        p = jnp.exp2(s - jnp.max(s, axis=-1, keepdims=True))
        den = jnp.sum(p, axis=-1, keepdims=True)
        outs.append(jnp.dot(p.astype(BF16), v, preferred_element_type=F32) / den)
    o_ref[...] = jnp.where(lane < HEAD_DIM, outs[0], outs[1]).astype(BF16)


def _attn_ctx(q, k, v, na, *, B, S, CTX):
    T, na_w = q.shape
    pw = 2 * HEAD_DIM
    ctx0 = (B * S) // CTX
    spec = pl.BlockSpec((CTX, pw), lambda b, p: (ctx0 + b, p))
    return pl.pallas_call(
        _attn_ctx_kernel,
        grid=(B, na_w // pw),
        in_specs=[spec, spec, spec, pl.BlockSpec(memory_space=pl.ANY)],
        out_specs=spec,
        out_shape=jax.ShapeDtypeStruct((T, na_w), BF16),
        input_output_aliases={3: 0},
        compiler_params=_cparams(("parallel", "parallel")),
        name="attn_ctx",
    )(q, k, v, na)


def _conv4(x_prev, x_main, x_next, cw):
    n = x_main.shape[0]
    xe = jnp.concatenate([x_prev, x_main, x_next], axis=0)
    ne = n + 2 * SUBLANES
    y = cw[2:3] * x_main
    y = y + cw[0:1] * pltpu.roll(xe, 2, 0)[SUBLANES:SUBLANES + n]
    y = y + cw[1:2] * pltpu.roll(xe, 1, 0)[SUBLANES:SUBLANES + n]
    y = y + cw[3:4] * pltpu.roll(xe, ne - 1, 0)[SUBLANES:SUBLANES + n]
    return y + cw[4:5]


def _softplus(x):
    return jnp.maximum(x, 0.0) + jnp.log1p(jnp.exp(-jnp.abs(x)))


def _lru_coeffs(xl, gates, vec):
    C = xl.shape[1]
    r = jax.nn.sigmoid(gates[:, :C] + vec[0:1])
    i = jax.nn.sigmoid(gates[:, C:] + vec[1:2])
    log_a = (-LRU_C * r) * _softplus(-vec[2:3])
    a = jnp.exp(log_a)
    b = jnp.sqrt(-jnp.tanh(log_a) * (1.0 + a * a)) * (i * xl)
    return a, b


def _group_scan(a, b, reverse):
    n, C = a.shape
    a = a.reshape(n // SUBLANES, SUBLANES, C)
    b = b.reshape(n // SUBLANES, SUBLANES, C)
    row = lax.broadcasted_iota(jnp.int32, a.shape, 1)
    for k in (1, 2, 4):
        shift = SUBLANES - k if reverse else k
        mask = (row < SUBLANES - k) if reverse else (row >= k)
        a_s = pltpu.roll(a, shift, 1)
        b_s = pltpu.roll(b, shift, 1)
        b = jnp.where(mask, a * b_s + b, b)
        a = jnp.where(mask, a * a_s, a)
    return a.reshape(n, C), b.reshape(n, C)


def _chunk_scan(a, b, h_in, reverse, acum_ref, bcum_ref, out_ref):
    n, C = a.shape
    ac, bc = _group_scan(a, b, reverse)
    acum_ref[...] = ac
    bcum_ref[...] = bc
    ng = n // SUBLANES

    def body(g, hb):
        idx = (ng - 1 - g) if reverse else g
        sl = pl.ds(pl.multiple_of(idx * SUBLANES, SUBLANES), SUBLANES)
        h = acum_ref[sl, :] * hb + bcum_ref[sl, :]
        out_ref[sl, :] = h
        edge = h[0:1] if reverse else h[SUBLANES - 1:SUBLANES]
        return jnp.broadcast_to(edge, (SUBLANES, C))

    return lax.fori_loop(0, ng, body, h_in, unroll=8)


def _lru_ctx_kernel(xr_ref, gg_ref, cw_ref, w_ref, vec_ref, o_ref, hend_ref, acum_ref, bcum_ref, hf_ref, hb_ref):
    xr = xr_ref[...]
    zeros8 = jnp.zeros((SUBLANES, xr.shape[1]), F32)
    xl = _conv4(zeros8, xr, zeros8, cw_ref[...])
    gates = jnp.dot(xl.astype(BF16), w_ref[...], preferred_element_type=F32)
    half = gates.shape[1] // 2
    a, b = _lru_coeffs(xl, gates[:, :half], vec_ref[0])
    hf_end = _chunk_scan(a, b, zeros8, False, acum_ref, bcum_ref, hf_ref)
    a, b = _lru_coeffs(xl, gates[:, half:], vec_ref[1])
    hb_end = _chunk_scan(a, b, zeros8, True, acum_ref, bcum_ref, hb_ref)
    o_ref[...] = ((hf_ref[...] + hb_ref[...]) * gg_ref[...]).astype(BF16)
    row = lax.broadcasted_iota(jnp.int32, hf_end.shape, 0)
    hend_ref[...] = jnp.where(row == 0, hf_end, hb_end)


def _lru_ctx(xr, gg, cw, w, vec, *, l, B, S, CTX):
    C = xr.shape[1]
    ctx0 = (B * S) // CTX
    spec = pl.BlockSpec((CTX, C), lambda b: (ctx0 + b, 0))
    return pl.pallas_call(
        _lru_ctx_kernel,
        grid=(B,),
        in_specs=[spec, spec,
                  pl.BlockSpec((None, SUBLANES, C), lambda b: (l, 0, 0)),
                  pl.BlockSpec((None, C, 4 * C), lambda b: (l, 0, 0)),
                  pl.BlockSpec((None, 2, SUBLANES, C), lambda b: (l, 0, 0, 0))],
        out_specs=[spec,
                   pl.BlockSpec((None, SUBLANES, C), lambda b: (b, 0, 0))],
        out_shape=[jax.ShapeDtypeStruct((xr.shape[0], C), BF16),
                   jax.ShapeDtypeStruct((B, SUBLANES, C), F32)],
        scratch_shapes=[pltpu.VMEM((CTX, C), F32)] * 4,
        compiler_params=_cparams(("parallel",)),
        name="lru_ctx",
    )(xr, gg, cw, w, vec)


def _lru_lat_kernel(*refs, reverse):
    if reverse:
        a_ref, b_ref, hend_ref, o_ref, carry_ref, acum_ref, bcum_ref = refs
    else:
        a_ref, b_ref, hend_ref, gg_ref, hb_ref, _lru_hbm_ref, o_ref, carry_ref, acum_ref, bcum_ref, hf_ref = refs
    C = a_ref.shape[1]

    @pl.when(pl.program_id(1) == 0)
    def _():
        row = hend_ref[1:2, :] if reverse else hend_ref[0:1, :]
        carry_ref[...] = jnp.broadcast_to(row, (SUBLANES, C))

    if reverse:
        carry_ref[...] = _chunk_scan(a_ref[...], b_ref[...], carry_ref[...], True, acum_ref, bcum_ref, o_ref)
    else:
        carry_ref[...] = _chunk_scan(a_ref[...], b_ref[...], carry_ref[...], False, acum_ref, bcum_ref, hf_ref)
        o_ref[...] = ((hf_ref[...] + hb_ref[...]) * gg_ref[...]).astype(BF16)


def _lru_lat(a, b, gg, hb, lru, hend, *, reverse, B, S, tt):
    C = a.shape[1]
    nch = S // tt
    chunk = (lambda i: nch - 1 - i) if reverse else (lambda i: i)
    main = pl.BlockSpec((tt, C), lambda b_, i: (b_ * nch + chunk(i), 0))
    in_specs = [main, main, pl.BlockSpec((None, SUBLANES, C), lambda b_, i: (b_, 0, 0))]
    args = [a, b, hend]
    scratch = [pltpu.VMEM((SUBLANES, C), F32), pltpu.VMEM((tt, C), F32), pltpu.VMEM((tt, C), F32)]
    if reverse:
        out_shape, aliases = jax.ShapeDtypeStruct((B * S, C), F32), {}
    else:
        in_specs += [main, main, pl.BlockSpec(memory_space=pl.ANY)]
        args += [gg, hb, lru]
        scratch.append(pltpu.VMEM((tt, C), F32))
        out_shape, aliases = jax.ShapeDtypeStruct(lru.shape, BF16), {len(args) - 1: 0}
    return pl.pallas_call(
        functools.partial(_lru_lat_kernel, reverse=reverse),
        grid=(B, nch),
        in_specs=in_specs,
        out_specs=main,
        out_shape=out_shape,
        input_output_aliases=aliases,
        scratch_shapes=scratch,
        compiler_params=_cparams(("parallel", "arbitrary")),
        name="lru_bwd" if reverse else "lru_fwd",
    )(*args)


def _fno_w_kernel(fw_ref, cs_ref, o_ref):
    fw = fw_ref[...]
    o_ref[0] = jnp.dot(cs_ref[0], fw, preferred_element_type=F32, precision=HIGHEST).astype(BF16)
    o_ref[1] = jnp.dot(cs_ref[1], fw, preferred_element_type=F32, precision=HIGHEST).astype(BF16)


def _fno_weights(fw_bd, cs_bd):
    L, C, _ = fw_bd.shape
    return pl.pallas_call(
        _fno_w_kernel,
        grid=(L,),
        in_specs=[pl.BlockSpec((None, C, C), lambda l: (l, 0, 0)),
                  pl.BlockSpec((2, C, C), lambda l: (0, 0, 0))],
        out_specs=pl.BlockSpec((None, 2, C, C), lambda l: (l, 0, 0, 0)),
        out_shape=jax.ShapeDtypeStruct((L, 2, C, C), BF16),
        compiler_params=_cparams(("parallel",)),
        name="fno_w",
    )(fw_bd, cs_bd)


def _fno_a_kernel(x_ref, w_ref, m_ref, o_ref, *, nb):
    n1 = x_ref.shape[0] // nb
    _, s, C = x_ref.shape
    uvs = []
    for bb in range(nb):
        x = x_ref[bb * n1:(bb + 1) * n1].reshape(n1 * s, C).astype(BF16)
        u = jnp.dot(x, w_ref[0], preferred_element_type=F32).astype(BF16)
        v = jnp.dot(x, w_ref[1], preferred_element_type=F32).astype(BF16)
        uvs.append(jnp.concatenate([u, v], axis=0))
    uv = jnp.concatenate(uvs, axis=1)
    a = jnp.dot(m_ref[...], uv, preferred_element_type=F32).astype(BF16)
    rows = 2 * s
    for bb in range(nb):
        for k1 in range(n1):
            o_ref[bb, :, k1 * C:(k1 + 1) * C] = a[k1 * rows:(k1 + 1) * rows, bb * C:(bb + 1) * C]


def _fno_c_kernel(a_ref, f_ref, o_ref, *, scale):
    res = jnp.dot(f_ref[...], a_ref[...], preferred_element_type=F32) * scale
    o_ref[...] = res.reshape(o_ref.shape).astype(BF16)


def _fno_lat(f, w_l, m_a, f_c, *, B, S):
    T, C = f.shape
    n1 = FNO_N1
    n2 = S // n1
    nj = n2 // SUBLANES
    nb = 4 if B % 4 == 0 else 1
    f4 = f.reshape(T // n2, nj, SUBLANES, C)
    rows_a = 2 * SUBLANES * n1
    a2 = pl.pallas_call(
        functools.partial(_fno_a_kernel, nb=nb),
        grid=(nj, B // nb),
        in_specs=[pl.BlockSpec((nb * n1, None, SUBLANES, C), lambda j, b: (b, j, 0, 0)),
                  pl.BlockSpec((2, C, C), lambda j, b: (0, 0, 0)),
                  pl.BlockSpec((None, rows_a, rows_a), lambda j, b: (j, 0, 0))],
        out_specs=pl.BlockSpec((nb, 2 * SUBLANES, n1 * C), lambda j, b: (b, j, 0)),
        out_shape=jax.ShapeDtypeStruct((B, 2 * n2, n1 * C), BF16),
        compiler_params=_cparams(("parallel", "parallel")),
        name="fno_a",
    )(f4, w_l, m_a)
    ncol = n1 * C
    tc = min(ncol, 4096)
    scale = 1.0 / math.sqrt(S * (C // FNO_GROUPS))
    out = pl.pallas_call(
        functools.partial(_fno_c_kernel, scale=scale),
        grid=(B, ncol // tc),
        in_specs=[pl.BlockSpec((None, 2 * n2, tc), lambda b, t: (b, 0, t)),
                  pl.BlockSpec((n2, 2 * n2), lambda b, t: (0, 0))],
        out_specs=pl.BlockSpec((None, n2, tc // C, C), lambda b, t: (b, 0, t, 0)),
        out_shape=jax.ShapeDtypeStruct((B, n2, n1, C), BF16),
        compiler_params=_cparams(("parallel", "parallel")),
        name="fno_c",
    )(a2, f_c)
    return out.reshape(B * S, C)


def _fno_ctx_kernel(x_ref, w_ref, f_ref, o_ref, *, scale):
    x = x_ref[...].astype(BF16)
    u = jnp.dot(x, w_ref[0], preferred_element_type=F32).astype(BF16)
    v = jnp.dot(x, w_ref[1], preferred_element_type=F32).astype(BF16)
    uv = jnp.concatenate([u, v], axis=0)
    o_ref[...] = (jnp.dot(f_ref[...], uv, preferred_element_type=F32) * scale).astype(BF16)


def _fno_ctx(f, w_l, f_ctx, *, B, S, CTX):
    C = f.shape[1]
    ctx0 = (B * S) // CTX
    scale = 1.0 / math.sqrt(CTX * (C // FNO_GROUPS))
    return pl.pallas_call(
        functools.partial(_fno_ctx_kernel, scale=scale),
        grid=(B,),
        in_specs=[pl.BlockSpec((CTX, C), lambda b: (ctx0 + b, 0)),
                  pl.BlockSpec((2, C, C), lambda b: (0, 0, 0)),
                  pl.BlockSpec((CTX, 2 * CTX), lambda b: (0, 0))],
        out_specs=pl.BlockSpec((CTX, C), lambda b: (b, 0)),
        out_shape=jax.ShapeDtypeStruct((B * CTX, C), BF16),
        compiler_params=_cparams(("parallel",)),
        name="fno_ctx",
    )(f, w_l, f_ctx)


def _dft_constants(S, CTX, C):
    gd = C // FNO_GROUPS
    n1, n2 = FNO_N1, S // FNO_N1
    nj = n2 // SUBLANES
    idx = jnp.arange(gd, dtype=jnp.int32)
    ang = (2.0 * math.pi / gd) * ((idx[:, None] * idx[None, :]) % gd).astype(F32)
    eye_g = jnp.eye(FNO_GROUPS, dtype=F32)
    cs_bd = jnp.stack([jnp.kron(eye_g, jnp.cos(ang)), jnp.kron(eye_g, jnp.sin(ang))])
    k1 = jnp.arange(n1, dtype=jnp.int32)[:, None, None]
    nn1 = jnp.arange(n1, dtype=jnp.int32)[None, :, None]
    nn2 = jnp.arange(n2, dtype=jnp.int32)[None, None, :]
    ph = (2.0 * math.pi / S) * ((k1 * (n2 * nn1 + nn2)) % S).astype(F32)
    cph = jnp.cos(ph).reshape(n1, n1, nj, SUBLANES).transpose(2, 0, 3, 1)
    sph = jnp.sin(ph).reshape(n1, n1, nj, SUBLANES).transpose(2, 0, 3, 1)
    t4 = jnp.stack([jnp.stack([cph, -sph], axis=3), jnp.stack([-sph, -cph], axis=3)], axis=2)
    nr, nq = 2 * SUBLANES * n1, 2 * n1
    expand = (jnp.arange(nq * SUBLANES, dtype=jnp.int32)[None, :] // SUBLANES
              == jnp.arange(nq, dtype=jnp.int32)[:, None]).astype(BF16)
    m_a = jnp.einsum('jrq,qc->jrc', t4.reshape(nj, nr, nq).astype(BF16), expand, preferred_element_type=F32)
    same_s = (jnp.arange(nr, dtype=jnp.int32)[:, None] % SUBLANES
              == jnp.arange(nq * SUBLANES, dtype=jnp.int32)[None, :] % SUBLANES)
    m_a = jnp.where(same_s[None], m_a, 0.0).astype(BF16)
    i2 = jnp.arange(n2, dtype=jnp.int32)
    a2 = (2.0 * math.pi / n2) * ((i2[:, None] * i2[None, :]) % n2).astype(F32)
    f_c = jnp.stack([jnp.cos(a2).reshape(n2, nj, SUBLANES), jnp.sin(a2).reshape(n2, nj, SUBLANES)], axis=2)
    f_c = f_c.reshape(n2, 2 * n2).astype(BF16)
    ic = jnp.arange(CTX, dtype=jnp.int32)
    ac = (2.0 * math.pi / CTX) * ((ic[:, None] * ic[None, :]) % CTX).astype(F32)
    f_ctx = jnp.concatenate([jnp.cos(ac), -jnp.sin(ac)], axis=1).astype(BF16)
    return cs_bd, m_a, f_c, f_ctx


def _block_diag(w):
    G, n = w.shape[-3], w.shape[-1]
    eye = jnp.eye(G, dtype=w.dtype)
    out = jnp.einsum('...gij,gh->...gihj', w, eye)
    return out.reshape(w.shape[:-3] + (G * n, G * n))


def kernel(x, c, ctx, c_ctx, w_ada, b_ada, ln_g, ln_b, ff1_gate, ff1_up, ff1_down, ff2_gate, ff2_up, ff2_down,
           w_in, w_out, na_rpb, lru_conv_w, lru_conv_b, lru_wa, lru_ba, lru_wx, lru_bx, lru_lambda, fno_w):
    B, S, D = x.shape
    CTX = ctx.shape[1]
    L = w_ada.shape[0]
    na_w = na_rpb.shape[1] * HEAD_DIM
    lru_w = lru_conv_w.shape[2]
    alpha = float((2 * L) ** 0.25)
    tm = min(1024, B * CTX)
    tiles_per_batch = S // tm
    nlat = (B * S) // tm
    ntiles = nlat + (B * CTX) // tm
    assert B < 16 and S % tm == 0 and (B * CTX) % tm == 0 and S % (GRID_W * NA_KH) == 0

    cc = jnp.zeros((16, D), F32).at[:B].set(c).at[B].set(c_ctx)
    mods = _ada_params(cc, w_ada, b_ada)

    bf = lambda w: w.astype(BF16)
    ff1 = (bf(ff1_gate), bf(ff1_up), bf(ff1_down))
    ff2 = (bf(ff2_gate), bf(ff2_up), bf(ff2_down))
    w_in_b, w_out_b = bf(w_in), bf(w_out)
    ln_g4 = ln_g[:, :, None, :]
    ln_b4 = ln_b[:, :, None, :]
    wb = _na_bias_tables(na_rpb)
    gate_w = jnp.concatenate([_block_diag(lru_wa), _block_diag(lru_wx)], axis=-1)
    lru_gate_w = bf(jnp.concatenate([gate_w[:, 0], gate_w[:, 1]], axis=-1))
    zrow = jnp.zeros((L, 2, SUBLANES - 3, lru_w), F32)
    lru_vec = jnp.concatenate([lru_ba[:, :, None], lru_bx[:, :, None], lru_lambda[:, :, None], zrow], axis=2)
    lru_cw = jnp.concatenate([lru_conv_w, lru_conv_b[:, None], jnp.zeros((L, SUBLANES - LRU_CONV - 1, lru_w), F32)], axis=1)
    cs_bd, m_a, f_c, f_ctx = _dft_constants(S, CTX, D - na_w - lru_w)
    fno_wcs = _fno_weights(_block_diag(fno_w), cs_bd)

    common = dict(tm=tm, tiles_per_batch=tiles_per_batch, nbatch=B)
    xs = (x.reshape(B * S, D), ctx.reshape(B * CTX, D))
    for l in range(L):
        last = l == L - 1
        x1 = _ffn(xs, mods, *ff1, ln_g4[:, 0], ln_b4[:, 0], j=0, l=l, alpha=alpha, nlat=nlat, ntiles=ntiles, **common)
        q, k, v, xr, gg, f, a_f, b_f, a_b, b_b = _inproj(x1, mods, w_in_b, lru_cw, lru_gate_w, lru_vec,
                                                        l=l, na_w=na_w, lru_w=lru_w, **common)
        na = _attn_lat(q, k, v, wb, l=l, B=B, S=S, CTX=CTX)
        lru, hend = _lru_ctx(xr, gg, lru_cw, lru_gate_w, lru_vec, l=l, B=B, S=S, CTX=CTX)
        tt = min(S, 2048)
        hb = _lru_lat(a_b, b_b, None, None, None, hend, reverse=True, B=B, S=S, tt=tt)
        lru = _lru_lat(a_f, b_f, gg, hb, lru, hend, reverse=False, B=B, S=S, tt=tt)
        fno_lat = _fno_lat(f, fno_wcs[l], m_a, f_c, B=B, S=S)
        if last:
            fno_ctx, nt = None, nlat
        else:
            na = _attn_ctx(q, k, v, na, B=B, S=S, CTX=CTX)
            fno_ctx, nt = _fno_ctx(f, fno_wcs[l], f_ctx, B=B, S=S, CTX=CTX), ntiles
        x3 = _ffn((x1,), mods, *ff2, ln_g4[:, 2], ln_b4[:, 2], j=2, l=l, alpha=alpha, nlat=nlat, ntiles=nt,
                  mix=(na, lru, fno_lat, fno_ctx, w_out_b, ln_g4[:, 1], ln_b4[:, 1]), **common)
        xs = (x3,)
    return xs[0].reshape(B, S, D)
```

```python
import functools
import math

import numpy as np
import jax
import jax.numpy as jnp
from jax import lax
from jax.experimental import pallas as pl
from jax.experimental.pallas import tpu as pltpu

F32 = jnp.float32
BF16 = jnp.bfloat16
HIGHEST = lax.Precision.HIGHEST

HEAD_DIM = 64
GRID_W = 64
NA_KH = 8
NA_KW = 16
LRU_HEADS = 4
LRU_CONV = 4
LRU_C = 8.0
FNO_GROUPS = 4
N_MOD = 9
MACARON = 0.5
LN_EPS = 1e-5
MASK_BIAS = -1e30
LOG2E = math.log2(math.e)
SUBLANES = 8
FNO_N1 = 64
VMEM_LIMIT = 56 * 1024 * 1024


def _cparams(sem):
    return pltpu.CompilerParams(dimension_semantics=sem, vmem_limit_bytes=VMEM_LIMIT)


def _layer_norm(z, g, b):
    mu = jnp.mean(z, axis=-1, keepdims=True)
    zc = z - mu
    var = jnp.mean(zc * zc, axis=-1, keepdims=True)
    return zc * lax.rsqrt(var + LN_EPS) * g + b


def _ada_kernel(c_ref, w_ref, b_ref, o_ref):
    c = c_ref[...]
    a = c * jax.nn.sigmoid(c)
    o_ref[...] = jnp.dot(a, w_ref[...], preferred_element_type=F32, precision=HIGHEST) + b_ref[...]


def _ada_params(cc, w_ada, b_ada):
    L, D, ND = w_ada.shape
    tn = D
    out = pl.pallas_call(
        _ada_kernel,
        grid=(L, ND // tn),
        in_specs=[pl.BlockSpec((16, D), lambda l, n: (0, 0)),
                  pl.BlockSpec((None, D, tn), lambda l, n: (l, 0, n)),
                  pl.BlockSpec((None, 1, tn), lambda l, n: (l, 0, n))],
        out_specs=pl.BlockSpec((None, 16, tn), lambda l, n: (l, 0, n)),
        out_shape=jax.ShapeDtypeStruct((L, 16, ND), F32),
        compiler_params=_cparams(("parallel", "parallel")),
        name="ada",
    )(cc, w_ada, b_ada.reshape(L, 1, ND))
    return out.reshape(L, 16, N_MOD, D)


def _ffn_kernel(*refs, j, tf, alpha, nlat, mode):
    m = refs[2 if mode == "two_x" else 1][...]
    if mode == "two_x":
        xl_ref, xc_ref, _, wg_ref, wu_ref, wd_ref, g_ref, b_ref, o_ref, h_ref, a_ref = refs
        x = jnp.where(pl.program_id(0) < nlat, xl_ref[...], xc_ref[...])
    elif mode == "plain":
        x_ref, _, wg_ref, wu_ref, wd_ref, g_ref, b_ref, o_ref, h_ref, a_ref = refs
        x = x_ref[...]
    else:
        if mode == "mix_ctx":
            (x_ref, _, na_ref, lru_ref, fl_ref, fc_ref, wo_ref, g1_ref, b1_ref,
             wg_ref, wu_ref, wd_ref, g_ref, b_ref, o_ref, h_ref, a_ref) = refs
            fno = jnp.where(pl.program_id(0) < nlat, fl_ref[...], fc_ref[...])
        else:
            (x_ref, _, na_ref, lru_ref, fl_ref, wo_ref, g1_ref, b1_ref,
             wg_ref, wu_ref, wd_ref, g_ref, b_ref, o_ref, h_ref, a_ref) = refs
            fno = fl_ref[...]
        na_w, lru_w = na_ref.shape[1], lru_ref.shape[1]
        y = jnp.dot(na_ref[...], wo_ref[0:na_w, :], preferred_element_type=F32)
        y = y + jnp.dot(lru_ref[...], wo_ref[na_w:na_w + lru_w, :], preferred_element_type=F32)
        y = y + jnp.dot(fno, wo_ref[na_w + lru_w:, :], preferred_element_type=F32)
        o_ref[...] = _layer_norm(alpha * x_ref[...] + m[5:6] * y, g1_ref[...], b1_ref[...])
        x = o_ref[...]
    shift, scale, gate = m[3 * j:3 * j + 1], m[3 * j + 1:3 * j + 2], m[3 * j + 2:3 * j + 3]
    h_ref[...] = (x * (1.0 + scale) + shift).astype(BF16)
    for c in range(wg_ref.shape[1] // tf):
        cols = slice(c * tf, (c + 1) * tf)
        h = h_ref[...]
        g = jnp.dot(h, wg_ref[:, cols], preferred_element_type=F32)
        u = jnp.dot(h, wu_ref[:, cols], preferred_element_type=F32)
        a_ref[:, cols] = (g * jax.nn.sigmoid(g) * u).astype(BF16)
    y = jnp.dot(a_ref[...], wd_ref[...], preferred_element_type=F32)
    if mode.startswith("mix"):
        x = o_ref[...]
    z = alpha * x + (MACARON * gate) * y
    o_ref[...] = _layer_norm(z, g_ref[...], b_ref[...])


def _ffn(xs, mods_l, wg, wu, wd, ln_g, ln_b, *, j, l, alpha, tm, tiles_per_batch, nlat, ntiles, nbatch, mix=None):
    two_x = len(xs) == 2
    mode = "two_x" if two_x else "plain"
    D = xs[0].shape[1]
    Fd = wg.shape[2]
    tf = 256
    row_idx = lambda i: jnp.minimum(i // tiles_per_batch, nbatch)
    if two_x:
        nctx = ntiles - nlat
        x_specs = [pl.BlockSpec((tm, D), lambda i: (jnp.minimum(i, nlat - 1), 0)),
                   pl.BlockSpec((tm, D), lambda i: (jnp.clip(i - nlat, 0, nctx - 1), 0))]
    else:
        x_specs = [pl.BlockSpec((tm, D), lambda i: (i, 0))]
    const = pl.Buffered(1)
    mix_specs, mix_args = [], []
    if mix is not None:
        na, lru, fno_l, fno_c, w_out, g1, b1 = mix
        mode = "mix" if fno_c is None else "mix_ctx"
        tile = lambda a: pl.BlockSpec((tm, a.shape[1]), lambda i: (i, 0))
        mix_specs = [tile(na), tile(lru), pl.BlockSpec((tm, fno_l.shape[1]), lambda i: (jnp.minimum(i, nlat - 1), 0))]
        mix_args = [na, lru, fno_l]
        if fno_c is not None:
            mix_specs.append(pl.BlockSpec((tm, fno_c.shape[1]), lambda i: (jnp.clip(i - nlat, 0, ntiles - nlat - 1), 0)))
            mix_args.append(fno_c)
        mix_specs += [pl.BlockSpec((None, D, D), lambda i: (l, 0, 0), pipeline_mode=const),
                      pl.BlockSpec((None, 1, D), lambda i: (l, 0, 0)),
                      pl.BlockSpec((None, 1, D), lambda i: (l, 0, 0))]
        mix_args += [w_out, g1, b1]
    return pl.pallas_call(
        functools.partial(_ffn_kernel, j=j, tf=tf, alpha=alpha, nlat=nlat, mode=mode),
        grid=(ntiles,),
        in_specs=x_specs + [
            pl.BlockSpec((None, None, N_MOD, D), lambda i: (l, row_idx(i), 0, 0))] + mix_specs + [
            pl.BlockSpec((None, D, Fd), lambda i: (l, 0, 0), pipeline_mode=const),
            pl.BlockSpec((None, D, Fd), lambda i: (l, 0, 0), pipeline_mode=const),
            pl.BlockSpec((None, Fd, D), lambda i: (l, 0, 0), pipeline_mode=const),
            pl.BlockSpec((None, 1, D), lambda i: (l, 0, 0)),
            pl.BlockSpec((None, 1, D), lambda i: (l, 0, 0))],
        out_specs=pl.BlockSpec((tm, D), lambda i: (i, 0)),
        out_shape=jax.ShapeDtypeStruct((ntiles * tm, D), F32),
        scratch_shapes=[pltpu.VMEM((tm, D), BF16), pltpu.VMEM((tm, Fd), BF16)],
        compiler_params=_cparams(("parallel",)),
        name=f"ffn{j}",
    )(*xs, mods_l, *mix_args, wg, wu, wd, ln_g, ln_b)


def _inproj_kernel(x_ref, xp_ref, xn_ref, m_ref, w_ref, cw_ref, wg_ref, vec_ref,
                   q_ref, k_ref, v_ref, xr_ref, gg_ref, f_ref, af_ref, bf_ref, ab_ref, bb_ref,
                   h_ref, xl_ref, gates_ref, *, na_w, lru_w, tiles_per_batch):
    m = m_ref[...]
    modulate = lambda xs: (xs * (1.0 + m[4:5]) + m[3:4]).astype(BF16)
    h_ref[...] = modulate(x_ref[...])

    def proj(lo, width, hs=None):
        return jnp.dot(h_ref[...] if hs is None else hs, w_ref[:, lo:lo + width], preferred_element_type=F32)

    xr_lo = 3 * na_w
    xr = proj(xr_lo, lru_w)
    xr_ref[...] = xr
    pos = pl.program_id(0) % tiles_per_batch
    xrp = jnp.where(pos == 0, 0.0, proj(xr_lo, lru_w, modulate(xp_ref[...])))
    xrn = jnp.where(pos == tiles_per_batch - 1, 0.0, proj(xr_lo, lru_w, modulate(xn_ref[...])))
    gr = proj(3 * na_w + lru_w, lru_w)
    q_ref[...] = (proj(0, na_w) * (HEAD_DIM ** -0.5 * LOG2E)).astype(BF16)
    xl_ref[...] = _conv4(xrp, xr, xrn, cw_ref[...])
    gg_ref[...] = jax.nn.gelu(gr).astype(BF16)
    xlb = xl_ref[...].astype(BF16)
    gates = jnp.dot(xlb, wg_ref[:, 0:2 * lru_w], preferred_element_type=F32)
    k_ref[...] = proj(na_w, na_w).astype(BF16)
    a, b = _lru_coeffs(xl_ref[...], gates, vec_ref[0])
    af_ref[...] = a
    bf_ref[...] = b
    gates = jnp.dot(xlb, wg_ref[:, 2 * lru_w:4 * lru_w], preferred_element_type=F32)
    v_ref[...] = proj(2 * na_w, na_w).astype(BF16)
    a, b = _lru_coeffs(xl_ref[...], gates, vec_ref[1])
    ab_ref[...] = a
    bb_ref[...] = b
    f_ref[...] = proj(3 * na_w + 2 * lru_w, w_ref.shape[1] - 3 * na_w - 2 * lru_w)


def _inproj(x_all, mods_l, w_in, lru_cw, lru_gate_w, lru_vec, *, l, tm, tiles_per_batch, nbatch, na_w, lru_w):
    T, D = x_all.shape
    cols = w_in.shape[2]
    fno_w = cols - 3 * na_w - 2 * lru_w
    row_idx = lambda i: jnp.minimum(i // tiles_per_batch, nbatch)
    nblk8 = T // SUBLANES
    widths = ([(na_w, BF16)] * 3 + [(lru_w, F32), (lru_w, BF16), (fno_w, F32)] + [(lru_w, F32)] * 4)
    const = pl.Buffered(1)
    return pl.pallas_call(
        functools.partial(_inproj_kernel, na_w=na_w, lru_w=lru_w, tiles_per_batch=tiles_per_batch),
        grid=(T // tm,),
        in_specs=[pl.BlockSpec((tm, D), lambda i: (i, 0)),
                  pl.BlockSpec((SUBLANES, D), lambda i: (jnp.maximum(i * (tm // SUBLANES) - 1, 0), 0)),
                  pl.BlockSpec((SUBLANES, D), lambda i: (jnp.minimum((i + 1) * (tm // SUBLANES), nblk8 - 1), 0)),
                  pl.BlockSpec((None, None, N_MOD, D), lambda i: (l, row_idx(i), 0, 0)),
                  pl.BlockSpec((None, D, cols), lambda i: (l, 0, 0), pipeline_mode=const),
                  pl.BlockSpec((None, SUBLANES, lru_w), lambda i: (l, 0, 0)),
                  pl.BlockSpec((None, lru_w, 4 * lru_w), lambda i: (l, 0, 0), pipeline_mode=const),
                  pl.BlockSpec((None, 2, SUBLANES, lru_w), lambda i: (l, 0, 0, 0))],
        out_specs=[pl.BlockSpec((tm, w), lambda i: (i, 0)) for w, _ in widths],
        out_shape=[jax.ShapeDtypeStruct((T, w), dt) for w, dt in widths],
        scratch_shapes=[pltpu.VMEM((tm, D), BF16), pltpu.VMEM((tm, lru_w), F32), pltpu.VMEM((tm, 4 * lru_w), F32)],
        compiler_params=_cparams(("parallel",)),
        name="inproj",
    )(x_all, x_all, x_all, mods_l, w_in, lru_cw, lru_gate_w, lru_vec)


def _na_bias_tables(rpb):
    L, H = rpb.shape[:2]
    W, ndr, ndc = GRID_W, 2 * NA_KH - 1, 2 * NA_KW - 1
    c = np.arange(W)[:, None]
    j = np.arange(W)[None, :]
    col0 = np.clip(c - NA_KW // 2, 0, W - NA_KW)
    valid = (j >= col0) & (j < col0 + NA_KW)
    entry = np.where(valid, np.clip(j - c + (NA_KW - 1), 0, ndc - 1), ndc)
    onehot = (np.arange(ndc + 1)[:, None, None] == entry[None]).astype(np.float32)
    ext = jnp.concatenate([rpb * LOG2E, jnp.full((L, H, ndr, 1), MASK_BIAS, F32)], axis=-1)
    bt = jnp.einsum('lhde,ecj->lhcdj', ext, onehot, precision=HIGHEST).reshape(L, H, W, ndr * W)
    wb = jnp.stack([bt[..., (NA_KH - 1 - case) * W:(2 * NA_KH - 1 - case) * W] for case in range(NA_KH)], axis=2)
    return wb.reshape(L, H // 2, 2, NA_KH, W, NA_KH * W)


def _masked_heads(q):
    lane = lax.broadcasted_iota(jnp.int32, q.shape, 1)
    qf = q.astype(F32)
    return (jnp.where(lane < HEAD_DIM, qf, 0.0).astype(BF16),
            jnp.where(lane >= HEAD_DIM, qf, 0.0).astype(BF16))


def _qk(q, k):
    return lax.dot_general(q, k, (((1,), (1,)), ((), ())), preferred_element_type=F32)


def _attn_kernel(q_ref, k_ref, v_ref, kc_ref, vc_ref, wb_ref, o_ref, *, R, G, rows):
    jb = pl.program_id(2)
    kc = kc_ref[...]
    vc = vc_ref[...]
    W = GRID_W
    nwin = NA_KH * W
    lane = lax.broadcasted_iota(jnp.int32, (W, 2 * HEAD_DIM), 1)

    def group(g, carry):
        qoff = pl.multiple_of(g * (R * W), R * W)
        qcat = []
        for rr in range(R):
            qe, qo = _masked_heads(q_ref[pl.ds(qoff + rr * W, W), :])
            qcat += [qe, qo]
        s_ctx_all = _qk(jnp.concatenate(qcat, axis=0), kc)
        o_loc, p_ctx, dens = [], [], []
        for rr in range(R):
            r = (jb * G + g) * R + rr
            rs = jnp.clip(r - NA_KH // 2, 0, rows - NA_KH)
            koff = pl.multiple_of(rs * W, W)
            kw = k_ref[pl.ds(koff, nwin), :]
            vw = v_ref[pl.ds(koff, nwin), :]
            qc = jnp.concatenate(qcat[2 * rr:2 * rr + 2], axis=0)
            s_loc = _qk(qc, kw) + jnp.concatenate([wb_ref[0, r - rs], wb_ref[1, r - rs]], axis=0)
            s_ctx = s_ctx_all[rr * 2 * W:(rr + 1) * 2 * W]
            m = jnp.maximum(jnp.max(s_loc, axis=-1, keepdims=True), jnp.max(s_ctx, axis=-1, keepdims=True))
            p_loc = jnp.exp2(s_loc - m)
            pc = jnp.exp2(s_ctx - m)
            dens.append(jnp.sum(p_loc, axis=-1, keepdims=True) + jnp.sum(pc, axis=-1, keepdims=True))
            o_loc.append(jnp.dot(p_loc.astype(BF16), vw, preferred_element_type=F32))
            p_ctx.append(pc.astype(BF16))
        o_ctx_all = jnp.dot(jnp.concatenate(p_ctx, axis=0), vc, preferred_element_type=F32)
        outs = []
        for rr in range(R):
            o = (o_loc[rr] + o_ctx_all[rr * 2 * W:(rr + 1) * 2 * W]) / dens[rr]
            outs.append(jnp.where(lane < HEAD_DIM, o[:W], o[W:]).astype(BF16))
        o_ref[pl.ds(qoff, R * W), :] = jnp.concatenate(outs, axis=0)
        return carry

    lax.fori_loop(0, G, group, 0, unroll=True)


def _attn_lat(q, k, v, wb, *, l, B, S, CTX):
    T, na_w = q.shape
    npair = na_w // (2 * HEAD_DIM)
    rows = S // GRID_W
    R = 8
    G = 8 if rows % (8 * R) == 0 else 1
    nrb = rows // (R * G)
    pw = 2 * HEAD_DIM
    ctx0 = (B * S) // CTX
    qblk = R * G * GRID_W
    return pl.pallas_call(
        functools.partial(_attn_kernel, R=R, G=G, rows=rows),
        grid=(B, npair, nrb),
        in_specs=[pl.BlockSpec((qblk, pw), lambda b, p, j: (b * nrb + j, p)),
                  pl.BlockSpec((S, pw), lambda b, p, j: (b, p)),
                  pl.BlockSpec((S, pw), lambda b, p, j: (b, p)),
                  pl.BlockSpec((CTX, pw), lambda b, p, j: (ctx0 + b, p)),
                  pl.BlockSpec((CTX, pw), lambda b, p, j: (ctx0 + b, p)),
                  pl.BlockSpec((None, None, 2, NA_KH, GRID_W, NA_KH * GRID_W), lambda b, p, j: (l, p, 0, 0, 0, 0))],
        out_specs=pl.BlockSpec((qblk, pw), lambda b, p, j: (b * nrb + j, p)),
        out_shape=jax.ShapeDtypeStruct((T, na_w), BF16),
        compiler_params=_cparams(("parallel", "parallel", "parallel")),
        name="attn_lat",
    )(q, k, v, k, v, wb)


def _attn_ctx_kernel(q_ref, k_ref, v_ref, na_hbm_ref, o_ref):
    del na_hbm_ref
    k = k_ref[...]
    v = v_ref[...]
    lane = lax.broadcasted_iota(jnp.int32, o_ref.shape, 1)
    outs = []
    for qm in _masked_heads(q_ref[...]):
        s = _qk(qm, k)
        p = jnp.exp2(s - jnp.max(s, axis=-1, keepdims=True))
        den = jnp.sum(p, axis=-1, keepdims=True)
        outs.append(jnp.dot(p.astype(BF16), v, preferred_element_type=F32) / den)
    o_ref[...] = jnp.where(lane < HEAD_DIM, outs[0], outs[1]).astype(BF16)


def _attn_ctx(q, k, v, na, *, B, S, CTX):
    T, na_w = q.shape
    pw = 2 * HEAD_DIM
    ctx0 = (B * S) // CTX
    spec = pl.BlockSpec((CTX, pw), lambda b, p: (ctx0 + b, p))
    return pl.pallas_call(
        _attn_ctx_kernel,
        grid=(B, na_w // pw),
        in_specs=[spec, spec, spec, pl.BlockSpec(memory_space=pl.ANY)],
        out_specs=spec,
        out_shape=jax.ShapeDtypeStruct((T, na_w), BF16),
        input_output_aliases={3: 0},
        compiler_params=_cparams(("parallel", "parallel")),
        name="attn_ctx",
    )(q, k, v, na)


def _conv4(x_prev, x_main, x_next, cw):
    n = x_main.shape[0]
    xe = jnp.concatenate([x_prev, x_main, x_next], axis=0)
    ne = n + 2 * SUBLANES
    y = cw[2:3] * x_main
    y = y + cw[0:1] * pltpu.roll(xe, 2, 0)[SUBLANES:SUBLANES + n]
    y = y + cw[1:2] * pltpu.roll(xe, 1, 0)[SUBLANES:SUBLANES + n]
    y = y + cw[3:4] * pltpu.roll(xe, ne - 1, 0)[SUBLANES:SUBLANES + n]
    return y + cw[4:5]


def _softplus(x):
    return jnp.maximum(x, 0.0) + jnp.log1p(jnp.exp(-jnp.abs(x)))


def _lru_coeffs(xl, gates, vec):
    C = xl.shape[1]
    r = jax.nn.sigmoid(gates[:, :C] + vec[0:1])
    i = jax.nn.sigmoid(gates[:, C:] + vec[1:2])
    log_a = (-LRU_C * r) * _softplus(-vec[2:3])
    a = jnp.exp(log_a)
    b = jnp.sqrt(-jnp.tanh(log_a) * (1.0 + a * a)) * (i * xl)
    return a, b


def _group_scan(a, b, reverse):
    n, C = a.shape
    a = a.reshape(n // SUBLANES, SUBLANES, C)
    b = b.reshape(n // SUBLANES, SUBLANES, C)
    row = lax.broadcasted_iota(jnp.int32, a.shape, 1)
    for k in (1, 2, 4):
        shift = SUBLANES - k if reverse else k
        mask = (row < SUBLANES - k) if reverse else (row >= k)
        a_s = pltpu.roll(a, shift, 1)
        b_s = pltpu.roll(b, shift, 1)
        b = jnp.where(mask, a * b_s + b, b)
        a = jnp.where(mask, a * a_s, a)
    return a.reshape(n, C), b.reshape(n, C)


def _chunk_scan(a, b, h_in, reverse, acum_ref, bcum_ref, out_ref):
    n, C = a.shape
    ac, bc = _group_scan(a, b, reverse)
    acum_ref[...] = ac
    bcum_ref[...] = bc
    ng = n // SUBLANES

    def body(g, hb):
        idx = (ng - 1 - g) if reverse else g
        sl = pl.ds(pl.multiple_of(idx * SUBLANES, SUBLANES), SUBLANES)
        h = acum_ref[sl, :] * hb + bcum_ref[sl, :]
        out_ref[sl, :] = h
        edge = h[0:1] if reverse else h[SUBLANES - 1:SUBLANES]
        return jnp.broadcast_to(edge, (SUBLANES, C))

    return lax.fori_loop(0, ng, body, h_in, unroll=8)


def _lru_ctx_kernel(xr_ref, gg_ref, cw_ref, w_ref, vec_ref, o_ref, hend_ref, acum_ref, bcum_ref, hf_ref, hb_ref):
    xr = xr_ref[...]
    zeros8 = jnp.zeros((SUBLANES, xr.shape[1]), F32)
    xl = _conv4(zeros8, xr, zeros8, cw_ref[...])
    gates = jnp.dot(xl.astype(BF16), w_ref[...], preferred_element_type=F32)
    half = gates.shape[1] // 2
    a, b = _lru_coeffs(xl, gates[:, :half], vec_ref[0])
    hf_end = _chunk_scan(a, b, zeros8, False, acum_ref, bcum_ref, hf_ref)
    a, b = _lru_coeffs(xl, gates[:, half:], vec_ref[1])
    hb_end = _chunk_scan(a, b, zeros8, True, acum_ref, bcum_ref, hb_ref)
    o_ref[...] = ((hf_ref[...] + hb_ref[...]) * gg_ref[...]).astype(BF16)
    row = lax.broadcasted_iota(jnp.int32, hf_end.shape, 0)
    hend_ref[...] = jnp.where(row == 0, hf_end, hb_end)


def _lru_ctx(xr, gg, cw, w, vec, *, l, B, S, CTX):
    C = xr.shape[1]
    ctx0 = (B * S) // CTX
    spec = pl.BlockSpec((CTX, C), lambda b: (ctx0 + b, 0))
    return pl.pallas_call(
        _lru_ctx_kernel,
        grid=(B,),
        in_specs=[spec, spec,
                  pl.BlockSpec((None, SUBLANES, C), lambda b: (l, 0, 0)),
                  pl.BlockSpec((None, C, 4 * C), lambda b: (l, 0, 0)),
                  pl.BlockSpec((None, 2, SUBLANES, C), lambda b: (l, 0, 0, 0))],
        out_specs=[spec,
                   pl.BlockSpec((None, SUBLANES, C), lambda b: (b, 0, 0))],
        out_shape=[jax.ShapeDtypeStruct((xr.shape[0], C), BF16),
                   jax.ShapeDtypeStruct((B, SUBLANES, C), F32)],
        scratch_shapes=[pltpu.VMEM((CTX, C), F32)] * 4,
        compiler_params=_cparams(("parallel",)),
        name="lru_ctx",
    )(xr, gg, cw, w, vec)


def _lru_lat_kernel(*refs, reverse):
    if reverse:
        a_ref, b_ref, hend_ref, o_ref, carry_ref, acum_ref, bcum_ref = refs
    else:
        a_ref, b_ref, hend_ref, gg_ref, hb_ref, _lru_hbm_ref, o_ref, carry_ref, acum_ref, bcum_ref, hf_ref = refs
    C = a_ref.shape[1]

    @pl.when(pl.program_id(1) == 0)
    def _():
        row = hend_ref[1:2, :] if reverse else hend_ref[0:1, :]
        carry_ref[...] = jnp.broadcast_to(row, (SUBLANES, C))

    if reverse:
        carry_ref[...] = _chunk_scan(a_ref[...], b_ref[...], carry_ref[...], True, acum_ref, bcum_ref, o_ref)
    else:
        carry_ref[...] = _chunk_scan(a_ref[...], b_ref[...], carry_ref[...], False, acum_ref, bcum_ref, hf_ref)
        o_ref[...] = ((hf_ref[...] + hb_ref[...]) * gg_ref[...]).astype(BF16)


def _lru_lat(a, b, gg, hb, lru, hend, *, reverse, B, S, tt):
    C = a.shape[1]
    nch = S // tt
    chunk = (lambda i: nch - 1 - i) if reverse else (lambda i: i)
    main = pl.BlockSpec((tt, C), lambda b_, i: (b_ * nch + chunk(i), 0))
    in_specs = [main, main, pl.BlockSpec((None, SUBLANES, C), lambda b_, i: (b_, 0, 0))]
    args = [a, b, hend]
    scratch = [pltpu.VMEM((SUBLANES, C), F32), pltpu.VMEM((tt, C), F32), pltpu.VMEM((tt, C), F32)]
    if reverse:
        out_shape, aliases = jax.ShapeDtypeStruct((B * S, C), F32), {}
    else:
        in_specs += [main, main, pl.BlockSpec(memory_space=pl.ANY)]
        args += [gg, hb, lru]
        scratch.append(pltpu.VMEM((tt, C), F32))
        out_shape, aliases = jax.ShapeDtypeStruct(lru.shape, BF16), {len(args) - 1: 0}
    return pl.pallas_call(
        functools.partial(_lru_lat_kernel, reverse=reverse),
        grid=(B, nch),
        in_specs=in_specs,
        out_specs=main,
        out_shape=out_shape,
        input_output_aliases=aliases,
        scratch_shapes=scratch,
        compiler_params=_cparams(("parallel", "arbitrary")),
        name="lru_bwd" if reverse else "lru_fwd",
    )(*args)


def _fno_w_kernel(fw_ref, cs_ref, o_ref):
    fw = fw_ref[...]
    o_ref[0] = jnp.dot(cs_ref[0], fw, preferred_element_type=F32, precision=HIGHEST).astype(BF16)
    o_ref[1] = jnp.dot(cs_ref[1], fw, preferred_element_type=F32, precision=HIGHEST).astype(BF16)


def _fno_weights(fw_bd, cs_bd):
    L, C, _ = fw_bd.shape
    return pl.pallas_call(
        _fno_w_kernel,
        grid=(L,),
        in_specs=[pl.BlockSpec((None, C, C), lambda l: (l, 0, 0)),
                  pl.BlockSpec((2, C, C), lambda l: (0, 0, 0))],
        out_specs=pl.BlockSpec((None, 2, C, C), lambda l: (l, 0, 0, 0)),
        out_shape=jax.ShapeDtypeStruct((L, 2, C, C), BF16),
        compiler_params=_cparams(("parallel",)),
        name="fno_w",
    )(fw_bd, cs_bd)


def _fno_a_kernel(x_ref, w_ref, m_ref, o_ref, *, nb):
    n1 = x_ref.shape[0] // nb
    _, s, C = x_ref.shape
    uvs = []
    for bb in range(nb):
        x = x_ref[bb * n1:(bb + 1) * n1].reshape(n1 * s, C).astype(BF16)
        u = jnp.dot(x, w_ref[0], preferred_element_type=F32).astype(BF16)
        v = jnp.dot(x, w_ref[1], preferred_element_type=F32).astype(BF16)
        uvs.append(jnp.concatenate([u, v], axis=0))
    uv = jnp.concatenate(uvs, axis=1)
    a = jnp.dot(m_ref[...], uv, preferred_element_type=F32).astype(BF16)
    rows = 2 * s
    for bb in range(nb):
        for k1 in range(n1):
            o_ref[bb, :, k1 * C:(k1 + 1) * C] = a[k1 * rows:(k1 + 1) * rows, bb * C:(bb + 1) * C]


def _fno_c_kernel(a_ref, f_ref, o_ref, *, scale):
    res = jnp.dot(f_ref[...], a_ref[...], preferred_element_type=F32) * scale
    o_ref[...] = res.reshape(o_ref.shape).astype(BF16)


def _fno_lat(f, w_l, m_a, f_c, *, B, S):
    T, C = f.shape
    n1 = FNO_N1
    n2 = S // n1
    nj = n2 // SUBLANES
    nb = 4 if B % 4 == 0 else 1
    f4 = f.reshape(T // n2, nj, SUBLANES, C)
    rows_a = 2 * SUBLANES * n1
    a2 = pl.pallas_call(
        functools.partial(_fno_a_kernel, nb=nb),
        grid=(nj, B // nb),
        in_specs=[pl.BlockSpec((nb * n1, None, SUBLANES, C), lambda j, b: (b, j, 0, 0)),
                  pl.BlockSpec((2, C, C), lambda j, b: (0, 0, 0)),
                  pl.BlockSpec((None, rows_a, rows_a), lambda j, b: (j, 0, 0))],
        out_specs=pl.BlockSpec((nb, 2 * SUBLANES, n1 * C), lambda j, b: (b, j, 0)),
        out_shape=jax.ShapeDtypeStruct((B, 2 * n2, n1 * C), BF16),
        compiler_params=_cparams(("parallel", "parallel")),
        name="fno_a",
    )(f4, w_l, m_a)
    ncol = n1 * C
    tc = min(ncol, 4096)
    scale = 1.0 / math.sqrt(S * (C // FNO_GROUPS))
    out = pl.pallas_call(
        functools.partial(_fno_c_kernel, scale=scale),
        grid=(B, ncol // tc),
        in_specs=[pl.BlockSpec((None, 2 * n2, tc), lambda b, t: (b, 0, t)),
                  pl.BlockSpec((n2, 2 * n2), lambda b, t: (0, 0))],
        out_specs=pl.BlockSpec((None, n2, tc // C, C), lambda b, t: (b, 0, t, 0)),
        out_shape=jax.ShapeDtypeStruct((B, n2, n1, C), BF16),
        compiler_params=_cparams(("parallel", "parallel")),
        name="fno_c",
    )(a2, f_c)
    return out.reshape(B * S, C)


def _fno_ctx_kernel(x_ref, w_ref, f_ref, o_ref, *, scale):
    x = x_ref[...].astype(BF16)
    u = jnp.dot(x, w_ref[0], preferred_element_type=F32).astype(BF16)
    v = jnp.dot(x, w_ref[1], preferred_element_type=F32).astype(BF16)
    uv = jnp.concatenate([u, v], axis=0)
    o_ref[...] = (jnp.dot(f_ref[...], uv, preferred_element_type=F32) * scale).astype(BF16)


def _fno_ctx(f, w_l, f_ctx, *, B, S, CTX):
    C = f.shape[1]
    ctx0 = (B * S) // CTX
    scale = 1.0 / math.sqrt(CTX * (C // FNO_GROUPS))
    return pl.pallas_call(
        functools.partial(_fno_ctx_kernel, scale=scale),
        grid=(B,),
        in_specs=[pl.BlockSpec((CTX, C), lambda b: (ctx0 + b, 0)),
                  pl.BlockSpec((2, C, C), lambda b: (0, 0, 0)),
                  pl.BlockSpec((CTX, 2 * CTX), lambda b: (0, 0))],
        out_specs=pl.BlockSpec((CTX, C), lambda b: (b, 0)),
        out_shape=jax.ShapeDtypeStruct((B * CTX, C), BF16),
        compiler_params=_cparams(("parallel",)),
        name="fno_ctx",
    )(f, w_l, f_ctx)


def _dft_constants(S, CTX, C):
    gd = C // FNO_GROUPS
    n1, n2 = FNO_N1, S // FNO_N1
    nj = n2 // SUBLANES
    idx = jnp.arange(gd, dtype=jnp.int32)
    ang = (2.0 * math.pi / gd) * ((idx[:, None] * idx[None, :]) % gd).astype(F32)
    eye_g = jnp.eye(FNO_GROUPS, dtype=F32)
    cs_bd = jnp.stack([jnp.kron(eye_g, jnp.cos(ang)), jnp.kron(eye_g, jnp.sin(ang))])
    k1 = jnp.arange(n1, dtype=jnp.int32)[:, None, None]
    nn1 = jnp.arange(n1, dtype=jnp.int32)[None, :, None]
    nn2 = jnp.arange(n2, dtype=jnp.int32)[None, None, :]
    ph = (2.0 * math.pi / S) * ((k1 * (n2 * nn1 + nn2)) % S).astype(F32)
    cph = jnp.cos(ph).reshape(n1, n1, nj, SUBLANES).transpose(2, 0, 3, 1)
    sph = jnp.sin(ph).reshape(n1, n1, nj, SUBLANES).transpose(2, 0, 3, 1)
    t4 = jnp.stack([jnp.stack([cph, -sph], axis=3), jnp.stack([-sph, -cph], axis=3)], axis=2)
    nr, nq = 2 * SUBLANES * n1, 2 * n1
    expand = (jnp.arange(nq * SUBLANES, dtype=jnp.int32)[None, :] // SUBLANES
              == jnp.arange(nq, dtype=jnp.int32)[:, None]).astype(BF16)
    m_a = jnp.einsum('jrq,qc->jrc', t4.reshape(nj, nr, nq).astype(BF16), expand, preferred_element_type=F32)
    same_s = (jnp.arange(nr, dtype=jnp.int32)[:, None] % SUBLANES
              == jnp.arange(nq * SUBLANES, dtype=jnp.int32)[None, :] % SUBLANES)
    m_a = jnp.where(same_s[None], m_a, 0.0).astype(BF16)
    i2 = jnp.arange(n2, dtype=jnp.int32)
    a2 = (2.0 * math.pi / n2) * ((i2[:, None] * i2[None, :]) % n2).astype(F32)
    f_c = jnp.stack([jnp.cos(a2).reshape(n2, nj, SUBLANES), jnp.sin(a2).reshape(n2, nj, SUBLANES)], axis=2)
    f_c = f_c.reshape(n2, 2 * n2).astype(BF16)
    ic = jnp.arange(CTX, dtype=jnp.int32)
    ac = (2.0 * math.pi / CTX) * ((ic[:, None] * ic[None, :]) % CTX).astype(F32)
    f_ctx = jnp.concatenate([jnp.cos(ac), -jnp.sin(ac)], axis=1).astype(BF16)
    return cs_bd, m_a, f_c, f_ctx


def _block_diag(w):
    G, n = w.shape[-3], w.shape[-1]
    eye = jnp.eye(G, dtype=w.dtype)
    out = jnp.einsum('...gij,gh->...gihj', w, eye)
    return out.reshape(w.shape[:-3] + (G * n, G * n))


def kernel(x, c, ctx, c_ctx, w_ada, b_ada, ln_g, ln_b, ff1_gate, ff1_up, ff1_down, ff2_gate, ff2_up, ff2_down,
           w_in, w_out, na_rpb, lru_conv_w, lru_conv_b, lru_wa, lru_ba, lru_wx, lru_bx, lru_lambda, fno_w):
    B, S, D = x.shape
    CTX = ctx.shape[1]
    L = w_ada.shape[0]
    na_w = na_rpb.shape[1] * HEAD_DIM
    lru_w = lru_conv_w.shape[2]
    alpha = float((2 * L) ** 0.25)
    tm = min(1024, B * CTX)
    tiles_per_batch = S // tm
    nlat = (B * S) // tm
    ntiles = nlat + (B * CTX) // tm
    assert B < 16 and S % tm == 0 and (B * CTX) % tm == 0 and S % (GRID_W * NA_KH) == 0

    cc = jnp.zeros((16, D), F32).at[:B].set(c).at[B].set(c_ctx)
    mods = _ada_params(cc, w_ada, b_ada)

    bf = lambda w: w.astype(BF16)
    ff1 = (bf(ff1_gate), bf(ff1_up), bf(ff1_down))
    ff2 = (bf(ff2_gate), bf(ff2_up), bf(ff2_down))
    w_in_b, w_out_b = bf(w_in), bf(w_out)
    ln_g4 = ln_g[:, :, None, :]
    ln_b4 = ln_b[:, :, None, :]
    wb = _na_bias_tables(na_rpb)
    gate_w = jnp.concatenate([_block_diag(lru_wa), _block_diag(lru_wx)], axis=-1)
    lru_gate_w = bf(jnp.concatenate([gate_w[:, 0], gate_w[:, 1]], axis=-1))
    zrow = jnp.zeros((L, 2, SUBLANES - 3, lru_w), F32)
    lru_vec = jnp.concatenate([lru_ba[:, :, None], lru_bx[:, :, None], lru_lambda[:, :, None], zrow], axis=2)
    lru_cw = jnp.concatenate([lru_conv_w, lru_conv_b[:, None], jnp.zeros((L, SUBLANES - LRU_CONV - 1, lru_w), F32)], axis=1)
    cs_bd, m_a, f_c, f_ctx = _dft_constants(S, CTX, D - na_w - lru_w)
    fno_wcs = _fno_weights(_block_diag(fno_w), cs_bd)

    common = dict(tm=tm, tiles_per_batch=tiles_per_batch, nbatch=B)
    xs = (x.reshape(B * S, D), ctx.reshape(B * CTX, D))
    for l in range(L):
        last = l == L - 1
        x1 = _ffn(xs, mods, *ff1, ln_g4[:, 0], ln_b4[:, 0], j=0, l=l, alpha=alpha, nlat=nlat, ntiles=ntiles, **common)
        q, k, v, xr, gg, f, a_f, b_f, a_b, b_b = _inproj(x1, mods, w_in_b, lru_cw, lru_gate_w, lru_vec,
                                                        l=l, na_w=na_w, lru_w=lru_w, **common)
        na = _attn_lat(q, k, v, wb, l=l, B=B, S=S, CTX=CTX)
        lru, hend = _lru_ctx(xr, gg, lru_cw, lru_gate_w, lru_vec, l=l, B=B, S=S, CTX=CTX)
        tt = min(S, 2048)
        hb = _lru_lat(a_b, b_b, None, None, None, hend, reverse=True, B=B, S=S, tt=tt)
        lru = _lru_lat(a_f, b_f, gg, hb, lru, hend, reverse=False, B=B, S=S, tt=tt)
        fno_lat = _fno_lat(f, fno_wcs[l], m_a, f_c, B=B, S=S)
        if last:
            fno_ctx, nt = None, nlat
        else:
            na = _attn_ctx(q, k, v, na, B=B, S=S, CTX=CTX)
            fno_ctx, nt = _fno_ctx(f, fno_wcs[l], f_ctx, B=B, S=S, CTX=CTX), ntiles
        x3 = _ffn((x1,), mods, *ff2, ln_g4[:, 2], ln_b4[:, 2], j=2, l=l, alpha=alpha, nlat=nlat, ntiles=nt,
                  mix=(na, lru, fno_lat, fno_ctx, w_out_b, ln_g4[:, 1], ln_b4[:, 1]), **common)
        xs = (x3,)
    return xs[0].reshape(B, S, D)
```

```python
import functools
import math

import numpy as np
import jax
import jax.numpy as jnp
from jax import lax
from jax.experimental import pallas as pl
from jax.experimental.pallas import tpu as pltpu

F32 = jnp.float32
BF16 = jnp.bfloat16
HIGHEST = lax.Precision.HIGHEST

HEAD_DIM = 64
GRID_W = 64
NA_KH = 8
NA_KW = 16
LRU_HEADS = 4
LRU_CONV = 4
LRU_C = 8.0
FNO_GROUPS = 4
N_MOD = 9
MACARON = 0.5
LN_EPS = 1e-5
MASK_BIAS = -1e30
LOG2E = math.log2(math.e)
SUBLANES = 8
FNO_N1 = 64
FFN_ROW_BLOCKS = 4
VMEM_LIMIT = 56 * 1024 * 1024


def _cparams(sem):
    return pltpu.CompilerParams(dimension_semantics=sem, vmem_limit_bytes=VMEM_LIMIT)


def _layer_norm(z, g, b):
    mu = jnp.mean(z, axis=-1, keepdims=True)
    zc = z - mu
    var = jnp.mean(zc * zc, axis=-1, keepdims=True)
    return zc * lax.rsqrt(var + LN_EPS) * g + b


def _ada_kernel(c_ref, w_ref, b_ref, o_ref):
    c = c_ref[...]
    a = c * jax.nn.sigmoid(c)
    o_ref[...] = jnp.dot(a, w_ref[...], preferred_element_type=F32, precision=HIGHEST) + b_ref[...]


def _ada_params(cc, w_ada, b_ada):
    L, D, ND = w_ada.shape
    tn = D
    out = pl.pallas_call(
        _ada_kernel,
        grid=(L, ND // tn),
        in_specs=[pl.BlockSpec((16, D), lambda l, n: (0, 0)),
                  pl.BlockSpec((None, D, tn), lambda l, n: (l, 0, n)),
                  pl.BlockSpec((None, 1, tn), lambda l, n: (l, 0, n))],
        out_specs=pl.BlockSpec((None, 16, tn), lambda l, n: (l, 0, n)),
        out_shape=jax.ShapeDtypeStruct((L, 16, ND), F32),
        compiler_params=_cparams(("parallel", "parallel")),
        name="ada",
    )(cc, w_ada, b_ada.reshape(L, 1, ND))
    return out.reshape(L, 16, N_MOD, D)


def _ffn_kernel(*refs, j, tf, alpha, nlat, mode):
    m = refs[2 if mode == "two_x" else 1][...]
    if mode == "two_x":
        xl_ref, xc_ref, _, wg_ref, wu_ref, wd_ref, g_ref, b_ref, o_ref, h_ref, a_ref = refs
        x = jnp.where(pl.program_id(0) < nlat, xl_ref[...], xc_ref[...])
    elif mode == "plain":
        x_ref, _, wg_ref, wu_ref, wd_ref, g_ref, b_ref, o_ref, h_ref, a_ref = refs
        x = x_ref[...]
    else:
        if mode == "mix_ctx":
            (x_ref, _, na_ref, lru_ref, fl_ref, fc_ref, wo_ref, g1_ref, b1_ref,
             wg_ref, wu_ref, wd_ref, g_ref, b_ref, o_ref, h_ref, a_ref) = refs
            fno = jnp.where(pl.program_id(0) < nlat, fl_ref[...], fc_ref[...])
        else:
            (x_ref, _, na_ref, lru_ref, fl_ref, wo_ref, g1_ref, b1_ref,
             wg_ref, wu_ref, wd_ref, g_ref, b_ref, o_ref, h_ref, a_ref) = refs
            fno = fl_ref[...]
        na_w, lru_w = na_ref.shape[1], lru_ref.shape[1]
        x = None
    shift, scale, gate = m[3 * j:3 * j + 1], m[3 * j + 1:3 * j + 2], m[3 * j + 2:3 * j + 3]
    nblk = FFN_ROW_BLOCKS
    rb = o_ref.shape[0] // nblk
    if x is None:
        for blk in range(nblk):
            rows = slice(blk * rb, (blk + 1) * rb)
            y = jnp.dot(na_ref[rows, :], wo_ref[0:na_w, :], preferred_element_type=F32)
            y = y + jnp.dot(lru_ref[rows, :], wo_ref[na_w:na_w + lru_w, :], preferred_element_type=F32)
            y = y + jnp.dot(fno[rows], wo_ref[na_w + lru_w:, :], preferred_element_type=F32)
            x2 = _layer_norm(alpha * x_ref[rows, :] + m[5:6] * y, g1_ref[...], b1_ref[...])
            o_ref[rows, :] = x2
            h_ref[rows, :] = (x2 * (1.0 + scale) + shift).astype(BF16)
    else:
        h_ref[...] = (x * (1.0 + scale) + shift).astype(BF16)
    for c in range(wg_ref.shape[1] // tf):
        cols = slice(c * tf, (c + 1) * tf)
        h = h_ref[...]
        g = jnp.dot(h, wg_ref[:, cols], preferred_element_type=F32)
        u = jnp.dot(h, wu_ref[:, cols], preferred_element_type=F32)
        a_ref[:, cols] = (g * jax.nn.sigmoid(g) * u).astype(BF16)
    xsrc = o_ref if mode.startswith("mix") else (None if mode == "two_x" else x_ref)
    for blk in range(nblk):
        rows = slice(blk * rb, (blk + 1) * rb)
        y = jnp.dot(a_ref[rows, :], wd_ref[...], preferred_element_type=F32)
        xb = x[rows] if xsrc is None else xsrc[rows, :]
        o_ref[rows, :] = _layer_norm(alpha * xb + (MACARON * gate) * y, g_ref[...], b_ref[...])


def _ffn(xs, mods_l, wg, wu, wd, ln_g, ln_b, *, j, l, alpha, tm, tiles_per_batch, nlat, ntiles, nbatch, mix=None):
    two_x = len(xs) == 2
    mode = "two_x" if two_x else "plain"
    D = xs[0].shape[1]
    Fd = wg.shape[2]
    tf = 256
    row_idx = lambda i: jnp.minimum(i // tiles_per_batch, nbatch)
    if two_x:
        nctx = ntiles - nlat
        x_specs = [pl.BlockSpec((tm, D), lambda i: (jnp.minimum(i, nlat - 1), 0)),
                   pl.BlockSpec((tm, D), lambda i: (jnp.clip(i - nlat, 0, nctx - 1), 0))]
    else:
        x_specs = [pl.BlockSpec((tm, D), lambda i: (i, 0))]
    const = pl.Buffered(1)
    mix_specs, mix_args = [], []
    if mix is not None:
        na, lru, fno_l, fno_c, w_out, g1, b1 = mix
        mode = "mix" if fno_c is None else "mix_ctx"
        tile = lambda a: pl.BlockSpec((tm, a.shape[1]), lambda i: (i, 0))
        mix_specs = [tile(na), tile(lru), pl.BlockSpec((tm, fno_l.shape[1]), lambda i: (jnp.minimum(i, nlat - 1), 0))]
        mix_args = [na, lru, fno_l]
        if fno_c is not None:
            mix_specs.append(pl.BlockSpec((tm, fno_c.shape[1]), lambda i: (jnp.clip(i - nlat, 0, ntiles - nlat - 1), 0)))
            mix_args.append(fno_c)
        mix_specs += [pl.BlockSpec((None, D, D), lambda i: (l, 0, 0), pipeline_mode=const),
                      pl.BlockSpec((None, 1, D), lambda i: (l, 0, 0)),
                      pl.BlockSpec((None, 1, D), lambda i: (l, 0, 0))]
        mix_args += [w_out, g1, b1]
    return pl.pallas_call(
        functools.partial(_ffn_kernel, j=j, tf=tf, alpha=alpha, nlat=nlat, mode=mode),
        grid=(ntiles,),
        in_specs=x_specs + [
            pl.BlockSpec((None, None, N_MOD, D), lambda i: (l, row_idx(i), 0, 0))] + mix_specs + [
            pl.BlockSpec((None, D, Fd), lambda i: (l, 0, 0), pipeline_mode=const),
            pl.BlockSpec((None, D, Fd), lambda i: (l, 0, 0), pipeline_mode=const),
            pl.BlockSpec((None, Fd, D), lambda i: (l, 0, 0), pipeline_mode=const),
            pl.BlockSpec((None, 1, D), lambda i: (l, 0, 0)),
            pl.BlockSpec((None, 1, D), lambda i: (l, 0, 0))],
        out_specs=pl.BlockSpec((tm, D), lambda i: (i, 0)),
        out_shape=jax.ShapeDtypeStruct((ntiles * tm, D), F32),
        scratch_shapes=[pltpu.VMEM((tm, D), BF16), pltpu.VMEM((tm, Fd), BF16)],
        compiler_params=_cparams(("parallel",)),
        name=f"ffn{j}",
    )(*xs, mods_l, *mix_args, wg, wu, wd, ln_g, ln_b)


def _inproj_kernel(x_ref, xp_ref, xn_ref, m_ref, w_ref, cw_ref, wg_ref, vec_ref,
                   q_ref, k_ref, v_ref, xr_ref, gg_ref, f_ref, af_ref, bf_ref, ab_ref, bb_ref,
                   h_ref, xl_ref, gates_ref, *, na_w, lru_w, tiles_per_batch):
    m = m_ref[...]
    modulate = lambda xs: (xs * (1.0 + m[4:5]) + m[3:4]).astype(BF16)
    h_ref[...] = modulate(x_ref[...])

    def proj(lo, width, hs=None):
        return jnp.dot(h_ref[...] if hs is None else hs, w_ref[:, lo:lo + width], preferred_element_type=F32)

    xr_lo = 3 * na_w
    xr = proj(xr_lo, lru_w)
    xr_ref[...] = xr
    pos = pl.program_id(0) % tiles_per_batch
    xrp = jnp.where(pos == 0, 0.0, proj(xr_lo, lru_w, modulate(xp_ref[...])))
    xrn = jnp.where(pos == tiles_per_batch - 1, 0.0, proj(xr_lo, lru_w, modulate(xn_ref[...])))
    gr = proj(3 * na_w + lru_w, lru_w)
    q_ref[...] = (proj(0, na_w) * (HEAD_DIM ** -0.5 * LOG2E)).astype(BF16)
    xl_ref[...] = _conv4(xrp, xr, xrn, cw_ref[...])
    gg_ref[...] = jax.nn.gelu(gr).astype(BF16)
    xlb = xl_ref[...].astype(BF16)
    gates = jnp.dot(xlb, wg_ref[:, 0:2 * lru_w], preferred_element_type=F32)
    k_ref[...] = proj(na_w, na_w).astype(BF16)
    a, b = _lru_coeffs(xl_ref[...], gates, vec_ref[0])
    af_ref[...] = a
    bf_ref[...] = b
    gates = jnp.dot(xlb, wg_ref[:, 2 * lru_w:4 * lru_w], preferred_element_type=F32)
    v_ref[...] = proj(2 * na_w, na_w).astype(BF16)
    a, b = _lru_coeffs(xl_ref[...], gates, vec_ref[1])
    ab_ref[...] = a
    bb_ref[...] = b
    f_ref[...] = proj(3 * na_w + 2 * lru_w, w_ref.shape[1] - 3 * na_w - 2 * lru_w)


def _inproj(x_all, mods_l, w_in, lru_cw, lru_gate_w, lru_vec, *, l, tm, tiles_per_batch, nbatch, na_w, lru_w):
    T, D = x_all.shape
    cols = w_in.shape[2]
    fno_w = cols - 3 * na_w - 2 * lru_w
    row_idx = lambda i: jnp.minimum(i // tiles_per_batch, nbatch)
    nblk8 = T // SUBLANES
    widths = ([(na_w, BF16)] * 3 + [(lru_w, F32), (lru_w, BF16), (fno_w, F32)] + [(lru_w, F32)] * 4)
    const = pl.Buffered(1)
    return pl.pallas_call(
        functools.partial(_inproj_kernel, na_w=na_w, lru_w=lru_w, tiles_per_batch=tiles_per_batch),
        grid=(T // tm,),
        in_specs=[pl.BlockSpec((tm, D), lambda i: (i, 0)),
                  pl.BlockSpec((SUBLANES, D), lambda i: (jnp.maximum(i * (tm // SUBLANES) - 1, 0), 0)),
                  pl.BlockSpec((SUBLANES, D), lambda i: (jnp.minimum((i + 1) * (tm // SUBLANES), nblk8 - 1), 0)),
                  pl.BlockSpec((None, None, N_MOD, D), lambda i: (l, row_idx(i), 0, 0)),
                  pl.BlockSpec((None, D, cols), lambda i: (l, 0, 0), pipeline_mode=const),
                  pl.BlockSpec((None, SUBLANES, lru_w), lambda i: (l, 0, 0)),
                  pl.BlockSpec((None, lru_w, 4 * lru_w), lambda i: (l, 0, 0), pipeline_mode=const),
                  pl.BlockSpec((None, 2, SUBLANES, lru_w), lambda i: (l, 0, 0, 0))],
        out_specs=[pl.BlockSpec((tm, w), lambda i: (i, 0)) for w, _ in widths],
        out_shape=[jax.ShapeDtypeStruct((T, w), dt) for w, dt in widths],
        scratch_shapes=[pltpu.VMEM((tm, D), BF16), pltpu.VMEM((tm, lru_w), F32), pltpu.VMEM((tm, 4 * lru_w), F32)],
        compiler_params=_cparams(("parallel",)),
        name="inproj",
    )(x_all, x_all, x_all, mods_l, w_in, lru_cw, lru_gate_w, lru_vec)


def _na_bias_tables(rpb):
    L, H = rpb.shape[:2]
    W, ndr, ndc = GRID_W, 2 * NA_KH - 1, 2 * NA_KW - 1
    c = np.arange(W)[:, None]
    j = np.arange(W)[None, :]
    col0 = np.clip(c - NA_KW // 2, 0, W - NA_KW)
    valid = (j >= col0) & (j < col0 + NA_KW)
    entry = np.where(valid, np.clip(j - c + (NA_KW - 1), 0, ndc - 1), ndc)
    onehot = (np.arange(ndc + 1)[:, None, None] == entry[None]).astype(np.float32)
    ext = jnp.concatenate([rpb * LOG2E, jnp.full((L, H, ndr, 1), MASK_BIAS, F32)], axis=-1)
    bt = jnp.einsum('lhde,ecj->lhcdj', ext, onehot, precision=HIGHEST).reshape(L, H, W, ndr * W)
    wb = jnp.stack([bt[..., (NA_KH - 1 - case) * W:(2 * NA_KH - 1 - case) * W] for case in range(NA_KH)], axis=2)
    return wb.reshape(L, H // 2, 2, NA_KH, W, NA_KH * W)


def _masked_heads(q):
    lane = lax.broadcasted_iota(jnp.int32, q.shape, 1)
    qf = q.astype(F32)
    return (jnp.where(lane < HEAD_DIM, qf, 0.0).astype(BF16),
            jnp.where(lane >= HEAD_DIM, qf, 0.0).astype(BF16))


def _qk(q, k):
    return lax.dot_general(q, k, (((1,), (1,)), ((), ())), preferred_element_type=F32)


def _attn_kernel(q_ref, k_ref, v_ref, kc_ref, vc_ref, wb_ref, o_ref, *, R, G, rows):
    jb = pl.program_id(2)
    kc = kc_ref[...]
    vc = vc_ref[...]
    W = GRID_W
    nwin = NA_KH * W
    lane = lax.broadcasted_iota(jnp.int32, (W, 2 * HEAD_DIM), 1)

    def group(g, carry):
        qoff = pl.multiple_of(g * (R * W), R * W)
        qcat = []
        for rr in range(R):
            qe, qo = _masked_heads(q_ref[pl.ds(qoff + rr * W, W), :])
            qcat += [qe, qo]
        s_ctx_all = _qk(jnp.concatenate(qcat, axis=0), kc)
        o_loc, p_ctx, dens = [], [], []
        for rr in range(R):
            r = (jb * G + g) * R + rr
            rs = jnp.clip(r - NA_KH // 2, 0, rows - NA_KH)
            koff = pl.multiple_of(rs * W, W)
            kw = k_ref[pl.ds(koff, nwin), :]
            vw = v_ref[pl.ds(koff, nwin), :]
            qc = jnp.concatenate(qcat[2 * rr:2 * rr + 2], axis=0)
            s_loc = _qk(qc, kw) + jnp.concatenate([wb_ref[0, r - rs], wb_ref[1, r - rs]], axis=0)
            s_ctx = s_ctx_all[rr * 2 * W:(rr + 1) * 2 * W]
            m = jnp.maximum(jnp.max(s_loc, axis=-1, keepdims=True), jnp.max(s_ctx, axis=-1, keepdims=True))
            p_loc = jnp.exp2(s_loc - m)
            pc = jnp.exp2(s_ctx - m)
            dens.append(jnp.sum(p_loc, axis=-1, keepdims=True) + jnp.sum(pc, axis=-1, keepdims=True))
            o_loc.append(jnp.dot(p_loc.astype(BF16), vw, preferred_element_type=F32))
            p_ctx.append(pc.astype(BF16))
        o_ctx_all = jnp.dot(jnp.concatenate(p_ctx, axis=0), vc, preferred_element_type=F32)
        outs = []
        for rr in range(R):
            o = (o_loc[rr] + o_ctx_all[rr * 2 * W:(rr + 1) * 2 * W]) / dens[rr]
            outs.append(jnp.where(lane < HEAD_DIM, o[:W], o[W:]).astype(BF16))
        o_ref[pl.ds(qoff, R * W), :] = jnp.concatenate(outs, axis=0)
        return carry

    lax.fori_loop(0, G, group, 0, unroll=True)


def _attn_lat(q, k, v, wb, *, l, B, S, CTX):
    T, na_w = q.shape
    npair = na_w // (2 * HEAD_DIM)
    rows = S // GRID_W
    R = 8
    G = 8 if rows % (8 * R) == 0 else 1
    nrb = rows // (R * G)
    pw = 2 * HEAD_DIM
    ctx0 = (B * S) // CTX
    qblk = R * G * GRID_W
    return pl.pallas_call(
        functools.partial(_attn_kernel, R=R, G=G, rows=rows),
        grid=(B, npair, nrb),
        in_specs=[pl.BlockSpec((qblk, pw), lambda b, p, j: (b * nrb + j, p)),
                  pl.BlockSpec((S, pw), lambda b, p, j: (b, p)),
                  pl.BlockSpec((S, pw), lambda b, p, j: (b, p)),
                  pl.BlockSpec((CTX, pw), lambda b, p, j: (ctx0 + b, p)),
                  pl.BlockSpec((CTX, pw), lambda b, p, j: (ctx0 + b, p)),
                  pl.BlockSpec((None, None, 2, NA_KH, GRID_W, NA_KH * GRID_W), lambda b, p, j: (l, p, 0, 0, 0, 0))],
        out_specs=pl.BlockSpec((qblk, pw), lambda b, p, j: (b * nrb + j, p)),
        out_shape=jax.ShapeDtypeStruct((T, na_w), BF16),
        compiler_params=_cparams(("parallel", "parallel", "parallel")),
        name="attn_lat",
    )(q, k, v, k, v, wb)


def _attn_ctx_kernel(q_ref, k_ref, v_ref, na_hbm_ref, o_ref):
    del na_hbm_ref
    k = k_ref[...]
    v = v_ref[...]
    lane = lax.broadcasted_iota(jnp.int32, o_ref.shape, 1)
    outs = []
    for qm in _masked_heads(q_ref[...]):
        s = _qk(qm, k)
        p = jnp.exp2(s - jnp.max(s, axis=-1, keepdims=True))
        den = jnp.sum(p, axis=-1, keepdims=True)
        outs.append(jnp.dot(p.astype(BF16), v, preferred_element_type=F32) / den)
    o_ref[...] = jnp.where(lane < HEAD_DIM, outs[0], outs[1]).astype(BF16)


def _attn_ctx(q, k, v, na, *, B, S, CTX):
    T, na_w = q.shape
    pw = 2 * HEAD_DIM
    ctx0 = (B * S) // CTX
    spec = pl.BlockSpec((CTX, pw), lambda b, p: (ctx0 + b, p))
    return pl.pallas_call(
        _attn_ctx_kernel,
        grid=(B, na_w // pw),
        in_specs=[spec, spec, spec, pl.BlockSpec(memory_space=pl.ANY)],
        out_specs=spec,
        out_shape=jax.ShapeDtypeStruct((T, na_w), BF16),
        input_output_aliases={3: 0},
        compiler_params=_cparams(("parallel", "parallel")),
        name="attn_ctx",
    )(q, k, v, na)


def _conv4(x_prev, x_main, x_next, cw):
    n = x_main.shape[0]
    xe = jnp.concatenate([x_prev, x_main, x_next], axis=0)
    ne = n + 2 * SUBLANES
    y = cw[2:3] * x_main
    y = y + cw[0:1] * pltpu.roll(xe, 2, 0)[SUBLANES:SUBLANES + n]
    y = y + cw[1:2] * pltpu.roll(xe, 1, 0)[SUBLANES:SUBLANES + n]
    y = y + cw[3:4] * pltpu.roll(xe, ne - 1, 0)[SUBLANES:SUBLANES + n]
    return y + cw[4:5]


def _softplus(x):
    return jnp.maximum(x, 0.0) + jnp.log1p(jnp.exp(-jnp.abs(x)))


def _lru_coeffs(xl, gates, vec):
    C = xl.shape[1]
    r = jax.nn.sigmoid(gates[:, :C] + vec[0:1])
    i = jax.nn.sigmoid(gates[:, C:] + vec[1:2])
    log_a = (-LRU_C * r) * _softplus(-vec[2:3])
    a = jnp.exp(log_a)
    b = jnp.sqrt(-jnp.tanh(log_a) * (1.0 + a * a)) * (i * xl)
    return a, b


def _group_scan(a, b, reverse):
    n, C = a.shape
    a = a.reshape(n // SUBLANES, SUBLANES, C)
    b = b.reshape(n // SUBLANES, SUBLANES, C)
    row = lax.broadcasted_iota(jnp.int32, a.shape, 1)
    for k in (1, 2, 4):
        shift = SUBLANES - k if reverse else k
        mask = (row < SUBLANES - k) if reverse else (row >= k)
        a_s = pltpu.roll(a, shift, 1)
        b_s = pltpu.roll(b, shift, 1)
        b = jnp.where(mask, a * b_s + b, b)
        a = jnp.where(mask, a * a_s, a)
    return a.reshape(n, C), b.reshape(n, C)


def _chunk_scan(a, b, h_in, reverse, acum_ref, bcum_ref, out_ref):
    n, C = a.shape
    ac, bc = _group_scan(a, b, reverse)
    acum_ref[...] = ac
    bcum_ref[...] = bc
    ng = n // SUBLANES

    def body(g, hb):
        idx = (ng - 1 - g) if reverse else g
        sl = pl.ds(pl.multiple_of(idx * SUBLANES, SUBLANES), SUBLANES)
        h = acum_ref[sl, :] * hb + bcum_ref[sl, :]
        out_ref[sl, :] = h
        edge = h[0:1] if reverse else h[SUBLANES - 1:SUBLANES]
        return jnp.broadcast_to(edge, (SUBLANES, C))

    return lax.fori_loop(0, ng, body, h_in, unroll=8)


def _lru_ctx_kernel(xr_ref, gg_ref, cw_ref, w_ref, vec_ref, o_ref, hend_ref, acum_ref, bcum_ref, hf_ref, hb_ref):
    xr = xr_ref[...]
    zeros8 = jnp.zeros((SUBLANES, xr.shape[1]), F32)
    xl = _conv4(zeros8, xr, zeros8, cw_ref[...])
    gates = jnp.dot(xl.astype(BF16), w_ref[...], preferred_element_type=F32)
    half = gates.shape[1] // 2
    a, b = _lru_coeffs(xl, gates[:, :half], vec_ref[0])
    hf_end = _chunk_scan(a, b, zeros8, False, acum_ref, bcum_ref, hf_ref)
    a, b = _lru_coeffs(xl, gates[:, half:], vec_ref[1])
    hb_end = _chunk_scan(a, b, zeros8, True, acum_ref, bcum_ref, hb_ref)
    o_ref[...] = ((hf_ref[...] + hb_ref[...]) * gg_ref[...]).astype(BF16)
    row = lax.broadcasted_iota(jnp.int32, hf_end.shape, 0)
    hend_ref[...] = jnp.where(row == 0, hf_end, hb_end)


def _lru_ctx(xr, gg, cw, w, vec, *, l, B, S, CTX):
    C = xr.shape[1]
    ctx0 = (B * S) // CTX
    spec = pl.BlockSpec((CTX, C), lambda b: (ctx0 + b, 0))
    return pl.pallas_call(
        _lru_ctx_kernel,
        grid=(B,),
        in_specs=[spec, spec,
                  pl.BlockSpec((None, SUBLANES, C), lambda b: (l, 0, 0)),
                  pl.BlockSpec((None, C, 4 * C), lambda b: (l, 0, 0)),
                  pl.BlockSpec((None, 2, SUBLANES, C), lambda b: (l, 0, 0, 0))],
        out_specs=[spec,
                   pl.BlockSpec((None, SUBLANES, C), lambda b: (b, 0, 0))],
        out_shape=[jax.ShapeDtypeStruct((xr.shape[0], C), BF16),
                   jax.ShapeDtypeStruct((B, SUBLANES, C), F32)],
        scratch_shapes=[pltpu.VMEM((CTX, C), F32)] * 4,
        compiler_params=_cparams(("parallel",)),
        name="lru_ctx",
    )(xr, gg, cw, w, vec)


def _lru_lat_kernel(*refs, reverse):
    if reverse:
        a_ref, b_ref, hend_ref, o_ref, carry_ref, acum_ref, bcum_ref = refs
    else:
        a_ref, b_ref, hend_ref, gg_ref, hb_ref, _lru_hbm_ref, o_ref, carry_ref, acum_ref, bcum_ref, hf_ref = refs
    C = a_ref.shape[1]

    @pl.when(pl.program_id(1) == 0)
    def _():
        row = hend_ref[1:2, :] if reverse else hend_ref[0:1, :]
        carry_ref[...] = jnp.broadcast_to(row, (SUBLANES, C))

    if reverse:
        carry_ref[...] = _chunk_scan(a_ref[...], b_ref[...], carry_ref[...], True, acum_ref, bcum_ref, o_ref)
    else:
        carry_ref[...] = _chunk_scan(a_ref[...], b_ref[...], carry_ref[...], False, acum_ref, bcum_ref, hf_ref)
        o_ref[...] = ((hf_ref[...] + hb_ref[...]) * gg_ref[...]).astype(BF16)


def _lru_lat(a, b, gg, hb, lru, hend, *, reverse, B, S, tt):
    C = a.shape[1]
    nch = S // tt
    chunk = (lambda i: nch - 1 - i) if reverse else (lambda i: i)
    main = pl.BlockSpec((tt, C), lambda b_, i: (b_ * nch + chunk(i), 0))
    in_specs = [main, main, pl.BlockSpec((None, SUBLANES, C), lambda b_, i: (b_, 0, 0))]
    args = [a, b, hend]
    scratch = [pltpu.VMEM((SUBLANES, C), F32), pltpu.VMEM((tt, C), F32), pltpu.VMEM((tt, C), F32)]
    if reverse:
        out_shape, aliases = jax.ShapeDtypeStruct((B * S, C), F32), {}
    else:
        in_specs += [main, main, pl.BlockSpec(memory_space=pl.ANY)]
        args += [gg, hb, lru]
        scratch.append(pltpu.VMEM((tt, C), F32))
        out_shape, aliases = jax.ShapeDtypeStruct(lru.shape, BF16), {len(args) - 1: 0}
    return pl.pallas_call(
        functools.partial(_lru_lat_kernel, reverse=reverse),
        grid=(B, nch),
        in_specs=in_specs,
        out_specs=main,
        out_shape=out_shape,
        input_output_aliases=aliases,
        scratch_shapes=scratch,
        compiler_params=_cparams(("parallel", "arbitrary")),
        name="lru_bwd" if reverse else "lru_fwd",
    )(*args)


def _fno_w_kernel(fw_ref, cs_ref, o_ref):
    fw = fw_ref[...]
    o_ref[0] = jnp.dot(cs_ref[0], fw, preferred_element_type=F32, precision=HIGHEST).astype(BF16)
    o_ref[1] = jnp.dot(cs_ref[1], fw, preferred_element_type=F32, precision=HIGHEST).astype(BF16)


def _fno_weights(fw_bd, cs_bd):
    L, C, _ = fw_bd.shape
    return pl.pallas_call(
        _fno_w_kernel,
        grid=(L,),
        in_specs=[pl.BlockSpec((None, C, C), lambda l: (l, 0, 0)),
                  pl.BlockSpec((2, C, C), lambda l: (0, 0, 0))],
        out_specs=pl.BlockSpec((None, 2, C, C), lambda l: (l, 0, 0, 0)),
        out_shape=jax.ShapeDtypeStruct((L, 2, C, C), BF16),
        compiler_params=_cparams(("parallel",)),
        name="fno_w",
    )(fw_bd, cs_bd)


def _fno_a_kernel(x_ref, w_ref, m_ref, o_ref, *, nb):
    n1 = x_ref.shape[0] // nb
    _, s, C = x_ref.shape
    uvs = []
    for bb in range(nb):
        x = x_ref[bb * n1:(bb + 1) * n1].reshape(n1 * s, C).astype(BF16)
        u = jnp.dot(x, w_ref[0], preferred_element_type=F32).astype(BF16)
        v = jnp.dot(x, w_ref[1], preferred_element_type=F32).astype(BF16)
        uvs.append(jnp.concatenate([u, v], axis=0))
    uv = jnp.concatenate(uvs, axis=1)
    a = jnp.dot(m_ref[...], uv, preferred_element_type=F32).astype(BF16)
    rows = 2 * s
    for bb in range(nb):
        for k1 in range(n1):
            o_ref[bb, :, k1 * C:(k1 + 1) * C] = a[k1 * rows:(k1 + 1) * rows, bb * C:(bb + 1) * C]


def _fno_c_kernel(a_ref, f_ref, o_ref, *, scale):
    res = jnp.dot(f_ref[...], a_ref[...], preferred_element_type=F32) * scale
    o_ref[...] = res.reshape(o_ref.shape).astype(BF16)


def _fno_lat(f, w_l, m_a, f_c, *, B, S):
    T, C = f.shape
    n1 = FNO_N1
    n2 = S // n1
    nj = n2 // SUBLANES
    nb = 4 if B % 4 == 0 else 1
    f4 = f.reshape(T // n2, nj, SUBLANES, C)
    rows_a = 2 * SUBLANES * n1
    a2 = pl.pallas_call(
        functools.partial(_fno_a_kernel, nb=nb),
        grid=(nj, B // nb),
        in_specs=[pl.BlockSpec((nb * n1, None, SUBLANES, C), lambda j, b: (b, j, 0, 0)),
                  pl.BlockSpec((2, C, C), lambda j, b: (0, 0, 0)),
                  pl.BlockSpec((None, rows_a, rows_a), lambda j, b: (j, 0, 0))],
        out_specs=pl.BlockSpec((nb, 2 * SUBLANES, n1 * C), lambda j, b: (b, j, 0)),
        out_shape=jax.ShapeDtypeStruct((B, 2 * n2, n1 * C), BF16),
        compiler_params=_cparams(("parallel", "parallel")),
        name="fno_a",
    )(f4, w_l, m_a)
    ncol = n1 * C
    tc = min(ncol, 4096)
    scale = 1.0 / math.sqrt(S * (C // FNO_GROUPS))
    out = pl.pallas_call(
        functools.partial(_fno_c_kernel, scale=scale),
        grid=(B, ncol // tc),
        in_specs=[pl.BlockSpec((None, 2 * n2, tc), lambda b, t: (b, 0, t)),
                  pl.BlockSpec((n2, 2 * n2), lambda b, t: (0, 0))],
        out_specs=pl.BlockSpec((None, n2, tc // C, C), lambda b, t: (b, 0, t, 0)),
        out_shape=jax.ShapeDtypeStruct((B, n2, n1, C), BF16),
        compiler_params=_cparams(("parallel", "parallel")),
        name="fno_c",
    )(a2, f_c)
    return out.reshape(B * S, C)


def _fno_ctx_kernel(x_ref, w_ref, f_ref, o_ref, *, scale):
    x = x_ref[...].astype(BF16)
    u = jnp.dot(x, w_ref[0], preferred_element_type=F32).astype(BF16)
    v = jnp.dot(x, w_ref[1], preferred_element_type=F32).astype(BF16)
    uv = jnp.concatenate([u, v], axis=0)
    o_ref[...] = (jnp.dot(f_ref[...], uv, preferred_element_type=F32) * scale).astype(BF16)


def _fno_ctx(f, w_l, f_ctx, *, B, S, CTX):
    C = f.shape[1]
    ctx0 = (B * S) // CTX
    scale = 1.0 / math.sqrt(CTX * (C // FNO_GROUPS))
    return pl.pallas_call(
        functools.partial(_fno_ctx_kernel, scale=scale),
        grid=(B,),
        in_specs=[pl.BlockSpec((CTX, C), lambda b: (ctx0 + b, 0)),
                  pl.BlockSpec((2, C, C), lambda b: (0, 0, 0)),
                  pl.BlockSpec((CTX, 2 * CTX), lambda b: (0, 0))],
        out_specs=pl.BlockSpec((CTX, C), lambda b: (b, 0)),
        out_shape=jax.ShapeDtypeStruct((B * CTX, C), BF16),
        compiler_params=_cparams(("parallel",)),
        name="fno_ctx",
    )(f, w_l, f_ctx)


def _dft_constants(S, CTX, C):
    gd = C // FNO_GROUPS
    n1, n2 = FNO_N1, S // FNO_N1
    nj = n2 // SUBLANES
    idx = jnp.arange(gd, dtype=jnp.int32)
    ang = (2.0 * math.pi / gd) * ((idx[:, None] * idx[None, :]) % gd).astype(F32)
    eye_g = jnp.eye(FNO_GROUPS, dtype=F32)
    cs_bd = jnp.stack([jnp.kron(eye_g, jnp.cos(ang)), jnp.kron(eye_g, jnp.sin(ang))])
    k1 = jnp.arange(n1, dtype=jnp.int32)[:, None, None]
    nn1 = jnp.arange(n1, dtype=jnp.int32)[None, :, None]
    nn2 = jnp.arange(n2, dtype=jnp.int32)[None, None, :]
    ph = (2.0 * math.pi / S) * ((k1 * (n2 * nn1 + nn2)) % S).astype(F32)
    cph = jnp.cos(ph).reshape(n1, n1, nj, SUBLANES).transpose(2, 0, 3, 1)
    sph = jnp.sin(ph).reshape(n1, n1, nj, SUBLANES).transpose(2, 0, 3, 1)
    t4 = jnp.stack([jnp.stack([cph, -sph], axis=3), jnp.stack([-sph, -cph], axis=3)], axis=2)
    nr, nq = 2 * SUBLANES * n1, 2 * n1
    expand = (jnp.arange(nq * SUBLANES, dtype=jnp.int32)[None, :] // SUBLANES
              == jnp.arange(nq, dtype=jnp.int32)[:, None]).astype(BF16)
    m_a = jnp.einsum('jrq,qc->jrc', t4.reshape(nj, nr, nq).astype(BF16), expand, preferred_element_type=F32)
    same_s = (jnp.arange(nr, dtype=jnp.int32)[:, None] % SUBLANES
              == jnp.arange(nq * SUBLANES, dtype=jnp.int32)[None, :] % SUBLANES)
    m_a = jnp.where(same_s[None], m_a, 0.0).astype(BF16)
    i2 = jnp.arange(n2, dtype=jnp.int32)
    a2 = (2.0 * math.pi / n2) * ((i2[:, None] * i2[None, :]) % n2).astype(F32)
    f_c = jnp.stack([jnp.cos(a2).reshape(n2, nj, SUBLANES), jnp.sin(a2).reshape(n2, nj, SUBLANES)], axis=2)
    f_c = f_c.reshape(n2, 2 * n2).astype(BF16)
    ic = jnp.arange(CTX, dtype=jnp.int32)
    ac = (2.0 * math.pi / CTX) * ((ic[:, None] * ic[None, :]) % CTX).astype(F32)
    f_ctx = jnp.concatenate([jnp.cos(ac), -jnp.sin(ac)], axis=1).astype(BF16)
    return cs_bd, m_a, f_c, f_ctx


def _block_diag(w):
    G, n = w.shape[-3], w.shape[-1]
    eye = jnp.eye(G, dtype=w.dtype)
    out = jnp.einsum('...gij,gh->...gihj', w, eye)
    return out.reshape(w.shape[:-3] + (G * n, G * n))


def kernel(x, c, ctx, c_ctx, w_ada, b_ada, ln_g, ln_b, ff1_gate, ff1_up, ff1_down, ff2_gate, ff2_up, ff2_down,
           w_in, w_out, na_rpb, lru_conv_w, lru_conv_b, lru_wa, lru_ba, lru_wx, lru_bx, lru_lambda, fno_w):
    B, S, D = x.shape
    CTX = ctx.shape[1]
    L = w_ada.shape[0]
    na_w = na_rpb.shape[1] * HEAD_DIM
    lru_w = lru_conv_w.shape[2]
    alpha = float((2 * L) ** 0.25)
    tm = min(1024, B * CTX)
    tiles_per_batch = S // tm
    nlat = (B * S) // tm
    ntiles = nlat + (B * CTX) // tm
    assert B < 16 and S % tm == 0 and (B * CTX) % tm == 0 and S % (GRID_W * NA_KH) == 0

    cc = jnp.zeros((16, D), F32).at[:B].set(c).at[B].set(c_ctx)
    mods = _ada_params(cc, w_ada, b_ada)

    bf = lambda w: w.astype(BF16)
    ff1 = (bf(ff1_gate), bf(ff1_up), bf(ff1_down))
    ff2 = (bf(ff2_gate), bf(ff2_up), bf(ff2_down))
    w_in_b, w_out_b = bf(w_in), bf(w_out)
    ln_g4 = ln_g[:, :, None, :]
    ln_b4 = ln_b[:, :, None, :]
    wb = _na_bias_tables(na_rpb)
    gate_w = jnp.concatenate([_block_diag(lru_wa), _block_diag(lru_wx)], axis=-1)
    lru_gate_w = bf(jnp.concatenate([gate_w[:, 0], gate_w[:, 1]], axis=-1))
    zrow = jnp.zeros((L, 2, SUBLANES - 3, lru_w), F32)
    lru_vec = jnp.concatenate([lru_ba[:, :, None], lru_bx[:, :, None], lru_lambda[:, :, None], zrow], axis=2)
    lru_cw = jnp.concatenate([lru_conv_w, lru_conv_b[:, None], jnp.zeros((L, SUBLANES - LRU_CONV - 1, lru_w), F32)], axis=1)
    cs_bd, m_a, f_c, f_ctx = _dft_constants(S, CTX, D - na_w - lru_w)
    fno_wcs = _fno_weights(_block_diag(fno_w), cs_bd)

    common = dict(tm=tm, tiles_per_batch=tiles_per_batch, nbatch=B)
    xs = (x.reshape(B * S, D), ctx.reshape(B * CTX, D))
    for l in range(L):
        last = l == L - 1
        x1 = _ffn(xs, mods, *ff1, ln_g4[:, 0], ln_b4[:, 0], j=0, l=l, alpha=alpha, nlat=nlat, ntiles=ntiles, **common)
        q, k, v, xr, gg, f, a_f, b_f, a_b, b_b = _inproj(x1, mods, w_in_b, lru_cw, lru_gate_w, lru_vec,
                                                        l=l, na_w=na_w, lru_w=lru_w, **common)
        na = _attn_lat(q, k, v, wb, l=l, B=B, S=S, CTX=CTX)
        lru, hend = _lru_ctx(xr, gg, lru_cw, lru_gate_w, lru_vec, l=l, B=B, S=S, CTX=CTX)
        tt = min(S, 2048)
        hb = _lru_lat(a_b, b_b, None, None, None, hend, reverse=True, B=B, S=S, tt=tt)
        lru = _lru_lat(a_f, b_f, gg, hb, lru, hend, reverse=False, B=B, S=S, tt=tt)
        fno_lat = _fno_lat(f, fno_wcs[l], m_a, f_c, B=B, S=S)
        if last:
            fno_ctx, nt = None, nlat
        else:
            na = _attn_ctx(q, k, v, na, B=B, S=S, CTX=CTX)
            fno_ctx, nt = _fno_ctx(f, fno_wcs[l], f_ctx, B=B, S=S, CTX=CTX), ntiles
        x3 = _ffn((x1,), mods, *ff2, ln_g4[:, 2], ln_b4[:, 2], j=2, l=l, alpha=alpha, nlat=nlat, ntiles=nt,
                  mix=(na, lru, fno_lat, fno_ctx, w_out_b, ln_g4[:, 1], ln_b4[:, 1]), **common)
        xs = (x3,)
    return xs[0].reshape(B, S, D)
```

```python
import functools
import math

import numpy as np
import jax
import jax.numpy as jnp
from jax import lax
from jax.experimental import pallas as pl
from jax.experimental.pallas import tpu as pltpu

F32 = jnp.float32
BF16 = jnp.bfloat16
HIGHEST = lax.Precision.HIGHEST

HEAD_DIM = 64
GRID_W = 64
NA_KH = 8
NA_KW = 16
LRU_HEADS = 4
LRU_CONV = 4
LRU_C = 8.0
FNO_GROUPS = 4
N_MOD = 9
MACARON = 0.5
LN_EPS = 1e-5
MASK_BIAS = -1e30
LOG2E = math.log2(math.e)
SUBLANES = 8
FNO_N1 = 64
FFN_ROW_BLOCKS = 4
VMEM_LIMIT = 56 * 1024 * 1024


def _cparams(sem):
    return pltpu.CompilerParams(dimension_semantics=sem, vmem_limit_bytes=VMEM_LIMIT)


def _layer_norm(z, g, b):
    mu = jnp.mean(z, axis=-1, keepdims=True)
    zc = z - mu
    var = jnp.mean(zc * zc, axis=-1, keepdims=True)
    return zc * lax.rsqrt(var + LN_EPS) * g + b


def _ada_kernel(c_ref, w_ref, b_ref, o_ref):
    c = c_ref[...]
    a = c * jax.nn.sigmoid(c)
    o_ref[...] = jnp.dot(a, w_ref[...], preferred_element_type=F32, precision=HIGHEST) + b_ref[...]


def _ada_params(cc, w_ada, b_ada):
    L, D, ND = w_ada.shape
    tn = D
    out = pl.pallas_call(
        _ada_kernel,
        grid=(L, ND // tn),
        in_specs=[pl.BlockSpec((16, D), lambda l, n: (0, 0)),
                  pl.BlockSpec((None, D, tn), lambda l, n: (l, 0, n)),
                  pl.BlockSpec((None, 1, tn), lambda l, n: (l, 0, n))],
        out_specs=pl.BlockSpec((None, 16, tn), lambda l, n: (l, 0, n)),
        out_shape=jax.ShapeDtypeStruct((L, 16, ND), F32),
        compiler_params=_cparams(("parallel", "parallel")),
        name="ada",
    )(cc, w_ada, b_ada.reshape(L, 1, ND))
    return out.reshape(L, 16, N_MOD, D)


def _ffn_kernel(*refs, j, tf, alpha, nlat, mode):
    m = refs[2 if mode == "two_x" else 1][...]
    if mode == "two_x":
        xl_ref, xc_ref, _, wg_ref, wu_ref, wd_ref, g_ref, b_ref, o_ref, h_ref, a_ref = refs
        x = jnp.where(pl.program_id(0) < nlat, xl_ref[...], xc_ref[...])
    elif mode == "plain":
        x_ref, _, wg_ref, wu_ref, wd_ref, g_ref, b_ref, o_ref, h_ref, a_ref = refs
        x = x_ref[...]
    else:
        if mode == "mix_ctx":
            (x_ref, _, na_ref, lru_ref, fl_ref, fc_ref, wo_ref, g1_ref, b1_ref,
             wg_ref, wu_ref, wd_ref, g_ref, b_ref, o_ref, h_ref, a_ref) = refs
            fno = jnp.where(pl.program_id(0) < nlat, fl_ref[...], fc_ref[...])
        else:
            (x_ref, _, na_ref, lru_ref, fl_ref, wo_ref, g1_ref, b1_ref,
             wg_ref, wu_ref, wd_ref, g_ref, b_ref, o_ref, h_ref, a_ref) = refs
            fno = fl_ref[...]
        na_w, lru_w = na_ref.shape[1], lru_ref.shape[1]
        x = None
    shift, scale, gate = m[3 * j:3 * j + 1], m[3 * j + 1:3 * j + 2], m[3 * j + 2:3 * j + 3]
    nblk = FFN_ROW_BLOCKS
    rb = o_ref.shape[0] // nblk
    if x is None:
        for blk in range(nblk):
            rows = slice(blk * rb, (blk + 1) * rb)
            y = jnp.dot(na_ref[rows, :], wo_ref[0:na_w, :], preferred_element_type=F32)
            y = y + jnp.dot(lru_ref[rows, :], wo_ref[na_w:na_w + lru_w, :], preferred_element_type=F32)
            y = y + jnp.dot(fno[rows], wo_ref[na_w + lru_w:, :], preferred_element_type=F32)
            x2 = _layer_norm(alpha * x_ref[rows, :] + m[5:6] * y, g1_ref[...], b1_ref[...])
            o_ref[rows, :] = x2
            h_ref[rows, :] = (x2 * (1.0 + scale) + shift).astype(BF16)
    else:
        h_ref[...] = (x * (1.0 + scale) + shift).astype(BF16)
    for c in range(wg_ref.shape[1] // tf):
        cols = slice(c * tf, (c + 1) * tf)
        h = h_ref[...]
        g = jnp.dot(h, wg_ref[:, cols], preferred_element_type=F32)
        u = jnp.dot(h, wu_ref[:, cols], preferred_element_type=F32)
        a_ref[:, cols] = (g * jax.nn.sigmoid(g) * u).astype(BF16)
    xsrc = o_ref if mode.startswith("mix") else (None if mode == "two_x" else x_ref)
    for blk in range(nblk):
        rows = slice(blk * rb, (blk + 1) * rb)
        y = jnp.dot(a_ref[rows, :], wd_ref[...], preferred_element_type=F32)
        xb = x[rows] if xsrc is None else xsrc[rows, :]
        o_ref[rows, :] = _layer_norm(alpha * xb + (MACARON * gate) * y, g_ref[...], b_ref[...])


def _ffn(xs, mods_l, wg, wu, wd, ln_g, ln_b, *, j, l, alpha, tm, tiles_per_batch, nlat, ntiles, nbatch, mix=None):
    two_x = len(xs) == 2
    mode = "two_x" if two_x else "plain"
    D = xs[0].shape[1]
    Fd = wg.shape[2]
    tf = 256
    row_idx = lambda i: jnp.minimum(i // tiles_per_batch, nbatch)
    if two_x:
        nctx = ntiles - nlat
        x_specs = [pl.BlockSpec((tm, D), lambda i: (jnp.minimum(i, nlat - 1), 0)),
                   pl.BlockSpec((tm, D), lambda i: (jnp.clip(i - nlat, 0, nctx - 1), 0))]
    else:
        x_specs = [pl.BlockSpec((tm, D), lambda i: (i, 0))]
    const = pl.Buffered(1)
    mix_specs, mix_args = [], []
    if mix is not None:
        na, lru, fno_l, fno_c, w_out, g1, b1 = mix
        mode = "mix" if fno_c is None else "mix_ctx"
        tile = lambda a: pl.BlockSpec((tm, a.shape[1]), lambda i: (i, 0))
        mix_specs = [tile(na), tile(lru), pl.BlockSpec((tm, fno_l.shape[1]), lambda i: (jnp.minimum(i, nlat - 1), 0))]
        mix_args = [na, lru, fno_l]
        if fno_c is not None:
            mix_specs.append(pl.BlockSpec((tm, fno_c.shape[1]), lambda i: (jnp.clip(i - nlat, 0, ntiles - nlat - 1), 0)))
            mix_args.append(fno_c)
        mix_specs += [pl.BlockSpec((None, D, D), lambda i: (l, 0, 0), pipeline_mode=const),
                      pl.BlockSpec((None, 1, D), lambda i: (l, 0, 0)),
                      pl.BlockSpec((None, 1, D), lambda i: (l, 0, 0))]
        mix_args += [w_out, g1, b1]
    return pl.pallas_call(
        functools.partial(_ffn_kernel, j=j, tf=tf, alpha=alpha, nlat=nlat, mode=mode),
        grid=(ntiles,),
        in_specs=x_specs + [
            pl.BlockSpec((None, None, N_MOD, D), lambda i: (l, row_idx(i), 0, 0))] + mix_specs + [
            pl.BlockSpec((None, D, Fd), lambda i: (l, 0, 0), pipeline_mode=const),
            pl.BlockSpec((None, D, Fd), lambda i: (l, 0, 0), pipeline_mode=const),
            pl.BlockSpec((None, Fd, D), lambda i: (l, 0, 0), pipeline_mode=const),
            pl.BlockSpec((None, 1, D), lambda i: (l, 0, 0)),
            pl.BlockSpec((None, 1, D), lambda i: (l, 0, 0))],
        out_specs=pl.BlockSpec((tm, D), lambda i: (i, 0)),
        out_shape=jax.ShapeDtypeStruct((ntiles * tm, D), F32),
        scratch_shapes=[pltpu.VMEM((tm, D), BF16), pltpu.VMEM((tm, Fd), BF16)],
        compiler_params=_cparams(("parallel",)),
        name=f"ffn{j}",
    )(*xs, mods_l, *mix_args, wg, wu, wd, ln_g, ln_b)


def _inproj_kernel(x_ref, xp_ref, xn_ref, m_ref, w_ref, cw_ref, wg_ref, vec_ref,
                   q_ref, k_ref, v_ref, xr_ref, gg_ref, f_ref, af_ref, bf_ref, ab_ref, bb_ref,
                   h_ref, xl_ref, gates_ref, *, na_w, lru_w, tiles_per_batch):
    m = m_ref[...]
    modulate = lambda xs: (xs * (1.0 + m[4:5]) + m[3:4]).astype(BF16)
    h_ref[...] = modulate(x_ref[...])

    def proj(lo, width, hs=None):
        return jnp.dot(h_ref[...] if hs is None else hs, w_ref[:, lo:lo + width], preferred_element_type=F32)

    xr_lo = 3 * na_w
    xr = proj(xr_lo, lru_w)
    xr_ref[...] = xr
    pos = pl.program_id(0) % tiles_per_batch
    xrp = jnp.where(pos == 0, 0.0, proj(xr_lo, lru_w, modulate(xp_ref[...])))
    xrn = jnp.where(pos == tiles_per_batch - 1, 0.0, proj(xr_lo, lru_w, modulate(xn_ref[...])))
    gr = proj(3 * na_w + lru_w, lru_w)
    q_ref[...] = (proj(0, na_w) * (HEAD_DIM ** -0.5 * LOG2E)).astype(BF16)
    xl_ref[...] = _conv4(xrp, xr, xrn, cw_ref[...])
    gg_ref[...] = jax.nn.gelu(gr).astype(BF16)
    xlb = xl_ref[...].astype(BF16)
    gates = jnp.dot(xlb, wg_ref[:, 0:2 * lru_w], preferred_element_type=F32)
    k_ref[...] = proj(na_w, na_w).astype(BF16)
    a, b = _lru_coeffs(xl_ref[...], gates, vec_ref[0])
    af_ref[...] = a
    bf_ref[...] = b.astype(BF16)
    gates = jnp.dot(xlb, wg_ref[:, 2 * lru_w:4 * lru_w], preferred_element_type=F32)
    v_ref[...] = proj(2 * na_w, na_w).astype(BF16)
    a, b = _lru_coeffs(xl_ref[...], gates, vec_ref[1])
    ab_ref[...] = a
    bb_ref[...] = b.astype(BF16)
    f_ref[...] = proj(3 * na_w + 2 * lru_w, w_ref.shape[1] - 3 * na_w - 2 * lru_w)


def _inproj(x_all, mods_l, w_in, lru_cw, lru_gate_w, lru_vec, *, l, tm, tiles_per_batch, nbatch, na_w, lru_w):
    T, D = x_all.shape
    cols = w_in.shape[2]
    fno_w = cols - 3 * na_w - 2 * lru_w
    row_idx = lambda i: jnp.minimum(i // tiles_per_batch, nbatch)
    nblk8 = T // SUBLANES
    widths = ([(na_w, BF16)] * 3 + [(lru_w, F32), (lru_w, BF16), (fno_w, F32)] + [(lru_w, F32), (lru_w, BF16)] * 2)
    const = pl.Buffered(1)
    return pl.pallas_call(
        functools.partial(_inproj_kernel, na_w=na_w, lru_w=lru_w, tiles_per_batch=tiles_per_batch),
        grid=(T // tm,),
        in_specs=[pl.BlockSpec((tm, D), lambda i: (i, 0)),
                  pl.BlockSpec((SUBLANES, D), lambda i: (jnp.maximum(i * (tm // SUBLANES) - 1, 0), 0)),
                  pl.BlockSpec((SUBLANES, D), lambda i: (jnp.minimum((i + 1) * (tm // SUBLANES), nblk8 - 1), 0)),
                  pl.BlockSpec((None, None, N_MOD, D), lambda i: (l, row_idx(i), 0, 0)),
                  pl.BlockSpec((None, D, cols), lambda i: (l, 0, 0), pipeline_mode=const),
                  pl.BlockSpec((None, SUBLANES, lru_w), lambda i: (l, 0, 0)),
                  pl.BlockSpec((None, lru_w, 4 * lru_w), lambda i: (l, 0, 0), pipeline_mode=const),
                  pl.BlockSpec((None, 2, SUBLANES, lru_w), lambda i: (l, 0, 0, 0))],
        out_specs=[pl.BlockSpec((tm, w), lambda i: (i, 0)) for w, _ in widths],
        out_shape=[jax.ShapeDtypeStruct((T, w), dt) for w, dt in widths],
        scratch_shapes=[pltpu.VMEM((tm, D), BF16), pltpu.VMEM((tm, lru_w), F32), pltpu.VMEM((tm, 4 * lru_w), F32)],
        compiler_params=_cparams(("parallel",)),
        name="inproj",
    )(x_all, x_all, x_all, mods_l, w_in, lru_cw, lru_gate_w, lru_vec)


def _na_bias_tables(rpb):
    L, H = rpb.shape[:2]
    W, ndr, ndc = GRID_W, 2 * NA_KH - 1, 2 * NA_KW - 1
    c = np.arange(W)[:, None]
    j = np.arange(W)[None, :]
    col0 = np.clip(c - NA_KW // 2, 0, W - NA_KW)
    valid = (j >= col0) & (j < col0 + NA_KW)
    entry = np.where(valid, np.clip(j - c + (NA_KW - 1), 0, ndc - 1), ndc)
    onehot = (np.arange(ndc + 1)[:, None, None] == entry[None]).astype(np.float32)
    ext = jnp.concatenate([rpb * LOG2E, jnp.full((L, H, ndr, 1), MASK_BIAS, F32)], axis=-1)
    bt = jnp.einsum('lhde,ecj->lhcdj', ext, onehot, precision=HIGHEST).reshape(L, H, W, ndr * W)
    wb = jnp.stack([bt[..., (NA_KH - 1 - case) * W:(2 * NA_KH - 1 - case) * W] for case in range(NA_KH)], axis=2)
    return wb.reshape(L, H // 2, 2, NA_KH, W, NA_KH * W)


def _masked_heads(q):
    lane = lax.broadcasted_iota(jnp.int32, q.shape, 1)
    qf = q.astype(F32)
    return (jnp.where(lane < HEAD_DIM, qf, 0.0).astype(BF16),
            jnp.where(lane >= HEAD_DIM, qf, 0.0).astype(BF16))


def _qk(q, k):
    return lax.dot_general(q, k, (((1,), (1,)), ((), ())), preferred_element_type=F32)


def _attn_kernel(q_ref, k_ref, v_ref, kc_ref, vc_ref, wb_ref, o_ref, *, R, G, rows):
    jb = pl.program_id(2)
    kc = kc_ref[...]
    vc = vc_ref[...]
    W = GRID_W
    nwin = NA_KH * W
    lane = lax.broadcasted_iota(jnp.int32, (W, 2 * HEAD_DIM), 1)

    def group(g, carry):
        qoff = pl.multiple_of(g * (R * W), R * W)
        qcat = []
        for rr in range(R):
            qe, qo = _masked_heads(q_ref[pl.ds(qoff + rr * W, W), :])
            qcat += [qe, qo]
        s_ctx_all = _qk(jnp.concatenate(qcat, axis=0), kc)
        o_loc, p_ctx, dens = [], [], []
        for rr in range(R):
            r = (jb * G + g) * R + rr
            rs = jnp.clip(r - NA_KH // 2, 0, rows - NA_KH)
            koff = pl.multiple_of(rs * W, W)
            kw = k_ref[pl.ds(koff, nwin), :]
            vw = v_ref[pl.ds(koff, nwin), :]
            qc = jnp.concatenate(qcat[2 * rr:2 * rr + 2], axis=0)
            s_loc = _qk(qc, kw) + jnp.concatenate([wb_ref[0, r - rs], wb_ref[1, r - rs]], axis=0)
            s_ctx = s_ctx_all[rr * 2 * W:(rr + 1) * 2 * W]
            m = jnp.maximum(jnp.max(s_loc, axis=-1, keepdims=True), jnp.max(s_ctx, axis=-1, keepdims=True))
            p_loc = jnp.exp2(s_loc - m)
            pc = jnp.exp2(s_ctx - m)
            dens.append(jnp.sum(p_loc, axis=-1, keepdims=True) + jnp.sum(pc, axis=-1, keepdims=True))
            o_loc.append(jnp.dot(p_loc.astype(BF16), vw, preferred_element_type=F32))
            p_ctx.append(pc.astype(BF16))
        o_ctx_all = jnp.dot(jnp.concatenate(p_ctx, axis=0), vc, preferred_element_type=F32)
        outs = []
        for rr in range(R):
            o = (o_loc[rr] + o_ctx_all[rr * 2 * W:(rr + 1) * 2 * W]) / dens[rr]
            outs.append(jnp.where(lane < HEAD_DIM, o[:W], o[W:]).astype(BF16))
        o_ref[pl.ds(qoff, R * W), :] = jnp.concatenate(outs, axis=0)
        return carry

    lax.fori_loop(0, G, group, 0, unroll=True)


def _attn_lat(q, k, v, wb, *, l, B, S, CTX):
    T, na_w = q.shape
    npair = na_w // (2 * HEAD_DIM)
    rows = S // GRID_W
    R = 8
    G = 8 if rows % (8 * R) == 0 else 1
    nrb = rows // (R * G)
    pw = 2 * HEAD_DIM
    ctx0 = (B * S) // CTX
    qblk = R * G * GRID_W
    return pl.pallas_call(
        functools.partial(_attn_kernel, R=R, G=G, rows=rows),
        grid=(B, npair, nrb),
        in_specs=[pl.BlockSpec((qblk, pw), lambda b, p, j: (b * nrb + j, p)),
                  pl.BlockSpec((S, pw), lambda b, p, j: (b, p)),
                  pl.BlockSpec((S, pw), lambda b, p, j: (b, p)),
                  pl.BlockSpec((CTX, pw), lambda b, p, j: (ctx0 + b, p)),
                  pl.BlockSpec((CTX, pw), lambda b, p, j: (ctx0 + b, p)),
                  pl.BlockSpec((None, None, 2, NA_KH, GRID_W, NA_KH * GRID_W), lambda b, p, j: (l, p, 0, 0, 0, 0))],
        out_specs=pl.BlockSpec((qblk, pw), lambda b, p, j: (b * nrb + j, p)),
        out_shape=jax.ShapeDtypeStruct((T, na_w), BF16),
        compiler_params=_cparams(("parallel", "parallel", "parallel")),
        name="attn_lat",
    )(q, k, v, k, v, wb)


def _attn_ctx_kernel(q_ref, k_ref, v_ref, na_hbm_ref, o_ref):
    del na_hbm_ref
    pw = 2 * HEAD_DIM
    lane = lax.broadcasted_iota(jnp.int32, (o_ref.shape[0], pw), 1)
    pair_outs = []
    for p0 in range(0, o_ref.shape[1], pw):
        k = k_ref[:, p0:p0 + pw]
        v = v_ref[:, p0:p0 + pw]
        outs = []
        for qm in _masked_heads(q_ref[:, p0:p0 + pw]):
            s = _qk(qm, k)
            p = jnp.exp2(s - jnp.max(s, axis=-1, keepdims=True))
            den = jnp.sum(p, axis=-1, keepdims=True)
            outs.append(jnp.dot(p.astype(BF16), v, preferred_element_type=F32) / den)
        pair_outs.append(jnp.where(lane < HEAD_DIM, outs[0], outs[1]).astype(BF16))
    o_ref[...] = jnp.concatenate(pair_outs, axis=1)


def _attn_ctx(q, k, v, na, *, B, S, CTX):
    T, na_w = q.shape
    ctx0 = (B * S) // CTX
    spec = pl.BlockSpec((CTX, na_w), lambda b: (ctx0 + b, 0))
    return pl.pallas_call(
        _attn_ctx_kernel,
        grid=(B,),
        in_specs=[spec, spec, spec, pl.BlockSpec(memory_space=pl.ANY)],
        out_specs=spec,
        out_shape=jax.ShapeDtypeStruct((T, na_w), BF16),
        input_output_aliases={3: 0},
        compiler_params=_cparams(("parallel",)),
        name="attn_ctx",
    )(q, k, v, na)


def _conv4(x_prev, x_main, x_next, cw):
    n = x_main.shape[0]
    xe = jnp.concatenate([x_prev, x_main, x_next], axis=0)
    ne = n + 2 * SUBLANES
    y = cw[2:3] * x_main
    y = y + cw[0:1] * pltpu.roll(xe, 2, 0)[SUBLANES:SUBLANES + n]
    y = y + cw[1:2] * pltpu.roll(xe, 1, 0)[SUBLANES:SUBLANES + n]
    y = y + cw[3:4] * pltpu.roll(xe, ne - 1, 0)[SUBLANES:SUBLANES + n]
    return y + cw[4:5]


def _softplus(x):
    return jnp.maximum(x, 0.0) + jnp.log1p(jnp.exp(-jnp.abs(x)))


def _lru_coeffs(xl, gates, vec):
    C = xl.shape[1]
    r = jax.nn.sigmoid(gates[:, :C] + vec[0:1])
    i = jax.nn.sigmoid(gates[:, C:] + vec[1:2])
    log_a = (-LRU_C * r) * _softplus(-vec[2:3])
    a = jnp.exp(log_a)
    b = jnp.sqrt(-jnp.tanh(log_a) * (1.0 + a * a)) * (i * xl)
    return a, b


def _group_scan(a, b, reverse):
    n, C = a.shape
    a = a.reshape(n // SUBLANES, SUBLANES, C)
    b = b.reshape(n // SUBLANES, SUBLANES, C)
    row = lax.broadcasted_iota(jnp.int32, a.shape, 1)
    for k in (1, 2, 4):
        shift = SUBLANES - k if reverse else k
        mask = (row < SUBLANES - k) if reverse else (row >= k)
        a_s = pltpu.roll(a, shift, 1)
        b_s = pltpu.roll(b, shift, 1)
        b = jnp.where(mask, a * b_s + b, b)
        a = jnp.where(mask, a * a_s, a)
    return a.reshape(n, C), b.reshape(n, C)


def _chunk_scan(a, b, h_in, reverse, acum_ref, bcum_ref, out_ref):
    n, C = a.shape
    ac, bc = _group_scan(a, b, reverse)
    acum_ref[...] = ac
    bcum_ref[...] = bc
    ng = n // SUBLANES

    def body(g, hb):
        idx = (ng - 1 - g) if reverse else g
        sl = pl.ds(pl.multiple_of(idx * SUBLANES, SUBLANES), SUBLANES)
        h = acum_ref[sl, :] * hb + bcum_ref[sl, :]
        out_ref[sl, :] = h
        edge = h[0:1] if reverse else h[SUBLANES - 1:SUBLANES]
        return jnp.broadcast_to(edge, (SUBLANES, C))

    return lax.fori_loop(0, ng, body, h_in, unroll=8)


def _lru_ctx_kernel(xr_ref, gg_ref, cw_ref, w_ref, vec_ref, o_ref, hend_ref, acum_ref, bcum_ref, hf_ref, hb_ref):
    xr = xr_ref[...]
    zeros8 = jnp.zeros((SUBLANES, xr.shape[1]), F32)
    xl = _conv4(zeros8, xr, zeros8, cw_ref[...])
    gates = jnp.dot(xl.astype(BF16), w_ref[...], preferred_element_type=F32)
    half = gates.shape[1] // 2
    a, b = _lru_coeffs(xl, gates[:, :half], vec_ref[0])
    hf_end = _chunk_scan(a, b, zeros8, False, acum_ref, bcum_ref, hf_ref)
    a, b = _lru_coeffs(xl, gates[:, half:], vec_ref[1])
    hb_end = _chunk_scan(a, b, zeros8, True, acum_ref, bcum_ref, hb_ref)
    o_ref[...] = ((hf_ref[...] + hb_ref[...]) * gg_ref[...]).astype(BF16)
    row = lax.broadcasted_iota(jnp.int32, hf_end.shape, 0)
    hend_ref[...] = jnp.where(row == 0, hf_end, hb_end)


def _lru_ctx(xr, gg, cw, w, vec, *, l, B, S, CTX):
    C = xr.shape[1]
    ctx0 = (B * S) // CTX
    spec = pl.BlockSpec((CTX, C), lambda b: (ctx0 + b, 0))
    return pl.pallas_call(
        _lru_ctx_kernel,
        grid=(B,),
        in_specs=[spec, spec,
                  pl.BlockSpec((None, SUBLANES, C), lambda b: (l, 0, 0)),
                  pl.BlockSpec((None, C, 4 * C), lambda b: (l, 0, 0)),
                  pl.BlockSpec((None, 2, SUBLANES, C), lambda b: (l, 0, 0, 0))],
        out_specs=[spec,
                   pl.BlockSpec((None, SUBLANES, C), lambda b: (b, 0, 0))],
        out_shape=[jax.ShapeDtypeStruct((xr.shape[0], C), BF16),
                   jax.ShapeDtypeStruct((B, SUBLANES, C), F32)],
        scratch_shapes=[pltpu.VMEM((CTX, C), F32)] * 4,
        compiler_params=_cparams(("parallel",)),
        name="lru_ctx",
    )(xr, gg, cw, w, vec)


def _lru_lat_kernel(*refs, reverse):
    if reverse:
        a_ref, b_ref, hend_ref, o_ref, carry_ref, acum_ref, bcum_ref, hs_ref = refs
    else:
        a_ref, b_ref, hend_ref, gg_ref, hb_ref, _lru_hbm_ref, o_ref, carry_ref, acum_ref, bcum_ref, hs_ref = refs
    C = a_ref.shape[1]

    @pl.when(pl.program_id(1) == 0)
    def _():
        row = hend_ref[1:2, :] if reverse else hend_ref[0:1, :]
        carry_ref[...] = jnp.broadcast_to(row, (SUBLANES, C))

    b = b_ref[...].astype(F32)
    carry_ref[...] = _chunk_scan(a_ref[...], b, carry_ref[...], reverse, acum_ref, bcum_ref, hs_ref)
    if reverse:
        o_ref[...] = hs_ref[...].astype(BF16)
    else:
        o_ref[...] = ((hs_ref[...] + hb_ref[...].astype(F32)) * gg_ref[...]).astype(BF16)


def _lru_lat(a, b, gg, hb, lru, hend, *, reverse, B, S, tt):
    C = a.shape[1]
    nch = S // tt
    chunk = (lambda i: nch - 1 - i) if reverse else (lambda i: i)
    main = pl.BlockSpec((tt, C), lambda b_, i: (b_ * nch + chunk(i), 0))
    in_specs = [main, main, pl.BlockSpec((None, SUBLANES, C), lambda b_, i: (b_, 0, 0))]
    args = [a, b, hend]
    scratch = [pltpu.VMEM((SUBLANES, C), F32)] + [pltpu.VMEM((tt, C), F32)] * 3
    if reverse:
        out_shape, aliases = jax.ShapeDtypeStruct((B * S, C), BF16), {}
    else:
        in_specs += [main, main, pl.BlockSpec(memory_space=pl.ANY)]
        args += [gg, hb, lru]
        out_shape, aliases = jax.ShapeDtypeStruct(lru.shape, BF16), {len(args) - 1: 0}
    return pl.pallas_call(
        functools.partial(_lru_lat_kernel, reverse=reverse),
        grid=(B, nch),
        in_specs=in_specs,
        out_specs=main,
        out_shape=out_shape,
        input_output_aliases=aliases,
        scratch_shapes=scratch,
        compiler_params=_cparams(("parallel", "arbitrary")),
        name="lru_bwd" if reverse else "lru_fwd",
    )(*args)


def _fno_w_kernel(fw_ref, cs_ref, o_ref):
    fw = fw_ref[...]
    o_ref[0] = jnp.dot(cs_ref[0], fw, preferred_element_type=F32, precision=HIGHEST).astype(BF16)
    o_ref[1] = jnp.dot(cs_ref[1], fw, preferred_element_type=F32, precision=HIGHEST).astype(BF16)


def _fno_weights(fw_bd, cs_bd):
    L, C, _ = fw_bd.shape
    return pl.pallas_call(
        _fno_w_kernel,
        grid=(L,),
        in_specs=[pl.BlockSpec((None, C, C), lambda l: (l, 0, 0)),
                  pl.BlockSpec((2, C, C), lambda l: (0, 0, 0))],
        out_specs=pl.BlockSpec((None, 2, C, C), lambda l: (l, 0, 0, 0)),
        out_shape=jax.ShapeDtypeStruct((L, 2, C, C), BF16),
        compiler_params=_cparams(("parallel",)),
        name="fno_w",
    )(fw_bd, cs_bd)


def _fno_a_kernel(x_ref, w_ref, m_ref, o_ref, *, nb):
    n1 = x_ref.shape[0] // nb
    _, s, C = x_ref.shape
    uvs = []
    for bb in range(nb):
        x = x_ref[bb * n1:(bb + 1) * n1].reshape(n1 * s, C).astype(BF16)
        u = jnp.dot(x, w_ref[0], preferred_element_type=F32).astype(BF16)
        v = jnp.dot(x, w_ref[1], preferred_element_type=F32).astype(BF16)
        uvs.append(jnp.concatenate([u, v], axis=0))
    uv = jnp.concatenate(uvs, axis=1)
    a = jnp.dot(m_ref[...], uv, preferred_element_type=F32).astype(BF16)
    rows = 2 * s
    for bb in range(nb):
        for k1 in range(n1):
            o_ref[bb, :, k1 * C:(k1 + 1) * C] = a[k1 * rows:(k1 + 1) * rows, bb * C:(bb + 1) * C]


def _fno_c_kernel(a_ref, f_ref, o_ref, *, scale):
    res = jnp.dot(f_ref[...], a_ref[...], preferred_element_type=F32) * scale
    o_ref[...] = res.reshape(o_ref.shape).astype(BF16)


def _fno_lat(f, w_l, m_a, f_c, *, B, S):
    T, C = f.shape
    n1 = FNO_N1
    n2 = S // n1
    nj = n2 // SUBLANES
    nb = 4 if B % 4 == 0 else 1
    f4 = f.reshape(T // n2, nj, SUBLANES, C)
    rows_a = 2 * SUBLANES * n1
    a2 = pl.pallas_call(
        functools.partial(_fno_a_kernel, nb=nb),
        grid=(nj, B // nb),
        in_specs=[pl.BlockSpec((nb * n1, None, SUBLANES, C), lambda j, b: (b, j, 0, 0)),
                  pl.BlockSpec((2, C, C), lambda j, b: (0, 0, 0)),
                  pl.BlockSpec((None, rows_a, rows_a), lambda j, b: (j, 0, 0))],
        out_specs=pl.BlockSpec((nb, 2 * SUBLANES, n1 * C), lambda j, b: (b, j, 0)),
        out_shape=jax.ShapeDtypeStruct((B, 2 * n2, n1 * C), BF16),
        compiler_params=_cparams(("parallel", "parallel")),
        name="fno_a",
    )(f4, w_l, m_a)
    ncol = n1 * C
    tc = min(ncol, 4096)
    scale = 1.0 / math.sqrt(S * (C // FNO_GROUPS))
    out = pl.pallas_call(
        functools.partial(_fno_c_kernel, scale=scale),
        grid=(B, ncol // tc),
        in_specs=[pl.BlockSpec((None, 2 * n2, tc), lambda b, t: (b, 0, t)),
                  pl.BlockSpec((n2, 2 * n2), lambda b, t: (0, 0))],
        out_specs=pl.BlockSpec((None, n2, tc // C, C), lambda b, t: (b, 0, t, 0)),
        out_shape=jax.ShapeDtypeStruct((B, n2, n1, C), BF16),
        compiler_params=_cparams(("parallel", "parallel")),
        name="fno_c",
    )(a2, f_c)
    return out.reshape(B * S, C)


def _fno_ctx_kernel(x_ref, w_ref, f_ref, o_ref, *, scale):
    x = x_ref[...].astype(BF16)
    u = jnp.dot(x, w_ref[0], preferred_element_type=F32).astype(BF16)
    v = jnp.dot(x, w_ref[1], preferred_element_type=F32).astype(BF16)
    uv = jnp.concatenate([u, v], axis=0)
    o_ref[...] = (jnp.dot(f_ref[...], uv, preferred_element_type=F32) * scale).astype(BF16)


def _fno_ctx(f, w_l, f_ctx, *, B, S, CTX):
    C = f.shape[1]
    ctx0 = (B * S) // CTX
    scale = 1.0 / math.sqrt(CTX * (C // FNO_GROUPS))
    return pl.pallas_call(
        functools.partial(_fno_ctx_kernel, scale=scale),
        grid=(B,),
        in_specs=[pl.BlockSpec((CTX, C), lambda b: (ctx0 + b, 0)),
                  pl.BlockSpec((2, C, C), lambda b: (0, 0, 0)),
                  pl.BlockSpec((CTX, 2 * CTX), lambda b: (0, 0))],
        out_specs=pl.BlockSpec((CTX, C), lambda b: (b, 0)),
        out_shape=jax.ShapeDtypeStruct((B * CTX, C), BF16),
        compiler_params=_cparams(("parallel",)),
        name="fno_ctx",
    )(f, w_l, f_ctx)


def _dft_constants(S, CTX, C):
    gd = C // FNO_GROUPS
    n1, n2 = FNO_N1, S // FNO_N1
    nj = n2 // SUBLANES
    idx = jnp.arange(gd, dtype=jnp.int32)
    ang = (2.0 * math.pi / gd) * ((idx[:, None] * idx[None, :]) % gd).astype(F32)
    eye_g = jnp.eye(FNO_GROUPS, dtype=F32)
    cs_bd = jnp.stack([jnp.kron(eye_g, jnp.cos(ang)), jnp.kron(eye_g, jnp.sin(ang))])
    k1 = jnp.arange(n1, dtype=jnp.int32)[:, None, None]
    nn1 = jnp.arange(n1, dtype=jnp.int32)[None, :, None]
    nn2 = jnp.arange(n2, dtype=jnp.int32)[None, None, :]
    ph = (2.0 * math.pi / S) * ((k1 * (n2 * nn1 + nn2)) % S).astype(F32)
    cph = jnp.cos(ph).reshape(n1, n1, nj, SUBLANES).transpose(2, 0, 3, 1)
    sph = jnp.sin(ph).reshape(n1, n1, nj, SUBLANES).transpose(2, 0, 3, 1)
    t4 = jnp.stack([jnp.stack([cph, -sph], axis=3), jnp.stack([-sph, -cph], axis=3)], axis=2)
    nr, nq = 2 * SUBLANES * n1, 2 * n1
    expand = (jnp.arange(nq * SUBLANES, dtype=jnp.int32)[None, :] // SUBLANES
              == jnp.arange(nq, dtype=jnp.int32)[:, None]).astype(BF16)
    m_a = jnp.einsum('jrq,qc->jrc', t4.reshape(nj, nr, nq).astype(BF16), expand, preferred_element_type=F32)
    same_s = (jnp.arange(nr, dtype=jnp.int32)[:, None] % SUBLANES
              == jnp.arange(nq * SUBLANES, dtype=jnp.int32)[None, :] % SUBLANES)
    m_a = jnp.where(same_s[None], m_a, 0.0).astype(BF16)
    i2 = jnp.arange(n2, dtype=jnp.int32)
    a2 = (2.0 * math.pi / n2) * ((i2[:, None] * i2[None, :]) % n2).astype(F32)
    f_c = jnp.stack([jnp.cos(a2).reshape(n2, nj, SUBLANES), jnp.sin(a2).reshape(n2, nj, SUBLANES)], axis=2)
    f_c = f_c.reshape(n2, 2 * n2).astype(BF16)
    ic = jnp.arange(CTX, dtype=jnp.int32)
    ac = (2.0 * math.pi / CTX) * ((ic[:, None] * ic[None, :]) % CTX).astype(F32)
    f_ctx = jnp.concatenate([jnp.cos(ac), -jnp.sin(ac)], axis=1).astype(BF16)
    return cs_bd, m_a, f_c, f_ctx


def _block_diag(w):
    G, n = w.shape[-3], w.shape[-1]
    eye = jnp.eye(G, dtype=w.dtype)
    out = jnp.einsum('...gij,gh->...gihj', w, eye)
    return out.reshape(w.shape[:-3] + (G * n, G * n))


def kernel(x, c, ctx, c_ctx, w_ada, b_ada, ln_g, ln_b, ff1_gate, ff1_up, ff1_down, ff2_gate, ff2_up, ff2_down,
           w_in, w_out, na_rpb, lru_conv_w, lru_conv_b, lru_wa, lru_ba, lru_wx, lru_bx, lru_lambda, fno_w):
    B, S, D = x.shape
    CTX = ctx.shape[1]
    L = w_ada.shape[0]
    na_w = na_rpb.shape[1] * HEAD_DIM
    lru_w = lru_conv_w.shape[2]
    alpha = float((2 * L) ** 0.25)
    tm = min(1024, B * CTX)
    tiles_per_batch = S // tm
    nlat = (B * S) // tm
    ntiles = nlat + (B * CTX) // tm
    assert B < 16 and S % tm == 0 and (B * CTX) % tm == 0 and S % (GRID_W * NA_KH) == 0

    cc = jnp.zeros((16, D), F32).at[:B].set(c).at[B].set(c_ctx)
    mods = _ada_params(cc, w_ada, b_ada)

    bf = lambda w: w.astype(BF16)
    ff1 = (bf(ff1_gate), bf(ff1_up), bf(ff1_down))
    ff2 = (bf(ff2_gate), bf(ff2_up), bf(ff2_down))
    w_in_b, w_out_b = bf(w_in), bf(w_out)
    ln_g4 = ln_g[:, :, None, :]
    ln_b4 = ln_b[:, :, None, :]
    wb = _na_bias_tables(na_rpb)
    gate_w = jnp.concatenate([_block_diag(lru_wa), _block_diag(lru_wx)], axis=-1)
    lru_gate_w = bf(jnp.concatenate([gate_w[:, 0], gate_w[:, 1]], axis=-1))
    zrow = jnp.zeros((L, 2, SUBLANES - 3, lru_w), F32)
    lru_vec = jnp.concatenate([lru_ba[:, :, None], lru_bx[:, :, None], lru_lambda[:, :, None], zrow], axis=2)
    lru_cw = jnp.concatenate([lru_conv_w, lru_conv_b[:, None], jnp.zeros((L, SUBLANES - LRU_CONV - 1, lru_w), F32)], axis=1)
    cs_bd, m_a, f_c, f_ctx = _dft_constants(S, CTX, D - na_w - lru_w)
    fno_wcs = _fno_weights(_block_diag(fno_w), cs_bd)

    common = dict(tm=tm, tiles_per_batch=tiles_per_batch, nbatch=B)
    xs = (x.reshape(B * S, D), ctx.reshape(B * CTX, D))
    for l in range(L):
        last = l == L - 1
        x1 = _ffn(xs, mods, *ff1, ln_g4[:, 0], ln_b4[:, 0], j=0, l=l, alpha=alpha, nlat=nlat, ntiles=ntiles, **common)
        q, k, v, xr, gg, f, a_f, b_f, a_b, b_b = _inproj(x1, mods, w_in_b, lru_cw, lru_gate_w, lru_vec,
                                                        l=l, na_w=na_w, lru_w=lru_w, **common)
        na = _attn_lat(q, k, v, wb, l=l, B=B, S=S, CTX=CTX)
        lru, hend = _lru_ctx(xr, gg, lru_cw, lru_gate_w, lru_vec, l=l, B=B, S=S, CTX=CTX)
        tt = min(S, 2048)
        hb = _lru_lat(a_b, b_b, None, None, None, hend, reverse=True, B=B, S=S, tt=tt)
        lru = _lru_lat(a_f, b_f, gg, hb, lru, hend, reverse=False, B=B, S=S, tt=tt)
        fno_lat = _fno_lat(f, fno_wcs[l], m_a, f_c, B=B, S=S)
        if last:
            fno_ctx, nt = None, nlat
        else:
            na = _attn_ctx(q, k, v, na, B=B, S=S, CTX=CTX)
            fno_ctx, nt = _fno_ctx(f, fno_wcs[l], f_ctx, B=B, S=S, CTX=CTX), ntiles
        x3 = _ffn((x1,), mods, *ff2, ln_g4[:, 2], ln_b4[:, 2], j=2, l=l, alpha=alpha, nlat=nlat, ntiles=nt,
                  mix=(na, lru, fno_lat, fno_ctx, w_out_b, ln_g4[:, 1], ln_b4[:, 1]), **common)
        xs = (x3,)
    return xs[0].reshape(B, S, D)
```

```python
import functools
import math

import numpy as np
import jax
import jax.numpy as jnp
from jax import lax
from jax.experimental import pallas as pl
from jax.experimental.pallas import tpu as pltpu

F32 = jnp.float32
BF16 = jnp.bfloat16
HIGHEST = lax.Precision.HIGHEST

HEAD_DIM = 64
GRID_W = 64
NA_KH = 8
NA_KW = 16
LRU_CONV = 4
LRU_C = 8.0
FNO_GROUPS = 4
N_MOD = 9
MACARON = 0.5
LN_EPS = 1e-5
MASK_BIAS = -1e30
LOG2E = math.log2(math.e)
SUBLANES = 8
FNO_N1 = 64

VMEM_LIMIT = 56 * 1024 * 1024
TOKEN_TILE = 1024
FFN_HIDDEN_CHUNK = 256
FFN_ROW_BLOCKS = 4
ATTN_ROWS_PER_GROUP = 8
ATTN_GROUPS_PER_STEP = 8
LRU_CHUNK = 4096
FNO_BATCHES_PER_STEP = 4
FNO_C_COLS = 4096


def _cparams(sem):
    return pltpu.CompilerParams(dimension_semantics=sem, vmem_limit_bytes=VMEM_LIMIT)


def _layer_norm(z, g, b):
    mu = jnp.mean(z, axis=-1, keepdims=True)
    zc = z - mu
    var = jnp.mean(zc * zc, axis=-1, keepdims=True)
    return zc * lax.rsqrt(var + LN_EPS) * g + b


def _ada_kernel(c_ref, w_ref, b_ref, o_ref):
    c = c_ref[...]
    a = c * jax.nn.sigmoid(c)
    o_ref[...] = jnp.dot(a, w_ref[...], preferred_element_type=F32, precision=HIGHEST) + b_ref[...]


def _ada_params(cc, w_ada, b_ada):
    L, D, ND = w_ada.shape
    tn = D
    out = pl.pallas_call(
        _ada_kernel,
        grid=(L, ND // tn),
        in_specs=[pl.BlockSpec((16, D), lambda l, n: (0, 0)),
                  pl.BlockSpec((None, D, tn), lambda l, n: (l, 0, n)),
                  pl.BlockSpec((None, 1, tn), lambda l, n: (l, 0, n))],
        out_specs=pl.BlockSpec((None, 16, tn), lambda l, n: (l, 0, n)),
        out_shape=jax.ShapeDtypeStruct((L, 16, ND), F32),
        compiler_params=_cparams(("parallel", "parallel")),
        name="ada",
    )(cc, w_ada, b_ada.reshape(L, 1, ND))
    return out.reshape(L, 16, N_MOD, D)


def _ffn_kernel(*refs, j, tf, alpha, nlat, mode):
    m = refs[2 if mode == "two_x" else 1][...]
    if mode == "two_x":
        xl_ref, xc_ref, _, wg_ref, wu_ref, wd_ref, g_ref, b_ref, o_ref, h_ref, a_ref = refs
        x = jnp.where(pl.program_id(0) < nlat, xl_ref[...], xc_ref[...])
    elif mode == "plain":
        x_ref, _, wg_ref, wu_ref, wd_ref, g_ref, b_ref, o_ref, h_ref, a_ref = refs
        x = x_ref[...]
    else:
        if mode == "mix_ctx":
            (x_ref, _, na_ref, lru_ref, fl_ref, fc_ref, wo_ref, g1_ref, b1_ref,
             wg_ref, wu_ref, wd_ref, g_ref, b_ref, o_ref, h_ref, a_ref) = refs
            fno = jnp.where(pl.program_id(0) < nlat, fl_ref[...], fc_ref[...])
        else:
            (x_ref, _, na_ref, lru_ref, fl_ref, wo_ref, g1_ref, b1_ref,
             wg_ref, wu_ref, wd_ref, g_ref, b_ref, o_ref, h_ref, a_ref) = refs
            fno = fl_ref[...]
        na_w, lru_w = na_ref.shape[1], lru_ref.shape[1]
        x = None
    shift, scale, gate = m[3 * j:3 * j + 1], m[3 * j + 1:3 * j + 2], m[3 * j + 2:3 * j + 3]
    nblk = FFN_ROW_BLOCKS
    rb = o_ref.shape[0] // nblk
    if x is None:
        for blk in range(nblk):
            rows = slice(blk * rb, (blk + 1) * rb)
            y = jnp.dot(na_ref[rows, :], wo_ref[0:na_w, :], preferred_element_type=F32)
            y = y + jnp.dot(lru_ref[rows, :], wo_ref[na_w:na_w + lru_w, :], preferred_element_type=F32)
            y = y + jnp.dot(fno[rows], wo_ref[na_w + lru_w:, :], preferred_element_type=F32)
            x2 = _layer_norm(alpha * x_ref[rows, :] + m[5:6] * y, g1_ref[...], b1_ref[...])
            o_ref[rows, :] = x2
            h_ref[rows, :] = (x2 * (1.0 + scale) + shift).astype(BF16)
    else:
        h_ref[...] = (x * (1.0 + scale) + shift).astype(BF16)
    for c in range(wg_ref.shape[1] // tf):
        cols = slice(c * tf, (c + 1) * tf)
        h = h_ref[...]
        g = jnp.dot(h, wg_ref[:, cols], preferred_element_type=F32)
        u = jnp.dot(h, wu_ref[:, cols], preferred_element_type=F32)
        a_ref[:, cols] = (g * jax.nn.sigmoid(g) * u).astype(BF16)
    xsrc = o_ref if mode.startswith("mix") else (None if mode == "two_x" else x_ref)
    for blk in range(nblk):
        rows = slice(blk * rb, (blk + 1) * rb)
        y = jnp.dot(a_ref[rows, :], wd_ref[...], preferred_element_type=F32)
        xb = x[rows] if xsrc is None else xsrc[rows, :]
        o_ref[rows, :] = _layer_norm(alpha * xb + (MACARON * gate) * y, g_ref[...], b_ref[...])


def _ffn(xs, mods_l, wg, wu, wd, ln_g, ln_b, *, j, l, alpha, tm, tiles_per_batch, nlat, ntiles, nbatch, mix=None):
    two_x = len(xs) == 2
    mode = "two_x" if two_x else "plain"
    D = xs[0].shape[1]
    Fd = wg.shape[2]
    tf = FFN_HIDDEN_CHUNK
    row_idx = lambda i: jnp.minimum(i // tiles_per_batch, nbatch)
    if two_x:
        nctx = ntiles - nlat
        x_specs = [pl.BlockSpec((tm, D), lambda i: (jnp.minimum(i, nlat - 1), 0)),
                   pl.BlockSpec((tm, D), lambda i: (jnp.clip(i - nlat, 0, nctx - 1), 0))]
    else:
        x_specs = [pl.BlockSpec((tm, D), lambda i: (i, 0))]
    const = pl.Buffered(1)
    mix_specs, mix_args = [], []
    if mix is not None:
        na, lru, fno_l, fno_c, w_out, g1, b1 = mix
        mode = "mix" if fno_c is None else "mix_ctx"
        tile = lambda a: pl.BlockSpec((tm, a.shape[1]), lambda i: (i, 0))
        mix_specs = [tile(na), tile(lru), pl.BlockSpec((tm, fno_l.shape[1]), lambda i: (jnp.minimum(i, nlat - 1), 0))]
        mix_args = [na, lru, fno_l]
        if fno_c is not None:
            mix_specs.append(pl.BlockSpec((tm, fno_c.shape[1]), lambda i: (jnp.clip(i - nlat, 0, ntiles - nlat - 1), 0)))
            mix_args.append(fno_c)
        mix_specs += [pl.BlockSpec((None, D, D), lambda i: (l, 0, 0), pipeline_mode=const),
                      pl.BlockSpec((None, 1, D), lambda i: (l, 0, 0)),
                      pl.BlockSpec((None, 1, D), lambda i: (l, 0, 0))]
        mix_args += [w_out, g1, b1]
    return pl.pallas_call(
        functools.partial(_ffn_kernel, j=j, tf=tf, alpha=alpha, nlat=nlat, mode=mode),
        grid=(ntiles,),
        in_specs=x_specs + [
            pl.BlockSpec((None, None, N_MOD, D), lambda i: (l, row_idx(i), 0, 0))] + mix_specs + [
            pl.BlockSpec((None, D, Fd), lambda i: (l, 0, 0), pipeline_mode=const),
            pl.BlockSpec((None, D, Fd), lambda i: (l, 0, 0), pipeline_mode=const),
            pl.BlockSpec((None, Fd, D), lambda i: (l, 0, 0), pipeline_mode=const),
            pl.BlockSpec((None, 1, D), lambda i: (l, 0, 0)),
            pl.BlockSpec((None, 1, D), lambda i: (l, 0, 0))],
        out_specs=pl.BlockSpec((tm, D), lambda i: (i, 0)),
        out_shape=jax.ShapeDtypeStruct((ntiles * tm, D), F32),
        scratch_shapes=[pltpu.VMEM((tm, D), BF16), pltpu.VMEM((tm, Fd), BF16)],
        compiler_params=_cparams(("parallel",)),
        name=f"ffn{j}",
    )(*xs, mods_l, *mix_args, wg, wu, wd, ln_g, ln_b)


def _inproj_kernel(x_ref, xp_ref, xn_ref, m_ref, w_ref, cw_ref, wg_ref, vec_ref,
                   q_ref, k_ref, v_ref, xr_ref, gg_ref, f_ref, af_ref, bf_ref, ab_ref, bb_ref,
                   h_ref, xl_ref, gates_ref, *, na_w, lru_w, tiles_per_batch):
    m = m_ref[...]
    modulate = lambda xs: (xs * (1.0 + m[4:5]) + m[3:4]).astype(BF16)
    h_ref[...] = modulate(x_ref[...])

    def proj(lo, width, hs=None):
        return jnp.dot(h_ref[...] if hs is None else hs, w_ref[:, lo:lo + width], preferred_element_type=F32)

    xr_lo = 3 * na_w
    xr = proj(xr_lo, lru_w)
    xr_ref[...] = xr
    pos = pl.program_id(0) % tiles_per_batch
    xrp = jnp.where(pos == 0, 0.0, proj(xr_lo, lru_w, modulate(xp_ref[...])))
    xrn = jnp.where(pos == tiles_per_batch - 1, 0.0, proj(xr_lo, lru_w, modulate(xn_ref[...])))
    gr = proj(3 * na_w + lru_w, lru_w)
    q_ref[...] = (proj(0, na_w) * (HEAD_DIM ** -0.5 * LOG2E)).astype(BF16)
    xl_ref[...] = _conv4(xrp, xr, xrn, cw_ref[...])
    gg_ref[...] = jax.nn.gelu(gr).astype(BF16)
    xlb = xl_ref[...].astype(BF16)
    gates = jnp.dot(xlb, wg_ref[:, 0:2 * lru_w], preferred_element_type=F32)
    k_ref[...] = proj(na_w, na_w).astype(BF16)
    a, b = _lru_coeffs(xl_ref[...], gates, vec_ref[0])
    af_ref[...] = a
    bf_ref[...] = b.astype(BF16)
    gates = jnp.dot(xlb, wg_ref[:, 2 * lru_w:4 * lru_w], preferred_element_type=F32)
    v_ref[...] = proj(2 * na_w, na_w).astype(BF16)
    a, b = _lru_coeffs(xl_ref[...], gates, vec_ref[1])
    ab_ref[...] = a
    bb_ref[...] = b.astype(BF16)
    f_ref[...] = proj(3 * na_w + 2 * lru_w, w_ref.shape[1] - 3 * na_w - 2 * lru_w)


def _inproj(x_all, mods_l, w_in, lru_cw, lru_gate_w, lru_vec, *, l, tm, tiles_per_batch, nbatch, na_w, lru_w):
    T, D = x_all.shape
    cols = w_in.shape[2]
    fno_w = cols - 3 * na_w - 2 * lru_w
    row_idx = lambda i: jnp.minimum(i // tiles_per_batch, nbatch)
    nblk8 = T // SUBLANES
    widths = ([(na_w, BF16)] * 3 + [(lru_w, F32), (lru_w, BF16), (fno_w, F32)] + [(lru_w, F32), (lru_w, BF16)] * 2)
    const = pl.Buffered(1)
    return pl.pallas_call(
        functools.partial(_inproj_kernel, na_w=na_w, lru_w=lru_w, tiles_per_batch=tiles_per_batch),
        grid=(T // tm,),
        in_specs=[pl.BlockSpec((tm, D), lambda i: (i, 0)),
                  pl.BlockSpec((SUBLANES, D), lambda i: (jnp.maximum(i * (tm // SUBLANES) - 1, 0), 0)),
                  pl.BlockSpec((SUBLANES, D), lambda i: (jnp.minimum((i + 1) * (tm // SUBLANES), nblk8 - 1), 0)),
                  pl.BlockSpec((None, None, N_MOD, D), lambda i: (l, row_idx(i), 0, 0)),
                  pl.BlockSpec((None, D, cols), lambda i: (l, 0, 0), pipeline_mode=const),
                  pl.BlockSpec((None, SUBLANES, lru_w), lambda i: (l, 0, 0)),
                  pl.BlockSpec((None, lru_w, 4 * lru_w), lambda i: (l, 0, 0), pipeline_mode=const),
                  pl.BlockSpec((None, 2, SUBLANES, lru_w), lambda i: (l, 0, 0, 0))],
        out_specs=[pl.BlockSpec((tm, w), lambda i: (i, 0)) for w, _ in widths],
        out_shape=[jax.ShapeDtypeStruct((T, w), dt) for w, dt in widths],
        scratch_shapes=[pltpu.VMEM((tm, D), BF16), pltpu.VMEM((tm, lru_w), F32), pltpu.VMEM((tm, 4 * lru_w), F32)],
        compiler_params=_cparams(("parallel",)),
        name="inproj",
    )(x_all, x_all, x_all, mods_l, w_in, lru_cw, lru_gate_w, lru_vec)


def _na_bias_tables(rpb):
    L, H = rpb.shape[:2]
    W, ndr, ndc = GRID_W, 2 * NA_KH - 1, 2 * NA_KW - 1
    c = np.arange(W)[:, None]
    j = np.arange(W)[None, :]
    col0 = np.clip(c - NA_KW // 2, 0, W - NA_KW)
    valid = (j >= col0) & (j < col0 + NA_KW)
    entry = np.where(valid, np.clip(j - c + (NA_KW - 1), 0, ndc - 1), ndc)
    onehot = (np.arange(ndc + 1)[:, None, None] == entry[None]).astype(np.float32)
    ext = jnp.concatenate([rpb * LOG2E, jnp.full((L, H, ndr, 1), MASK_BIAS, F32)], axis=-1)
    bt = jnp.einsum('lhde,ecj->lhcdj', ext, onehot, precision=HIGHEST).reshape(L, H, W, ndr * W)
    wb = jnp.stack([bt[..., (NA_KH - 1 - case) * W:(2 * NA_KH - 1 - case) * W] for case in range(NA_KH)], axis=2)
    return wb.reshape(L, H // 2, 2, NA_KH, W, NA_KH * W)


def _masked_heads(q):
    lane = lax.broadcasted_iota(jnp.int32, q.shape, 1)
    qf = q.astype(F32)
    return (jnp.where(lane < HEAD_DIM, qf, 0.0).astype(BF16),
            jnp.where(lane >= HEAD_DIM, qf, 0.0).astype(BF16))


def _qk(q, k):
    return lax.dot_general(q, k, (((1,), (1,)), ((), ())), preferred_element_type=F32)


def _attn_kernel(q_ref, k_ref, v_ref, kc_ref, vc_ref, wb_ref, o_ref, *, R, G, rows):
    jb = pl.program_id(2)
    kc = kc_ref[...]
    vc = vc_ref[...]
    W = GRID_W
    nwin = NA_KH * W
    lane = lax.broadcasted_iota(jnp.int32, (W, 2 * HEAD_DIM), 1)

    def group(g, carry):
        qoff = pl.multiple_of(g * (R * W), R * W)
        qcat = []
        for rr in range(R):
            qe, qo = _masked_heads(q_ref[pl.ds(qoff + rr * W, W), :])
            qcat += [qe, qo]
        s_ctx_all = _qk(jnp.concatenate(qcat, axis=0), kc)
        o_loc, p_ctx, dens = [], [], []
        for rr in range(R):
            r = (jb * G + g) * R + rr
            rs = jnp.clip(r - NA_KH // 2, 0, rows - NA_KH)
            koff = pl.multiple_of(rs * W, W)
            kw = k_ref[pl.ds(koff, nwin), :]
            vw = v_ref[pl.ds(koff, nwin), :]
            qc = jnp.concatenate(qcat[2 * rr:2 * rr + 2], axis=0)
            s_loc = _qk(qc, kw) + jnp.concatenate([wb_ref[0, r - rs], wb_ref[1, r - rs]], axis=0)
            s_ctx = s_ctx_all[rr * 2 * W:(rr + 1) * 2 * W]
            m = jnp.maximum(jnp.max(s_loc, axis=-1, keepdims=True), jnp.max(s_ctx, axis=-1, keepdims=True))
            p_loc = jnp.exp2(s_loc - m)
            pc = jnp.exp2(s_ctx - m)
            dens.append(jnp.sum(p_loc, axis=-1, keepdims=True) + jnp.sum(pc, axis=-1, keepdims=True))
            o_loc.append(jnp.dot(p_loc.astype(BF16), vw, preferred_element_type=F32))
            p_ctx.append(pc.astype(BF16))
        o_ctx_all = jnp.dot(jnp.concatenate(p_ctx, axis=0), vc, preferred_element_type=F32)
        outs = []
        for rr in range(R):
            o = (o_loc[rr] + o_ctx_all[rr * 2 * W:(rr + 1) * 2 * W]) / dens[rr]
            outs.append(jnp.where(lane < HEAD_DIM, o[:W], o[W:]).astype(BF16))
        o_ref[pl.ds(qoff, R * W), :] = jnp.concatenate(outs, axis=0)
        return carry

    lax.fori_loop(0, G, group, 0, unroll=True)


def _attn_lat(q, k, v, wb, *, l, B, S, CTX):
    T, na_w = q.shape
    npair = na_w // (2 * HEAD_DIM)
    rows = S // GRID_W
    R = ATTN_ROWS_PER_GROUP
    G = ATTN_GROUPS_PER_STEP if rows % (ATTN_GROUPS_PER_STEP * R) == 0 else 1
    nrb = rows // (R * G)
    pw = 2 * HEAD_DIM
    ctx0 = (B * S) // CTX
    qblk = R * G * GRID_W
    return pl.pallas_call(
        functools.partial(_attn_kernel, R=R, G=G, rows=rows),
        grid=(B, npair, nrb),
        in_specs=[pl.BlockSpec((qblk, pw), lambda b, p, j: (b * nrb + j, p)),
                  pl.BlockSpec((S, pw), lambda b, p, j: (b, p)),
                  pl.BlockSpec((S, pw), lambda b, p, j: (b, p)),
                  pl.BlockSpec((CTX, pw), lambda b, p, j: (ctx0 + b, p)),
                  pl.BlockSpec((CTX, pw), lambda b, p, j: (ctx0 + b, p)),
                  pl.BlockSpec((None, None, 2, NA_KH, GRID_W, NA_KH * GRID_W), lambda b, p, j: (l, p, 0, 0, 0, 0))],
        out_specs=pl.BlockSpec((qblk, pw), lambda b, p, j: (b * nrb + j, p)),
        out_shape=jax.ShapeDtypeStruct((T, na_w), BF16),
        compiler_params=_cparams(("parallel", "parallel", "parallel")),
        name="attn_lat",
    )(q, k, v, k, v, wb)


def _attn_ctx_kernel(q_ref, k_ref, v_ref, na_hbm_ref, o_ref):
    del na_hbm_ref
    pw = 2 * HEAD_DIM
    lane = lax.broadcasted_iota(jnp.int32, (o_ref.shape[0], pw), 1)
    pair_outs = []
    for p0 in range(0, o_ref.shape[1], pw):
        k = k_ref[:, p0:p0 + pw]
        v = v_ref[:, p0:p0 + pw]
        outs = []
        for qm in _masked_heads(q_ref[:, p0:p0 + pw]):
            s = _qk(qm, k)
            p = jnp.exp2(s - jnp.max(s, axis=-1, keepdims=True))
            den = jnp.sum(p, axis=-1, keepdims=True)
            outs.append(jnp.dot(p.astype(BF16), v, preferred_element_type=F32) / den)
        pair_outs.append(jnp.where(lane < HEAD_DIM, outs[0], outs[1]).astype(BF16))
    o_ref[...] = jnp.concatenate(pair_outs, axis=1)


def _attn_ctx(q, k, v, na, *, B, S, CTX):
    T, na_w = q.shape
    ctx0 = (B * S) // CTX
    spec = pl.BlockSpec((CTX, na_w), lambda b: (ctx0 + b, 0))
    return pl.pallas_call(
        _attn_ctx_kernel,
        grid=(B,),
        in_specs=[spec, spec, spec, pl.BlockSpec(memory_space=pl.ANY)],
        out_specs=spec,
        out_shape=jax.ShapeDtypeStruct((T, na_w), BF16),
        input_output_aliases={3: 0},
        compiler_params=_cparams(("parallel",)),
        name="attn_ctx",
    )(q, k, v, na)


def _conv4(x_prev, x_main, x_next, cw):
    n = x_main.shape[0]
    xe = jnp.concatenate([x_prev, x_main, x_next], axis=0)
    ne = n + 2 * SUBLANES
    y = cw[2:3] * x_main
    y = y + cw[0:1] * pltpu.roll(xe, 2, 0)[SUBLANES:SUBLANES + n]
    y = y + cw[1:2] * pltpu.roll(xe, 1, 0)[SUBLANES:SUBLANES + n]
    y = y + cw[3:4] * pltpu.roll(xe, ne - 1, 0)[SUBLANES:SUBLANES + n]
    return y + cw[4:5]


def _softplus(x):
    return jnp.maximum(x, 0.0) + jnp.log1p(jnp.exp(-jnp.abs(x)))


def _lru_coeffs(xl, gates, vec):
    C = xl.shape[1]
    r = jax.nn.sigmoid(gates[:, :C] + vec[0:1])
    i = jax.nn.sigmoid(gates[:, C:] + vec[1:2])
    log_a = (-LRU_C * r) * _softplus(-vec[2:3])
    a = jnp.exp(log_a)
    b = jnp.sqrt(-jnp.tanh(log_a) * (1.0 + a * a)) * (i * xl)
    return a, b


def _group_scan(a, b, reverse):
    n, C = a.shape
    a = a.reshape(n // SUBLANES, SUBLANES, C)
    b = b.reshape(n // SUBLANES, SUBLANES, C)
    row = lax.broadcasted_iota(jnp.int32, a.shape, 1)
    for k in (1, 2, 4):
        shift = SUBLANES - k if reverse else k
        mask = (row < SUBLANES - k) if reverse else (row >= k)
        a_s = pltpu.roll(a, shift, 1)
        b_s = pltpu.roll(b, shift, 1)
        b = jnp.where(mask, a * b_s + b, b)
        a = jnp.where(mask, a * a_s, a)
    return a.reshape(n, C), b.reshape(n, C)


def _chunk_scan(a, b, h_in, reverse, acum_ref, bcum_ref, out_ref):
    n, C = a.shape
    ac, bc = _group_scan(a, b, reverse)
    acum_ref[...] = ac
    bcum_ref[...] = bc
    ng = n // SUBLANES

    def body(g, hb):
        idx = (ng - 1 - g) if reverse else g
        sl = pl.ds(pl.multiple_of(idx * SUBLANES, SUBLANES), SUBLANES)
        h = acum_ref[sl, :] * hb + bcum_ref[sl, :]
        out_ref[sl, :] = h
        edge = h[0:1] if reverse else h[SUBLANES - 1:SUBLANES]
        return jnp.broadcast_to(edge, (SUBLANES, C))

    return lax.fori_loop(0, ng, body, h_in, unroll=8)


def _lru_ctx_kernel(xr_ref, gg_ref, cw_ref, w_ref, vec_ref, o_ref, hend_ref, acum_ref, bcum_ref, hf_ref, hb_ref):
    xr = xr_ref[...]
    zeros8 = jnp.zeros((SUBLANES, xr.shape[1]), F32)
    xl = _conv4(zeros8, xr, zeros8, cw_ref[...])
    gates = jnp.dot(xl.astype(BF16), w_ref[...], preferred_element_type=F32)
    half = gates.shape[1] // 2
    a, b = _lru_coeffs(xl, gates[:, :half], vec_ref[0])
    hf_end = _chunk_scan(a, b, zeros8, False, acum_ref, bcum_ref, hf_ref)
    a, b = _lru_coeffs(xl, gates[:, half:], vec_ref[1])
    hb_end = _chunk_scan(a, b, zeros8, True, acum_ref, bcum_ref, hb_ref)
    o_ref[...] = ((hf_ref[...] + hb_ref[...]) * gg_ref[...]).astype(BF16)
    row = lax.broadcasted_iota(jnp.int32, hf_end.shape, 0)
    hend_ref[...] = jnp.where(row == 0, hf_end, hb_end)


def _lru_ctx(xr, gg, cw, w, vec, *, l, B, S, CTX):
    C = xr.shape[1]
    ctx0 = (B * S) // CTX
    spec = pl.BlockSpec((CTX, C), lambda b: (ctx0 + b, 0))
    return pl.pallas_call(
        _lru_ctx_kernel,
        grid=(B,),
        in_specs=[spec, spec,
                  pl.BlockSpec((None, SUBLANES, C), lambda b: (l, 0, 0)),
                  pl.BlockSpec((None, C, 4 * C), lambda b: (l, 0, 0)),
                  pl.BlockSpec((None, 2, SUBLANES, C), lambda b: (l, 0, 0, 0))],
        out_specs=[spec,
                   pl.BlockSpec((None, SUBLANES, C), lambda b: (b, 0, 0))],
        out_shape=[jax.ShapeDtypeStruct((xr.shape[0], C), BF16),
                   jax.ShapeDtypeStruct((B, SUBLANES, C), F32)],
        scratch_shapes=[pltpu.VMEM((CTX, C), F32)] * 4,
        compiler_params=_cparams(("parallel",)),
        name="lru_ctx",
    )(xr, gg, cw, w, vec)


def _lru_lat_kernel(*refs, reverse):
    if reverse:
        a_ref, b_ref, hend_ref, o_ref, carry_ref, acum_ref, bcum_ref, hs_ref = refs
    else:
        a_ref, b_ref, hend_ref, gg_ref, hb_ref, _lru_hbm_ref, o_ref, carry_ref, acum_ref, bcum_ref, hs_ref = refs
    C = a_ref.shape[1]

    @pl.when(pl.program_id(1) == 0)
    def _():
        row = hend_ref[1:2, :] if reverse else hend_ref[0:1, :]
        carry_ref[...] = jnp.broadcast_to(row, (SUBLANES, C))

    b = b_ref[...].astype(F32)
    carry_ref[...] = _chunk_scan(a_ref[...], b, carry_ref[...], reverse, acum_ref, bcum_ref, hs_ref)
    if reverse:
        o_ref[...] = hs_ref[...].astype(BF16)
    else:
        o_ref[...] = ((hs_ref[...] + hb_ref[...].astype(F32)) * gg_ref[...]).astype(BF16)


def _lru_lat(a, b, gg, hb, lru, hend, *, reverse, B, S, tt):
    C = a.shape[1]
    nch = S // tt
    chunk = (lambda i: nch - 1 - i) if reverse else (lambda i: i)
    main = pl.BlockSpec((tt, C), lambda b_, i: (b_ * nch + chunk(i), 0))
    in_specs = [main, main, pl.BlockSpec((None, SUBLANES, C), lambda b_, i: (b_, 0, 0))]
    args = [a, b, hend]
    scratch = [pltpu.VMEM((SUBLANES, C), F32)] + [pltpu.VMEM((tt, C), F32)] * 3
    if reverse:
        out_shape, aliases = jax.ShapeDtypeStruct((B * S, C), BF16), {}
    else:
        in_specs += [main, main, pl.BlockSpec(memory_space=pl.ANY)]
        args += [gg, hb, lru]
        out_shape, aliases = jax.ShapeDtypeStruct(lru.shape, BF16), {len(args) - 1: 0}
    return pl.pallas_call(
        functools.partial(_lru_lat_kernel, reverse=reverse),
        grid=(B, nch),
        in_specs=in_specs,
        out_specs=main,
        out_shape=out_shape,
        input_output_aliases=aliases,
        scratch_shapes=scratch,
        compiler_params=_cparams(("parallel", "arbitrary")),
        name="lru_bwd" if reverse else "lru_fwd",
    )(*args)


def _fno_w_kernel(fw_ref, cs_ref, o_ref):
    fw = fw_ref[...]
    o_ref[0] = jnp.dot(cs_ref[0], fw, preferred_element_type=F32, precision=HIGHEST).astype(BF16)
    o_ref[1] = jnp.dot(cs_ref[1], fw, preferred_element_type=F32, precision=HIGHEST).astype(BF16)


def _fno_weights(fw_bd, cs_bd):
    L, C, _ = fw_bd.shape
    return pl.pallas_call(
        _fno_w_kernel,
        grid=(L,),
        in_specs=[pl.BlockSpec((None, C, C), lambda l: (l, 0, 0)),
                  pl.BlockSpec((2, C, C), lambda l: (0, 0, 0))],
        out_specs=pl.BlockSpec((None, 2, C, C), lambda l: (l, 0, 0, 0)),
        out_shape=jax.ShapeDtypeStruct((L, 2, C, C), BF16),
        compiler_params=_cparams(("parallel",)),
        name="fno_w",
    )(fw_bd, cs_bd)


def _fno_a_kernel(x_ref, w_ref, m_ref, o_ref, *, nb):
    n1 = x_ref.shape[0] // nb
    _, s, C = x_ref.shape
    uvs = []
    for bb in range(nb):
        x = x_ref[bb * n1:(bb + 1) * n1].reshape(n1 * s, C).astype(BF16)
        u = jnp.dot(x, w_ref[0], preferred_element_type=F32).astype(BF16)
        v = jnp.dot(x, w_ref[1], preferred_element_type=F32).astype(BF16)
        uvs.append(jnp.concatenate([u, v], axis=0))
    uv = jnp.concatenate(uvs, axis=1)
    a = jnp.dot(m_ref[...], uv, preferred_element_type=F32).astype(BF16)
    rows = 2 * s
    for bb in range(nb):
        for k1 in range(n1):
            o_ref[bb, :, k1 * C:(k1 + 1) * C] = a[k1 * rows:(k1 + 1) * rows, bb * C:(bb + 1) * C]


def _fno_c_kernel(a_ref, f_ref, o_ref, *, scale):
    res = jnp.dot(f_ref[...], a_ref[...], preferred_element_type=F32) * scale
    o_ref[...] = res.reshape(o_ref.shape).astype(BF16)


def _fno_lat(f, w_l, m_a, f_c, *, B, S):
    T, C = f.shape
    n1 = FNO_N1
    n2 = S // n1
    nj = n2 // SUBLANES
    nb = FNO_BATCHES_PER_STEP if B % FNO_BATCHES_PER_STEP == 0 else 1
    f4 = f.reshape(T // n2, nj, SUBLANES, C)
    rows_a = 2 * SUBLANES * n1
    a2 = pl.pallas_call(
        functools.partial(_fno_a_kernel, nb=nb),
        grid=(nj, B // nb),
        in_specs=[pl.BlockSpec((nb * n1, None, SUBLANES, C), lambda j, b: (b, j, 0, 0)),
                  pl.BlockSpec((2, C, C), lambda j, b: (0, 0, 0)),
                  pl.BlockSpec((None, rows_a, rows_a), lambda j, b: (j, 0, 0))],
        out_specs=pl.BlockSpec((nb, 2 * SUBLANES, n1 * C), lambda j, b: (b, j, 0)),
        out_shape=jax.ShapeDtypeStruct((B, 2 * n2, n1 * C), BF16),
        compiler_params=_cparams(("parallel", "parallel")),
        name="fno_a",
    )(f4, w_l, m_a)
    ncol = n1 * C
    tc = min(ncol, FNO_C_COLS)
    scale = 1.0 / math.sqrt(S * (C // FNO_GROUPS))
    out = pl.pallas_call(
        functools.partial(_fno_c_kernel, scale=scale),
        grid=(B, ncol // tc),
        in_specs=[pl.BlockSpec((None, 2 * n2, tc), lambda b, t: (b, 0, t)),
                  pl.BlockSpec((n2, 2 * n2), lambda b, t: (0, 0))],
        out_specs=pl.BlockSpec((None, n2, tc // C, C), lambda b, t: (b, 0, t, 0)),
        out_shape=jax.ShapeDtypeStruct((B, n2, n1, C), BF16),
        compiler_params=_cparams(("parallel", "parallel")),
        name="fno_c",
    )(a2, f_c)
    return out.reshape(B * S, C)


def _fno_ctx_kernel(x_ref, w_ref, f_ref, o_ref, *, scale):
    x = x_ref[...].astype(BF16)
    u = jnp.dot(x, w_ref[0], preferred_element_type=F32).astype(BF16)
    v = jnp.dot(x, w_ref[1], preferred_element_type=F32).astype(BF16)
    uv = jnp.concatenate([u, v], axis=0)
    o_ref[...] = (jnp.dot(f_ref[...], uv, preferred_element_type=F32) * scale).astype(BF16)


def _fno_ctx(f, w_l, f_ctx, *, B, S, CTX):
    C = f.shape[1]
    ctx0 = (B * S) // CTX
    scale = 1.0 / math.sqrt(CTX * (C // FNO_GROUPS))
    return pl.pallas_call(
        functools.partial(_fno_ctx_kernel, scale=scale),
        grid=(B,),
        in_specs=[pl.BlockSpec((CTX, C), lambda b: (ctx0 + b, 0)),
                  pl.BlockSpec((2, C, C), lambda b: (0, 0, 0)),
                  pl.BlockSpec((CTX, 2 * CTX), lambda b: (0, 0))],
        out_specs=pl.BlockSpec((CTX, C), lambda b: (b, 0)),
        out_shape=jax.ShapeDtypeStruct((B * CTX, C), BF16),
        compiler_params=_cparams(("parallel",)),
        name="fno_ctx",
    )(f, w_l, f_ctx)


def _dft_constants(S, CTX, C):
    gd = C // FNO_GROUPS
    n1, n2 = FNO_N1, S // FNO_N1
    nj = n2 // SUBLANES
    idx = jnp.arange(gd, dtype=jnp.int32)
    ang = (2.0 * math.pi / gd) * ((idx[:, None] * idx[None, :]) % gd).astype(F32)
    eye_g = jnp.eye(FNO_GROUPS, dtype=F32)
    cs_bd = jnp.stack([jnp.kron(eye_g, jnp.cos(ang)), jnp.kron(eye_g, jnp.sin(ang))])
    k1 = jnp.arange(n1, dtype=jnp.int32)[:, None, None]
    nn1 = jnp.arange(n1, dtype=jnp.int32)[None, :, None]
    nn2 = jnp.arange(n2, dtype=jnp.int32)[None, None, :]
    ph = (2.0 * math.pi / S) * ((k1 * (n2 * nn1 + nn2)) % S).astype(F32)
    cph = jnp.cos(ph).reshape(n1, n1, nj, SUBLANES).transpose(2, 0, 3, 1)
    sph = jnp.sin(ph).reshape(n1, n1, nj, SUBLANES).transpose(2, 0, 3, 1)
    t4 = jnp.stack([jnp.stack([cph, -sph], axis=3), jnp.stack([-sph, -cph], axis=3)], axis=2)
    nr, nq = 2 * SUBLANES * n1, 2 * n1
    expand = (jnp.arange(nq * SUBLANES, dtype=jnp.int32)[None, :] // SUBLANES
              == jnp.arange(nq, dtype=jnp.int32)[:, None]).astype(BF16)
    m_a = jnp.einsum('jrq,qc->jrc', t4.reshape(nj, nr, nq).astype(BF16), expand, preferred_element_type=F32)
    same_s = (jnp.arange(nr, dtype=jnp.int32)[:, None] % SUBLANES
              == jnp.arange(nq * SUBLANES, dtype=jnp.int32)[None, :] % SUBLANES)
    m_a = jnp.where(same_s[None], m_a, 0.0).astype(BF16)
    i2 = jnp.arange(n2, dtype=jnp.int32)
    a2 = (2.0 * math.pi / n2) * ((i2[:, None] * i2[None, :]) % n2).astype(F32)
    f_c = jnp.stack([jnp.cos(a2).reshape(n2, nj, SUBLANES), jnp.sin(a2).reshape(n2, nj, SUBLANES)], axis=2)
    f_c = f_c.reshape(n2, 2 * n2).astype(BF16)
    ic = jnp.arange(CTX, dtype=jnp.int32)
    ac = (2.0 * math.pi / CTX) * ((ic[:, None] * ic[None, :]) % CTX).astype(F32)
    f_ctx = jnp.concatenate([jnp.cos(ac), -jnp.sin(ac)], axis=1).astype(BF16)
    return cs_bd, m_a, f_c, f_ctx


def _block_diag(w):
    G, n = w.shape[-3], w.shape[-1]
    eye = jnp.eye(G, dtype=w.dtype)
    out = jnp.einsum('...gij,gh->...gihj', w, eye)
    return out.reshape(w.shape[:-3] + (G * n, G * n))


def kernel(x, c, ctx, c_ctx, w_ada, b_ada, ln_g, ln_b, ff1_gate, ff1_up, ff1_down, ff2_gate, ff2_up, ff2_down,
           w_in, w_out, na_rpb, lru_conv_w, lru_conv_b, lru_wa, lru_ba, lru_wx, lru_bx, lru_lambda, fno_w):
    B, S, D = x.shape
    CTX = ctx.shape[1]
    L = w_ada.shape[0]
    na_w = na_rpb.shape[1] * HEAD_DIM
    lru_w = lru_conv_w.shape[2]
    alpha = float((2 * L) ** 0.25)
    tm = min(TOKEN_TILE, B * CTX)
    tiles_per_batch = S // tm
    nlat = (B * S) // tm
    ntiles = nlat + (B * CTX) // tm
    assert B < 16 and S % tm == 0 and (B * CTX) % tm == 0 and S % (GRID_W * NA_KH) == 0

    cc = jnp.zeros((16, D), F32).at[:B].set(c).at[B].set(c_ctx)
    mods = _ada_params(cc, w_ada, b_ada)

    bf = lambda w: w.astype(BF16)
    ff1 = (bf(ff1_gate), bf(ff1_up), bf(ff1_down))
    ff2 = (bf(ff2_gate), bf(ff2_up), bf(ff2_down))
    w_in_b, w_out_b = bf(w_in), bf(w_out)
    ln_g4 = ln_g[:, :, None, :]
    ln_b4 = ln_b[:, :, None, :]
    wb = _na_bias_tables(na_rpb)
    gate_w = jnp.concatenate([_block_diag(lru_wa), _block_diag(lru_wx)], axis=-1)
    lru_gate_w = bf(jnp.concatenate([gate_w[:, 0], gate_w[:, 1]], axis=-1))
    zrow = jnp.zeros((L, 2, SUBLANES - 3, lru_w), F32)
    lru_vec = jnp.concatenate([lru_ba[:, :, None], lru_bx[:, :, None], lru_lambda[:, :, None], zrow], axis=2)
    lru_cw = jnp.concatenate([lru_conv_w, lru_conv_b[:, None], jnp.zeros((L, SUBLANES - LRU_CONV - 1, lru_w), F32)], axis=1)
    cs_bd, m_a, f_c, f_ctx = _dft_constants(S, CTX, D - na_w - lru_w)
    fno_wcs = _fno_weights(_block_diag(fno_w), cs_bd)

    common = dict(tm=tm, tiles_per_batch=tiles_per_batch, nbatch=B)
    xs = (x.reshape(B * S, D), ctx.reshape(B * CTX, D))
    for l in range(L):
        last = l == L - 1
        x1 = _ffn(xs, mods, *ff1, ln_g4[:, 0], ln_b4[:, 0], j=0, l=l, alpha=alpha, nlat=nlat, ntiles=ntiles, **common)
        q, k, v, xr, gg, f, a_f, b_f, a_b, b_b = _inproj(x1, mods, w_in_b, lru_cw, lru_gate_w, lru_vec,
                                                        l=l, na_w=na_w, lru_w=lru_w, **common)
        na = _attn_lat(q, k, v, wb, l=l, B=B, S=S, CTX=CTX)
        lru, hend = _lru_ctx(xr, gg, lru_cw, lru_gate_w, lru_vec, l=l, B=B, S=S, CTX=CTX)
        tt = min(S, LRU_CHUNK)
        hb = _lru_lat(a_b, b_b, None, None, None, hend, reverse=True, B=B, S=S, tt=tt)
        lru = _lru_lat(a_f, b_f, gg, hb, lru, hend, reverse=False, B=B, S=S, tt=tt)
        fno_lat = _fno_lat(f, fno_wcs[l], m_a, f_c, B=B, S=S)
        if last:
            fno_ctx, nt = None, nlat
        else:
            na = _attn_ctx(q, k, v, na, B=B, S=S, CTX=CTX)
            fno_ctx, nt = _fno_ctx(f, fno_wcs[l], f_ctx, B=B, S=S, CTX=CTX), ntiles
        x3 = _ffn((x1,), mods, *ff2, ln_g4[:, 2], ln_b4[:, 2], j=2, l=l, alpha=alpha, nlat=nlat, ntiles=nt,
                  mix=(na, lru, fno_lat, fno_ctx, w_out_b, ln_g4[:, 1], ln_b4[:, 1]), **common)
        xs = (x3,)
    return xs[0].reshape(B, S, D)
```

```python
import functools
import math

import numpy as np
import jax
import jax.numpy as jnp
from jax import lax
from jax.experimental import pallas as pl
from jax.experimental.pallas import tpu as pltpu

F32 = jnp.float32
BF16 = jnp.bfloat16
HIGHEST = lax.Precision.HIGHEST

HEAD_DIM = 64
GRID_W = 64
NA_KH = 8
NA_KW = 16
LRU_CONV = 4
LRU_C = 8.0
FNO_GROUPS = 4
N_MOD = 9
MACARON = 0.5
LN_EPS = 1e-5
MASK_BIAS = -1e30
LOG2E = math.log2(math.e)
SUBLANES = 8
FNO_N1 = 64

VMEM_LIMIT = 56 * 1024 * 1024
TOKEN_TILE = 1024
FFN_HIDDEN_CHUNK = 256
FFN_ROW_BLOCKS = 4
ATTN_ROWS_PER_GROUP = 8
ATTN_GROUPS_PER_STEP = 8
LRU_CHUNK = 4096
FNO_BATCHES_PER_STEP = 4
FNO_C_COLS = 4096


def _cparams(sem):
    return pltpu.CompilerParams(dimension_semantics=sem, vmem_limit_bytes=VMEM_LIMIT)


def _layer_norm(z, g, b):
    mu = jnp.mean(z, axis=-1, keepdims=True)
    zc = z - mu
    var = jnp.mean(zc * zc, axis=-1, keepdims=True)
    return zc * lax.rsqrt(var + LN_EPS) * g + b


def _ada_kernel(c_ref, w_ref, b_ref, o_ref):
    c = c_ref[...]
    a = c * jax.nn.sigmoid(c)
    o_ref[...] = jnp.dot(a, w_ref[...], preferred_element_type=F32, precision=HIGHEST) + b_ref[...]


def _ada_params(cc, w_ada, b_ada):
    L, D, ND = w_ada.shape
    tn = D
    out = pl.pallas_call(
        _ada_kernel,
        grid=(L, ND // tn),
        in_specs=[pl.BlockSpec((16, D), lambda l, n: (0, 0)),
                  pl.BlockSpec((None, D, tn), lambda l, n: (l, 0, n)),
                  pl.BlockSpec((None, 1, tn), lambda l, n: (l, 0, n))],
        out_specs=pl.BlockSpec((None, 16, tn), lambda l, n: (l, 0, n)),
        out_shape=jax.ShapeDtypeStruct((L, 16, ND), F32),
        compiler_params=_cparams(("parallel", "parallel")),
        name="ada",
    )(cc, w_ada, b_ada.reshape(L, 1, ND))
    return out.reshape(L, 16, N_MOD, D)


def _ffn_kernel(*refs, j, tf, alpha, nlat, mode):
    m = refs[2 if mode == "two_x" else 1][...]
    if mode == "two_x":
        xl_ref, xc_ref, _, wg_ref, wu_ref, wd_ref, g_ref, b_ref, o_ref, h_ref, a_ref = refs
        x = jnp.where(pl.program_id(0) < nlat, xl_ref[...], xc_ref[...])
    elif mode == "plain":
        x_ref, _, wg_ref, wu_ref, wd_ref, g_ref, b_ref, o_ref, h_ref, a_ref = refs
        x = x_ref[...]
    else:
        if mode == "mix_ctx":
            (x_ref, _, nal_ref, lrul_ref, fnol_ref, nac_ref, lruc_ref, fnoc_ref, wo_ref, g1_ref, b1_ref,
             wg_ref, wu_ref, wd_ref, g_ref, b_ref, o_ref, h_ref, a_ref) = refs
            is_lat = pl.program_id(0) < nlat
            na = jnp.where(is_lat, nal_ref[...], nac_ref[...])
            lru = jnp.where(is_lat, lrul_ref[...], lruc_ref[...])
            fno = jnp.where(is_lat, fnol_ref[...], fnoc_ref[...])
        else:
            (x_ref, _, nal_ref, lrul_ref, fnol_ref, wo_ref, g1_ref, b1_ref,
             wg_ref, wu_ref, wd_ref, g_ref, b_ref, o_ref, h_ref, a_ref) = refs
            na, lru, fno = nal_ref[...], lrul_ref[...], fnol_ref[...]
        na_w, lru_w = na.shape[1], lru.shape[1]
        x = None
    shift, scale, gate = m[3 * j:3 * j + 1], m[3 * j + 1:3 * j + 2], m[3 * j + 2:3 * j + 3]
    nblk = FFN_ROW_BLOCKS
    rb = o_ref.shape[0] // nblk
    if x is None:
        for blk in range(nblk):
            rows = slice(blk * rb, (blk + 1) * rb)
            y = jnp.dot(na[rows], wo_ref[0:na_w, :], preferred_element_type=F32)
            y = y + jnp.dot(lru[rows], wo_ref[na_w:na_w + lru_w, :], preferred_element_type=F32)
            y = y + jnp.dot(fno[rows], wo_ref[na_w + lru_w:, :], preferred_element_type=F32)
            x2 = _layer_norm(alpha * x_ref[rows, :] + m[5:6] * y, g1_ref[...], b1_ref[...])
            o_ref[rows, :] = x2
            h_ref[rows, :] = (x2 * (1.0 + scale) + shift).astype(BF16)
    else:
        h_ref[...] = (x * (1.0 + scale) + shift).astype(BF16)
    for c in range(wg_ref.shape[1] // tf):
        cols = slice(c * tf, (c + 1) * tf)
        h = h_ref[...]
        g = jnp.dot(h, wg_ref[:, cols], preferred_element_type=F32)
        u = jnp.dot(h, wu_ref[:, cols], preferred_element_type=F32)
        a_ref[:, cols] = (g * jax.nn.sigmoid(g) * u).astype(BF16)
    xsrc = o_ref if mode.startswith("mix") else (None if mode == "two_x" else x_ref)
    for blk in range(nblk):
        rows = slice(blk * rb, (blk + 1) * rb)
        y = jnp.dot(a_ref[rows, :], wd_ref[...], preferred_element_type=F32)
        xb = x[rows] if xsrc is None else xsrc[rows, :]
        o_ref[rows, :] = _layer_norm(alpha * xb + (MACARON * gate) * y, g_ref[...], b_ref[...])


def _ffn(xs, mods_l, wg, wu, wd, ln_g, ln_b, *, j, l, alpha, tm, tiles_per_batch, nlat, ntiles, nbatch, mix=None):
    two_x = len(xs) == 2
    mode = "two_x" if two_x else "plain"
    D = xs[0].shape[1]
    Fd = wg.shape[2]
    tf = FFN_HIDDEN_CHUNK
    row_idx = lambda i: jnp.minimum(i // tiles_per_batch, nbatch)
    if two_x:
        nctx = ntiles - nlat
        x_specs = [pl.BlockSpec((tm, D), lambda i: (jnp.minimum(i, nlat - 1), 0)),
                   pl.BlockSpec((tm, D), lambda i: (jnp.clip(i - nlat, 0, nctx - 1), 0))]
    else:
        x_specs = [pl.BlockSpec((tm, D), lambda i: (i, 0))]
    const = pl.Buffered(1)
    mix_specs, mix_args = [], []
    if mix is not None:
        lat, ctx, w_out, g1, b1 = mix
        mode = "mix" if ctx is None else "mix_ctx"
        mix_specs = [pl.BlockSpec((tm, a.shape[1]), lambda i: (jnp.minimum(i, nlat - 1), 0)) for a in lat]
        mix_args = list(lat)
        if ctx is not None:
            nctx = ntiles - nlat
            mix_specs += [pl.BlockSpec((tm, a.shape[1]), lambda i: (jnp.clip(i - nlat, 0, nctx - 1), 0)) for a in ctx]
            mix_args += list(ctx)
        mix_specs += [pl.BlockSpec((None, D, D), lambda i: (l, 0, 0), pipeline_mode=const),
                      pl.BlockSpec((None, 1, D), lambda i: (l, 0, 0)),
                      pl.BlockSpec((None, 1, D), lambda i: (l, 0, 0))]
        mix_args += [w_out, g1, b1]
    return pl.pallas_call(
        functools.partial(_ffn_kernel, j=j, tf=tf, alpha=alpha, nlat=nlat, mode=mode),
        grid=(ntiles,),
        in_specs=x_specs + [
            pl.BlockSpec((None, None, N_MOD, D), lambda i: (l, row_idx(i), 0, 0))] + mix_specs + [
            pl.BlockSpec((None, D, Fd), lambda i: (l, 0, 0), pipeline_mode=const),
            pl.BlockSpec((None, D, Fd), lambda i: (l, 0, 0), pipeline_mode=const),
            pl.BlockSpec((None, Fd, D), lambda i: (l, 0, 0), pipeline_mode=const),
            pl.BlockSpec((None, 1, D), lambda i: (l, 0, 0)),
            pl.BlockSpec((None, 1, D), lambda i: (l, 0, 0))],
        out_specs=pl.BlockSpec((tm, D), lambda i: (i, 0)),
        out_shape=jax.ShapeDtypeStruct((ntiles * tm, D), F32),
        scratch_shapes=[pltpu.VMEM((tm, D), BF16), pltpu.VMEM((tm, Fd), BF16)],
        compiler_params=_cparams(("parallel",)),
        name=f"ffn{j}",
    )(*xs, mods_l, *mix_args, wg, wu, wd, ln_g, ln_b)


def _inproj_kernel(x_ref, xp_ref, xn_ref, m_ref, w_ref, cw_ref, wg_ref, vec_ref,
                   q_ref, k_ref, v_ref, xr_ref, gg_ref, f_ref, af_ref, bf_ref, ab_ref, bb_ref,
                   h_ref, xl_ref, gates_ref, *, na_w, lru_w, tiles_per_batch):
    m = m_ref[...]
    modulate = lambda xs: (xs * (1.0 + m[4:5]) + m[3:4]).astype(BF16)
    h_ref[...] = modulate(x_ref[...])

    def proj(lo, width, hs=None):
        return jnp.dot(h_ref[...] if hs is None else hs, w_ref[:, lo:lo + width], preferred_element_type=F32)

    xr_lo = 3 * na_w
    xr = proj(xr_lo, lru_w)
    xr_ref[...] = xr
    pos = pl.program_id(0) % tiles_per_batch
    xrp = jnp.where(pos == 0, 0.0, proj(xr_lo, lru_w, modulate(xp_ref[...])))
    xrn = jnp.where(pos == tiles_per_batch - 1, 0.0, proj(xr_lo, lru_w, modulate(xn_ref[...])))
    gr = proj(3 * na_w + lru_w, lru_w)
    q_ref[...] = (proj(0, na_w) * (HEAD_DIM ** -0.5 * LOG2E)).astype(BF16)
    xl_ref[...] = _conv4(xrp, xr, xrn, cw_ref[...])
    gg_ref[...] = jax.nn.gelu(gr).astype(BF16)
    xlb = xl_ref[...].astype(BF16)
    gates = jnp.dot(xlb, wg_ref[:, 0:2 * lru_w], preferred_element_type=F32)
    k_ref[...] = proj(na_w, na_w).astype(BF16)
    a, b = _lru_coeffs(xl_ref[...], gates, vec_ref[0])
    af_ref[...] = a
    bf_ref[...] = b.astype(BF16)
    gates = jnp.dot(xlb, wg_ref[:, 2 * lru_w:4 * lru_w], preferred_element_type=F32)
    v_ref[...] = proj(2 * na_w, na_w).astype(BF16)
    a, b = _lru_coeffs(xl_ref[...], gates, vec_ref[1])
    ab_ref[...] = a
    bb_ref[...] = b.astype(BF16)
    f_ref[...] = proj(3 * na_w + 2 * lru_w, w_ref.shape[1] - 3 * na_w - 2 * lru_w)


def _inproj(x_all, mods_l, w_in, lru_cw, lru_gate_w, lru_vec, *, l, tm, tiles_per_batch, nbatch, na_w, lru_w):
    T, D = x_all.shape
    cols = w_in.shape[2]
    fno_w = cols - 3 * na_w - 2 * lru_w
    row_idx = lambda i: jnp.minimum(i // tiles_per_batch, nbatch)
    nblk8 = T // SUBLANES
    widths = ([(na_w, BF16)] * 3 + [(lru_w, F32), (lru_w, BF16), (fno_w, F32)] + [(lru_w, F32), (lru_w, BF16)] * 2)
    const = pl.Buffered(1)
    return pl.pallas_call(
        functools.partial(_inproj_kernel, na_w=na_w, lru_w=lru_w, tiles_per_batch=tiles_per_batch),
        grid=(T // tm,),
        in_specs=[pl.BlockSpec((tm, D), lambda i: (i, 0)),
                  pl.BlockSpec((SUBLANES, D), lambda i: (jnp.maximum(i * (tm // SUBLANES) - 1, 0), 0)),
                  pl.BlockSpec((SUBLANES, D), lambda i: (jnp.minimum((i + 1) * (tm // SUBLANES), nblk8 - 1), 0)),
                  pl.BlockSpec((None, None, N_MOD, D), lambda i: (l, row_idx(i), 0, 0)),
                  pl.BlockSpec((None, D, cols), lambda i: (l, 0, 0), pipeline_mode=const),
                  pl.BlockSpec((None, SUBLANES, lru_w), lambda i: (l, 0, 0)),
                  pl.BlockSpec((None, lru_w, 4 * lru_w), lambda i: (l, 0, 0), pipeline_mode=const),
                  pl.BlockSpec((None, 2, SUBLANES, lru_w), lambda i: (l, 0, 0, 0))],
        out_specs=[pl.BlockSpec((tm, w), lambda i: (i, 0)) for w, _ in widths],
        out_shape=[jax.ShapeDtypeStruct((T, w), dt) for w, dt in widths],
        scratch_shapes=[pltpu.VMEM((tm, D), BF16), pltpu.VMEM((tm, lru_w), F32), pltpu.VMEM((tm, 4 * lru_w), F32)],
        compiler_params=_cparams(("parallel",)),
        name="inproj",
    )(x_all, x_all, x_all, mods_l, w_in, lru_cw, lru_gate_w, lru_vec)


def _na_bias_tables(rpb):
    L, H = rpb.shape[:2]
    W, ndr, ndc = GRID_W, 2 * NA_KH - 1, 2 * NA_KW - 1
    c = np.arange(W)[:, None]
    j = np.arange(W)[None, :]
    col0 = np.clip(c - NA_KW // 2, 0, W - NA_KW)
    valid = (j >= col0) & (j < col0 + NA_KW)
    entry = np.where(valid, np.clip(j - c + (NA_KW - 1), 0, ndc - 1), ndc)
    onehot = (np.arange(ndc + 1)[:, None, None] == entry[None]).astype(np.float32)
    ext = jnp.concatenate([rpb * LOG2E, jnp.full((L, H, ndr, 1), MASK_BIAS, F32)], axis=-1)
    bt = jnp.einsum('lhde,ecj->lhcdj', ext, onehot, precision=HIGHEST).reshape(L, H, W, ndr * W)
    wb = jnp.stack([bt[..., (NA_KH - 1 - case) * W:(2 * NA_KH - 1 - case) * W] for case in range(NA_KH)], axis=2)
    return wb.reshape(L, H // 2, 2, NA_KH, W, NA_KH * W)


def _masked_heads(q):
    lane = lax.broadcasted_iota(jnp.int32, q.shape, 1)
    qf = q.astype(F32)
    return (jnp.where(lane < HEAD_DIM, qf, 0.0).astype(BF16),
            jnp.where(lane >= HEAD_DIM, qf, 0.0).astype(BF16))


def _qk(q, k):
    return lax.dot_general(q, k, (((1,), (1,)), ((), ())), preferred_element_type=F32)


def _attn_kernel(q_ref, k_ref, v_ref, kc_ref, vc_ref, wb_ref, o_ref, *, R, G, rows):
    jb = pl.program_id(2)
    kc = kc_ref[...]
    vc = vc_ref[...]
    W = GRID_W
    nwin = NA_KH * W
    lane = lax.broadcasted_iota(jnp.int32, (W, 2 * HEAD_DIM), 1)

    def group(g, carry):
        qoff = pl.multiple_of(g * (R * W), R * W)
        qcat = []
        for rr in range(R):
            qe, qo = _masked_heads(q_ref[pl.ds(qoff + rr * W, W), :])
            qcat += [qe, qo]
        s_ctx_all = _qk(jnp.concatenate(qcat, axis=0), kc)
        o_loc, p_ctx, dens = [], [], []
        for rr in range(R):
            r = (jb * G + g) * R + rr
            rs = jnp.clip(r - NA_KH // 2, 0, rows - NA_KH)
            koff = pl.multiple_of(rs * W, W)
            kw = k_ref[pl.ds(koff, nwin), :]
            vw = v_ref[pl.ds(koff, nwin), :]
            qc = jnp.concatenate(qcat[2 * rr:2 * rr + 2], axis=0)
            s_loc = _qk(qc, kw) + jnp.concatenate([wb_ref[0, r - rs], wb_ref[1, r - rs]], axis=0)
            s_ctx = s_ctx_all[rr * 2 * W:(rr + 1) * 2 * W]
            m = jnp.maximum(jnp.max(s_loc, axis=-1, keepdims=True), jnp.max(s_ctx, axis=-1, keepdims=True))
            p_loc = jnp.exp2(s_loc - m)
            pc = jnp.exp2(s_ctx - m)
            dens.append(jnp.sum(p_loc, axis=-1, keepdims=True) + jnp.sum(pc, axis=-1, keepdims=True))
            o_loc.append(jnp.dot(p_loc.astype(BF16), vw, preferred_element_type=F32))
            p_ctx.append(pc.astype(BF16))
        o_ctx_all = jnp.dot(jnp.concatenate(p_ctx, axis=0), vc, preferred_element_type=F32)
        outs = []
        for rr in range(R):
            o = (o_loc[rr] + o_ctx_all[rr * 2 * W:(rr + 1) * 2 * W]) / dens[rr]
            outs.append(jnp.where(lane < HEAD_DIM, o[:W], o[W:]).astype(BF16))
        o_ref[pl.ds(qoff, R * W), :] = jnp.concatenate(outs, axis=0)
        return carry

    lax.fori_loop(0, G, group, 0, unroll=True)


def _attn_lat(q, k, v, wb, *, l, B, S, CTX):
    T, na_w = q.shape
    npair = na_w // (2 * HEAD_DIM)
    rows = S // GRID_W
    R = ATTN_ROWS_PER_GROUP
    G = ATTN_GROUPS_PER_STEP if rows % (ATTN_GROUPS_PER_STEP * R) == 0 else 1
    nrb = rows // (R * G)
    pw = 2 * HEAD_DIM
    ctx0 = (B * S) // CTX
    qblk = R * G * GRID_W
    return pl.pallas_call(
        functools.partial(_attn_kernel, R=R, G=G, rows=rows),
        grid=(B, npair, nrb),
        in_specs=[pl.BlockSpec((qblk, pw), lambda b, p, j: (b * nrb + j, p)),
                  pl.BlockSpec((S, pw), lambda b, p, j: (b, p)),
                  pl.BlockSpec((S, pw), lambda b, p, j: (b, p)),
                  pl.BlockSpec((CTX, pw), lambda b, p, j: (ctx0 + b, p)),
                  pl.BlockSpec((CTX, pw), lambda b, p, j: (ctx0 + b, p)),
                  pl.BlockSpec((None, None, 2, NA_KH, GRID_W, NA_KH * GRID_W), lambda b, p, j: (l, p, 0, 0, 0, 0))],
        out_specs=pl.BlockSpec((qblk, pw), lambda b, p, j: (b * nrb + j, p)),
        out_shape=jax.ShapeDtypeStruct((B * S, na_w), BF16),
        compiler_params=_cparams(("parallel", "parallel", "parallel")),
        name="attn_lat",
    )(q, k, v, k, v, wb)


def _attn_ctx_kernel(q_ref, k_ref, v_ref, o_ref):
    pw = 2 * HEAD_DIM
    lane = lax.broadcasted_iota(jnp.int32, (o_ref.shape[0], pw), 1)
    pair_outs = []
    for p0 in range(0, o_ref.shape[1], pw):
        k = k_ref[:, p0:p0 + pw]
        v = v_ref[:, p0:p0 + pw]
        outs = []
        for qm in _masked_heads(q_ref[:, p0:p0 + pw]):
            s = _qk(qm, k)
            p = jnp.exp2(s - jnp.max(s, axis=-1, keepdims=True))
            den = jnp.sum(p, axis=-1, keepdims=True)
            outs.append(jnp.dot(p.astype(BF16), v, preferred_element_type=F32) / den)
        pair_outs.append(jnp.where(lane < HEAD_DIM, outs[0], outs[1]).astype(BF16))
    o_ref[...] = jnp.concatenate(pair_outs, axis=1)


def _attn_ctx(q, k, v, *, B, S, CTX):
    T, na_w = q.shape
    ctx0 = (B * S) // CTX
    spec = pl.BlockSpec((CTX, na_w), lambda b: (ctx0 + b, 0))
    return pl.pallas_call(
        _attn_ctx_kernel,
        grid=(B,),
        in_specs=[spec, spec, spec],
        out_specs=pl.BlockSpec((CTX, na_w), lambda b: (b, 0)),
        out_shape=jax.ShapeDtypeStruct((B * CTX, na_w), BF16),
        compiler_params=_cparams(("parallel",)),
        name="attn_ctx",
    )(q, k, v)


def _conv4(x_prev, x_main, x_next, cw):
    n = x_main.shape[0]
    xe = jnp.concatenate([x_prev, x_main, x_next], axis=0)
    ne = n + 2 * SUBLANES
    y = cw[2:3] * x_main
    y = y + cw[0:1] * pltpu.roll(xe, 2, 0)[SUBLANES:SUBLANES + n]
    y = y + cw[1:2] * pltpu.roll(xe, 1, 0)[SUBLANES:SUBLANES + n]
    y = y + cw[3:4] * pltpu.roll(xe, ne - 1, 0)[SUBLANES:SUBLANES + n]
    return y + cw[4:5]


def _softplus(x):
    return jnp.maximum(x, 0.0) + jnp.log1p(jnp.exp(-jnp.abs(x)))


def _lru_coeffs(xl, gates, vec):
    C = xl.shape[1]
    r = jax.nn.sigmoid(gates[:, :C] + vec[0:1])
    i = jax.nn.sigmoid(gates[:, C:] + vec[1:2])
    log_a = (-LRU_C * r) * _softplus(-vec[2:3])
    a = jnp.exp(log_a)
    b = jnp.sqrt(-jnp.tanh(log_a) * (1.0 + a * a)) * (i * xl)
    return a, b


def _group_scan(a, b, reverse):
    n, C = a.shape
    a = a.reshape(n // SUBLANES, SUBLANES, C)
    b = b.reshape(n // SUBLANES, SUBLANES, C)
    row = lax.broadcasted_iota(jnp.int32, a.shape, 1)
    for k in (1, 2, 4):
        shift = SUBLANES - k if reverse else k
        mask = (row < SUBLANES - k) if reverse else (row >= k)
        a_s = pltpu.roll(a, shift, 1)
        b_s = pltpu.roll(b, shift, 1)
        b = jnp.where(mask, a * b_s + b, b)
        a = jnp.where(mask, a * a_s, a)
    return a.reshape(n, C), b.reshape(n, C)


def _chunk_scan(a, b, h_in, reverse, acum_ref, bcum_ref, out_ref):
    n, C = a.shape
    ac, bc = _group_scan(a, b, reverse)
    acum_ref[...] = ac
    bcum_ref[...] = bc
    ng = n // SUBLANES

    def body(g, hb):
        idx = (ng - 1 - g) if reverse else g
        sl = pl.ds(pl.multiple_of(idx * SUBLANES, SUBLANES), SUBLANES)
        h = acum_ref[sl, :] * hb + bcum_ref[sl, :]
        out_ref[sl, :] = h
        edge = h[0:1] if reverse else h[SUBLANES - 1:SUBLANES]
        return jnp.broadcast_to(edge, (SUBLANES, C))

    return lax.fori_loop(0, ng, body, h_in, unroll=8)


def _lru_ctx_kernel(xr_ref, gg_ref, cw_ref, w_ref, vec_ref, o_ref, hend_ref, acum_ref, bcum_ref, hf_ref, hb_ref):
    xr = xr_ref[...]
    zeros8 = jnp.zeros((SUBLANES, xr.shape[1]), F32)
    xl = _conv4(zeros8, xr, zeros8, cw_ref[...])
    gates = jnp.dot(xl.astype(BF16), w_ref[...], preferred_element_type=F32)
    half = gates.shape[1] // 2
    a, b = _lru_coeffs(xl, gates[:, :half], vec_ref[0])
    hf_end = _chunk_scan(a, b, zeros8, False, acum_ref, bcum_ref, hf_ref)
    a, b = _lru_coeffs(xl, gates[:, half:], vec_ref[1])
    hb_end = _chunk_scan(a, b, zeros8, True, acum_ref, bcum_ref, hb_ref)
    o_ref[...] = ((hf_ref[...] + hb_ref[...]) * gg_ref[...]).astype(BF16)
    row = lax.broadcasted_iota(jnp.int32, hf_end.shape, 0)
    hend_ref[...] = jnp.where(row == 0, hf_end, hb_end)


def _lru_ctx(xr, gg, cw, w, vec, *, l, B, S, CTX):
    C = xr.shape[1]
    ctx0 = (B * S) // CTX
    spec = pl.BlockSpec((CTX, C), lambda b: (ctx0 + b, 0))
    return pl.pallas_call(
        _lru_ctx_kernel,
        grid=(B,),
        in_specs=[spec, spec,
                  pl.BlockSpec((None, SUBLANES, C), lambda b: (l, 0, 0)),
                  pl.BlockSpec((None, C, 4 * C), lambda b: (l, 0, 0)),
                  pl.BlockSpec((None, 2, SUBLANES, C), lambda b: (l, 0, 0, 0))],
        out_specs=[pl.BlockSpec((CTX, C), lambda b: (b, 0)),
                   pl.BlockSpec((None, SUBLANES, C), lambda b: (b, 0, 0))],
        out_shape=[jax.ShapeDtypeStruct((B * CTX, C), BF16),
                   jax.ShapeDtypeStruct((B, SUBLANES, C), F32)],
        scratch_shapes=[pltpu.VMEM((CTX, C), F32)] * 4,
        compiler_params=_cparams(("parallel",)),
        name="lru_ctx",
    )(xr, gg, cw, w, vec)


def _lru_lat_kernel(*refs, reverse):
    if reverse:
        a_ref, b_ref, hend_ref, o_ref, carry_ref, acum_ref, bcum_ref, hs_ref = refs
    else:
        a_ref, b_ref, hend_ref, gg_ref, hb_ref, o_ref, carry_ref, acum_ref, bcum_ref, hs_ref = refs
    C = a_ref.shape[1]

    @pl.when(pl.program_id(1) == 0)
    def _():
        row = hend_ref[1:2, :] if reverse else hend_ref[0:1, :]
        carry_ref[...] = jnp.broadcast_to(row, (SUBLANES, C))

    b = b_ref[...].astype(F32)
    carry_ref[...] = _chunk_scan(a_ref[...], b, carry_ref[...], reverse, acum_ref, bcum_ref, hs_ref)
    if reverse:
        o_ref[...] = hs_ref[...].astype(BF16)
    else:
        o_ref[...] = ((hs_ref[...] + hb_ref[...].astype(F32)) * gg_ref[...]).astype(BF16)


def _lru_lat(a, b, gg, hb, hend, *, reverse, B, S, tt):
    C = a.shape[1]
    nch = S // tt
    chunk = (lambda i: nch - 1 - i) if reverse else (lambda i: i)
    main = pl.BlockSpec((tt, C), lambda b_, i: (b_ * nch + chunk(i), 0))
    in_specs = [main, main, pl.BlockSpec((None, SUBLANES, C), lambda b_, i: (b_, 0, 0))]
    args = [a, b, hend]
    scratch = [pltpu.VMEM((SUBLANES, C), F32)] + [pltpu.VMEM((tt, C), F32)] * 3
    if not reverse:
        in_specs += [main, main]
        args += [gg, hb]
    return pl.pallas_call(
        functools.partial(_lru_lat_kernel, reverse=reverse),
        grid=(B, nch),
        in_specs=in_specs,
        out_specs=main,
        out_shape=jax.ShapeDtypeStruct((B * S, C), BF16),
        scratch_shapes=scratch,
        compiler_params=_cparams(("parallel", "arbitrary")),
        name="lru_bwd" if reverse else "lru_fwd",
    )(*args)


def _fno_w_kernel(fw_ref, cs_ref, o_ref):
    fw = fw_ref[...]
    o_ref[0] = jnp.dot(cs_ref[0], fw, preferred_element_type=F32, precision=HIGHEST).astype(BF16)
    o_ref[1] = jnp.dot(cs_ref[1], fw, preferred_element_type=F32, precision=HIGHEST).astype(BF16)


def _fno_weights(fw_bd, cs_bd):
    L, C, _ = fw_bd.shape
    return pl.pallas_call(
        _fno_w_kernel,
        grid=(L,),
        in_specs=[pl.BlockSpec((None, C, C), lambda l: (l, 0, 0)),
                  pl.BlockSpec((2, C, C), lambda l: (0, 0, 0))],
        out_specs=pl.BlockSpec((None, 2, C, C), lambda l: (l, 0, 0, 0)),
        out_shape=jax.ShapeDtypeStruct((L, 2, C, C), BF16),
        compiler_params=_cparams(("parallel",)),
        name="fno_w",
    )(fw_bd, cs_bd)


def _fno_a_kernel(x_ref, w_ref, m_ref, o_ref, *, nb):
    n1 = x_ref.shape[0] // nb
    _, s, C = x_ref.shape
    uvs = []
    for bb in range(nb):
        x = x_ref[bb * n1:(bb + 1) * n1].reshape(n1 * s, C).astype(BF16)
        u = jnp.dot(x, w_ref[0], preferred_element_type=F32).astype(BF16)
        v = jnp.dot(x, w_ref[1], preferred_element_type=F32).astype(BF16)
        uvs.append(jnp.concatenate([u, v], axis=0))
    uv = jnp.concatenate(uvs, axis=1)
    a = jnp.dot(m_ref[...], uv, preferred_element_type=F32).astype(BF16)
    rows = 2 * s
    for bb in range(nb):
        for k1 in range(n1):
            o_ref[bb, :, k1 * C:(k1 + 1) * C] = a[k1 * rows:(k1 + 1) * rows, bb * C:(bb + 1) * C]


def _fno_c_kernel(a_ref, f_ref, o_ref, *, scale):
    res = jnp.dot(f_ref[...], a_ref[...], preferred_element_type=F32) * scale
    o_ref[...] = res.reshape(o_ref.shape).astype(BF16)


def _fno_lat(f, w_l, m_a, f_c, *, B, S):
    T, C = f.shape
    n1 = FNO_N1
    n2 = S // n1
    nj = n2 // SUBLANES
    nb = FNO_BATCHES_PER_STEP if B % FNO_BATCHES_PER_STEP == 0 else 1
    f4 = f.reshape(T // n2, nj, SUBLANES, C)
    rows_a = 2 * SUBLANES * n1
    a2 = pl.pallas_call(
        functools.partial(_fno_a_kernel, nb=nb),
        grid=(nj, B // nb),
        in_specs=[pl.BlockSpec((nb * n1, None, SUBLANES, C), lambda j, b: (b, j, 0, 0)),
                  pl.BlockSpec((2, C, C), lambda j, b: (0, 0, 0)),
                  pl.BlockSpec((None, rows_a, rows_a), lambda j, b: (j, 0, 0))],
        out_specs=pl.BlockSpec((nb, 2 * SUBLANES, n1 * C), lambda j, b: (b, j, 0)),
        out_shape=jax.ShapeDtypeStruct((B, 2 * n2, n1 * C), BF16),
        compiler_params=_cparams(("parallel", "parallel")),
        name="fno_a",
    )(f4, w_l, m_a)
    ncol = n1 * C
    tc = min(ncol, FNO_C_COLS)
    scale = 1.0 / math.sqrt(S * (C // FNO_GROUPS))
    out = pl.pallas_call(
        functools.partial(_fno_c_kernel, scale=scale),
        grid=(B, ncol // tc),
        in_specs=[pl.BlockSpec((None, 2 * n2, tc), lambda b, t: (b, 0, t)),
                  pl.BlockSpec((n2, 2 * n2), lambda b, t: (0, 0))],
        out_specs=pl.BlockSpec((None, n2, tc // C, C), lambda b, t: (b, 0, t, 0)),
        out_shape=jax.ShapeDtypeStruct((B, n2, n1, C), BF16),
        compiler_params=_cparams(("parallel", "parallel")),
        name="fno_c",
    )(a2, f_c)
    return out.reshape(B * S, C)


def _fno_ctx_kernel(x_ref, w_ref, f_ref, o_ref, *, scale):
    x = x_ref[...].astype(BF16)
    u = jnp.dot(x, w_ref[0], preferred_element_type=F32).astype(BF16)
    v = jnp.dot(x, w_ref[1], preferred_element_type=F32).astype(BF16)
    uv = jnp.concatenate([u, v], axis=0)
    o_ref[...] = (jnp.dot(f_ref[...], uv, preferred_element_type=F32) * scale).astype(BF16)


def _fno_ctx(f, w_l, f_ctx, *, B, S, CTX):
    C = f.shape[1]
    ctx0 = (B * S) // CTX
    scale = 1.0 / math.sqrt(CTX * (C // FNO_GROUPS))
    return pl.pallas_call(
        functools.partial(_fno_ctx_kernel, scale=scale),
        grid=(B,),
        in_specs=[pl.BlockSpec((CTX, C), lambda b: (ctx0 + b, 0)),
                  pl.BlockSpec((2, C, C), lambda b: (0, 0, 0)),
                  pl.BlockSpec((CTX, 2 * CTX), lambda b: (0, 0))],
        out_specs=pl.BlockSpec((CTX, C), lambda b: (b, 0)),
        out_shape=jax.ShapeDtypeStruct((B * CTX, C), BF16),
        compiler_params=_cparams(("parallel",)),
        name="fno_ctx",
    )(f, w_l, f_ctx)


def _dft_constants(S, CTX, C):
    gd = C // FNO_GROUPS
    n1, n2 = FNO_N1, S // FNO_N1
    nj = n2 // SUBLANES
    idx = jnp.arange(gd, dtype=jnp.int32)
    ang = (2.0 * math.pi / gd) * ((idx[:, None] * idx[None, :]) % gd).astype(F32)
    eye_g = jnp.eye(FNO_GROUPS, dtype=F32)
    cs_bd = jnp.stack([jnp.kron(eye_g, jnp.cos(ang)), jnp.kron(eye_g, jnp.sin(ang))])
    k1 = jnp.arange(n1, dtype=jnp.int32)[:, None, None]
    nn1 = jnp.arange(n1, dtype=jnp.int32)[None, :, None]
    nn2 = jnp.arange(n2, dtype=jnp.int32)[None, None, :]
    ph = (2.0 * math.pi / S) * ((k1 * (n2 * nn1 + nn2)) % S).astype(F32)
    cph = jnp.cos(ph).reshape(n1, n1, nj, SUBLANES).transpose(2, 0, 3, 1)
    sph = jnp.sin(ph).reshape(n1, n1, nj, SUBLANES).transpose(2, 0, 3, 1)
    t4 = jnp.stack([jnp.stack([cph, -sph], axis=3), jnp.stack([-sph, -cph], axis=3)], axis=2)
    nr, nq = 2 * SUBLANES * n1, 2 * n1
    expand = (jnp.arange(nq * SUBLANES, dtype=jnp.int32)[None, :] // SUBLANES
              == jnp.arange(nq, dtype=jnp.int32)[:, None]).astype(BF16)
    m_a = jnp.einsum('jrq,qc->jrc', t4.reshape(nj, nr, nq).astype(BF16), expand, preferred_element_type=F32)
    same_s = (jnp.arange(nr, dtype=jnp.int32)[:, None] % SUBLANES
              == jnp.arange(nq * SUBLANES, dtype=jnp.int32)[None, :] % SUBLANES)
    m_a = jnp.where(same_s[None], m_a, 0.0).astype(BF16)
    i2 = jnp.arange(n2, dtype=jnp.int32)
    a2 = (2.0 * math.pi / n2) * ((i2[:, None] * i2[None, :]) % n2).astype(F32)
    f_c = jnp.stack([jnp.cos(a2).reshape(n2, nj, SUBLANES), jnp.sin(a2).reshape(n2, nj, SUBLANES)], axis=2)
    f_c = f_c.reshape(n2, 2 * n2).astype(BF16)
    ic = jnp.arange(CTX, dtype=jnp.int32)
    ac = (2.0 * math.pi / CTX) * ((ic[:, None] * ic[None, :]) % CTX).astype(F32)
    f_ctx = jnp.concatenate([jnp.cos(ac), -jnp.sin(ac)], axis=1).astype(BF16)
    return cs_bd, m_a, f_c, f_ctx


def _block_diag(w):
    G, n = w.shape[-3], w.shape[-1]
    eye = jnp.eye(G, dtype=w.dtype)
    out = jnp.einsum('...gij,gh->...gihj', w, eye)
    return out.reshape(w.shape[:-3] + (G * n, G * n))


def kernel(x, c, ctx, c_ctx, w_ada, b_ada, ln_g, ln_b, ff1_gate, ff1_up, ff1_down, ff2_gate, ff2_up, ff2_down,
           w_in, w_out, na_rpb, lru_conv_w, lru_conv_b, lru_wa, lru_ba, lru_wx, lru_bx, lru_lambda, fno_w):
    B, S, D = x.shape
    CTX = ctx.shape[1]
    L = w_ada.shape[0]
    na_w = na_rpb.shape[1] * HEAD_DIM
    lru_w = lru_conv_w.shape[2]
    alpha = float((2 * L) ** 0.25)
    tm = min(TOKEN_TILE, B * CTX)
    tiles_per_batch = S // tm
    nlat = (B * S) // tm
    ntiles = nlat + (B * CTX) // tm
    assert B < 16 and S % tm == 0 and (B * CTX) % tm == 0 and S % (GRID_W * NA_KH) == 0

    cc = jnp.zeros((16, D), F32).at[:B].set(c).at[B].set(c_ctx)
    mods = _ada_params(cc, w_ada, b_ada)

    bf = lambda w: w.astype(BF16)
    ff1 = (bf(ff1_gate), bf(ff1_up), bf(ff1_down))
    ff2 = (bf(ff2_gate), bf(ff2_up), bf(ff2_down))
    w_in_b, w_out_b = bf(w_in), bf(w_out)
    ln_g4 = ln_g[:, :, None, :]
    ln_b4 = ln_b[:, :, None, :]
    wb = _na_bias_tables(na_rpb)
    gate_w = jnp.concatenate([_block_diag(lru_wa), _block_diag(lru_wx)], axis=-1)
    lru_gate_w = bf(jnp.concatenate([gate_w[:, 0], gate_w[:, 1]], axis=-1))
    zrow = jnp.zeros((L, 2, SUBLANES - 3, lru_w), F32)
    lru_vec = jnp.concatenate([lru_ba[:, :, None], lru_bx[:, :, None], lru_lambda[:, :, None], zrow], axis=2)
    lru_cw = jnp.concatenate([lru_conv_w, lru_conv_b[:, None], jnp.zeros((L, SUBLANES - LRU_CONV - 1, lru_w), F32)], axis=1)
    cs_bd, m_a, f_c, f_ctx = _dft_constants(S, CTX, D - na_w - lru_w)
    fno_wcs = _fno_weights(_block_diag(fno_w), cs_bd)

    common = dict(tm=tm, tiles_per_batch=tiles_per_batch, nbatch=B)
    xs = (x.reshape(B * S, D), ctx.reshape(B * CTX, D))
    for l in range(L):
        last = l == L - 1
        x1 = _ffn(xs, mods, *ff1, ln_g4[:, 0], ln_b4[:, 0], j=0, l=l, alpha=alpha, nlat=nlat, ntiles=ntiles, **common)
        q, k, v, xr, gg, f, a_f, b_f, a_b, b_b = _inproj(x1, mods, w_in_b, lru_cw, lru_gate_w, lru_vec,
                                                        l=l, na_w=na_w, lru_w=lru_w, **common)
        na_lat = _attn_lat(q, k, v, wb, l=l, B=B, S=S, CTX=CTX)
        lru_ctx, hend = _lru_ctx(xr, gg, lru_cw, lru_gate_w, lru_vec, l=l, B=B, S=S, CTX=CTX)
        tt = min(S, LRU_CHUNK)
        hb = _lru_lat(a_b, b_b, None, None, hend, reverse=True, B=B, S=S, tt=tt)
        lru_lat = _lru_lat(a_f, b_f, gg, hb, hend, reverse=False, B=B, S=S, tt=tt)
        fno_lat = _fno_lat(f, fno_wcs[l], m_a, f_c, B=B, S=S)
        if last:
            ctx_parts, nt = None, nlat
        else:
            ctx_parts = (_attn_ctx(q, k, v, B=B, S=S, CTX=CTX), lru_ctx, _fno_ctx(f, fno_wcs[l], f_ctx, B=B, S=S, CTX=CTX))
            nt = ntiles
        x3 = _ffn((x1,), mods, *ff2, ln_g4[:, 2], ln_b4[:, 2], j=2, l=l, alpha=alpha, nlat=nlat, ntiles=nt,
                  mix=((na_lat, lru_lat, fno_lat), ctx_parts, w_out_b, ln_g4[:, 1], ln_b4[:, 1]), **common)
        xs = (x3,)
    return xs[0].reshape(B, S, D)
```

```python
import functools
import math

import numpy as np
import jax
import jax.numpy as jnp
from jax import lax
from jax.experimental import pallas as pl
from jax.experimental.pallas import tpu as pltpu

F32 = jnp.float32
BF16 = jnp.bfloat16
HIGHEST = lax.Precision.HIGHEST

HEAD_DIM = 64
GRID_W = 64
NA_KH = 8
NA_KW = 16
LRU_CONV = 4
LRU_C = 8.0
FNO_GROUPS = 4
N_MOD = 9
MACARON = 0.5
LN_EPS = 1e-5
MASK_BIAS = -1e30
LOG2E = math.log2(math.e)
SUBLANES = 8
FNO_N1 = 64

VMEM_LIMIT = 56 * 1024 * 1024
TOKEN_TILE = 1024
FFN_HIDDEN_CHUNK = 256
FFN_ROW_BLOCKS = 4
ATTN_ROWS_PER_GROUP = 8
ATTN_GROUPS_PER_STEP = 8
LRU_CHUNK = 2048
FNO_BATCHES_PER_STEP = 4
FNO_C_COLS = 4096


def _cparams(sem):
    return pltpu.CompilerParams(dimension_semantics=sem, vmem_limit_bytes=VMEM_LIMIT)


def _layer_norm(z, g, b):
    mu = jnp.mean(z, axis=-1, keepdims=True)
    zc = z - mu
    var = jnp.mean(zc * zc, axis=-1, keepdims=True)
    return zc * lax.rsqrt(var + LN_EPS) * g + b


def _ada_kernel(c_ref, w_ref, b_ref, o_ref):
    c = c_ref[...]
    a = c * jax.nn.sigmoid(c)
    o_ref[...] = jnp.dot(a, w_ref[...], preferred_element_type=F32, precision=HIGHEST) + b_ref[...]


def _ada_params(cc, w_ada, b_ada):
    L, D, ND = w_ada.shape
    tn = D
    out = pl.pallas_call(
        _ada_kernel,
        grid=(L, ND // tn),
        in_specs=[pl.BlockSpec((16, D), lambda l, n: (0, 0)),
                  pl.BlockSpec((None, D, tn), lambda l, n: (l, 0, n)),
                  pl.BlockSpec((None, 1, tn), lambda l, n: (l, 0, n))],
        out_specs=pl.BlockSpec((None, 16, tn), lambda l, n: (l, 0, n)),
        out_shape=jax.ShapeDtypeStruct((L, 16, ND), F32),
        compiler_params=_cparams(("parallel", "parallel")),
        name="ada",
    )(cc, w_ada, b_ada.reshape(L, 1, ND))
    return out.reshape(L, 16, N_MOD, D)


def _ffn_kernel(*refs, j, tf, alpha, nlat, mode):
    m = refs[2 if mode == "two_x" else 1][...]
    if mode == "two_x":
        xl_ref, xc_ref, _, wg_ref, wu_ref, wd_ref, g_ref, b_ref, o_ref, h_ref, a_ref = refs
        x = jnp.where(pl.program_id(0) < nlat, xl_ref[...], xc_ref[...])
    elif mode == "plain":
        x_ref, _, wg_ref, wu_ref, wd_ref, g_ref, b_ref, o_ref, h_ref, a_ref = refs
        x = x_ref[...]
    else:
        if mode == "mix_ctx":
            (x_ref, _, nal_ref, lrul_ref, fnol_ref, nac_ref, lruc_ref, fnoc_ref, wo_ref, g1_ref, b1_ref,
             wg_ref, wu_ref, wd_ref, g_ref, b_ref, o_ref, h_ref, a_ref) = refs
            is_lat = pl.program_id(0) < nlat
            na = jnp.where(is_lat, nal_ref[...], nac_ref[...])
            lru = jnp.where(is_lat, lrul_ref[...], lruc_ref[...])
            fno = jnp.where(is_lat, fnol_ref[...], fnoc_ref[...])
        else:
            (x_ref, _, nal_ref, lrul_ref, fnol_ref, wo_ref, g1_ref, b1_ref,
             wg_ref, wu_ref, wd_ref, g_ref, b_ref, o_ref, h_ref, a_ref) = refs
            na, lru, fno = nal_ref[...], lrul_ref[...], fnol_ref[...]
        na_w, lru_w = na.shape[1], lru.shape[1]
        x = None
    shift, scale, gate = m[3 * j:3 * j + 1], m[3 * j + 1:3 * j + 2], m[3 * j + 2:3 * j + 3]
    nblk = FFN_ROW_BLOCKS
    rb = o_ref.shape[0] // nblk
    if x is None:
        for blk in range(nblk):
            rows = slice(blk * rb, (blk + 1) * rb)
            y = jnp.dot(na[rows], wo_ref[0:na_w, :], preferred_element_type=F32)
            y = y + jnp.dot(lru[rows], wo_ref[na_w:na_w + lru_w, :], preferred_element_type=F32)
            y = y + jnp.dot(fno[rows], wo_ref[na_w + lru_w:, :], preferred_element_type=F32)
            x2 = _layer_norm(alpha * x_ref[rows, :] + m[5:6] * y, g1_ref[...], b1_ref[...])
            o_ref[rows, :] = x2
            h_ref[rows, :] = (x2 * (1.0 + scale) + shift).astype(BF16)
    else:
        h_ref[...] = (x * (1.0 + scale) + shift).astype(BF16)
    for c in range(wg_ref.shape[1] // tf):
        cols = slice(c * tf, (c + 1) * tf)
        h = h_ref[...]
        g = jnp.dot(h, wg_ref[:, cols], preferred_element_type=F32)
        u = jnp.dot(h, wu_ref[:, cols], preferred_element_type=F32)
        a_ref[:, cols] = (g * jax.nn.sigmoid(g) * u).astype(BF16)
    xsrc = o_ref if mode.startswith("mix") else (None if mode == "two_x" else x_ref)
    for blk in range(nblk):
        rows = slice(blk * rb, (blk + 1) * rb)
        y = jnp.dot(a_ref[rows, :], wd_ref[...], preferred_element_type=F32)
        xb = x[rows] if xsrc is None else xsrc[rows, :]
        o_ref[rows, :] = _layer_norm(alpha * xb + (MACARON * gate) * y, g_ref[...], b_ref[...])


def _ffn(xs, mods_l, wg, wu, wd, ln_g, ln_b, *, j, l, alpha, tm, tiles_per_batch, nlat, ntiles, nbatch, mix=None):
    two_x = len(xs) == 2
    mode = "two_x" if two_x else "plain"
    D = xs[0].shape[1]
    Fd = wg.shape[2]
    tf = FFN_HIDDEN_CHUNK
    row_idx = lambda i: jnp.minimum(i // tiles_per_batch, nbatch)
    if two_x:
        nctx = ntiles - nlat
        x_specs = [pl.BlockSpec((tm, D), lambda i: (jnp.minimum(i, nlat - 1), 0)),
                   pl.BlockSpec((tm, D), lambda i: (jnp.clip(i - nlat, 0, nctx - 1), 0))]
    else:
        x_specs = [pl.BlockSpec((tm, D), lambda i: (i, 0))]
    const = pl.Buffered(1)
    mix_specs, mix_args = [], []
    if mix is not None:
        lat, ctx, w_out, g1, b1 = mix
        mode = "mix" if ctx is None else "mix_ctx"
        mix_specs = [pl.BlockSpec((tm, a.shape[1]), lambda i: (jnp.minimum(i, nlat - 1), 0)) for a in lat]
        mix_args = list(lat)
        if ctx is not None:
            nctx = ntiles - nlat
            mix_specs += [pl.BlockSpec((tm, a.shape[1]), lambda i: (jnp.clip(i - nlat, 0, nctx - 1), 0)) for a in ctx]
            mix_args += list(ctx)
        mix_specs += [pl.BlockSpec((None, D, D), lambda i: (l, 0, 0), pipeline_mode=const),
                      pl.BlockSpec((None, 1, D), lambda i: (l, 0, 0)),
                      pl.BlockSpec((None, 1, D), lambda i: (l, 0, 0))]
        mix_args += [w_out, g1, b1]
    return pl.pallas_call(
        functools.partial(_ffn_kernel, j=j, tf=tf, alpha=alpha, nlat=nlat, mode=mode),
        grid=(ntiles,),
        in_specs=x_specs + [
            pl.BlockSpec((None, None, N_MOD, D), lambda i: (l, row_idx(i), 0, 0))] + mix_specs + [
            pl.BlockSpec((None, D, Fd), lambda i: (l, 0, 0), pipeline_mode=const),
            pl.BlockSpec((None, D, Fd), lambda i: (l, 0, 0), pipeline_mode=const),
            pl.BlockSpec((None, Fd, D), lambda i: (l, 0, 0), pipeline_mode=const),
            pl.BlockSpec((None, 1, D), lambda i: (l, 0, 0)),
            pl.BlockSpec((None, 1, D), lambda i: (l, 0, 0))],
        out_specs=pl.BlockSpec((tm, D), lambda i: (i, 0)),
        out_shape=jax.ShapeDtypeStruct((ntiles * tm, D), F32),
        scratch_shapes=[pltpu.VMEM((tm, D), BF16), pltpu.VMEM((tm, Fd), BF16)],
        compiler_params=_cparams(("parallel",)),
        name=f"ffn{j}",
    )(*xs, mods_l, *mix_args, wg, wu, wd, ln_g, ln_b)


def _inproj_kernel(x_ref, xp_ref, xn_ref, m_ref, w_ref, cw_ref, wg_ref, vec_ref,
                   q_ref, k_ref, v_ref, xr_ref, gg_ref, f_ref, af_ref, bf_ref, ab_ref, bb_ref,
                   h_ref, xl_ref, gates_ref, *, na_w, lru_w, tiles_per_batch):
    m = m_ref[...]
    modulate = lambda xs: (xs * (1.0 + m[4:5]) + m[3:4]).astype(BF16)
    h_ref[...] = modulate(x_ref[...])

    def proj(lo, width, hs=None):
        return jnp.dot(h_ref[...] if hs is None else hs, w_ref[:, lo:lo + width], preferred_element_type=F32)

    xr_lo = 3 * na_w
    xr = proj(xr_lo, lru_w)
    xr_ref[...] = xr
    pos = pl.program_id(0) % tiles_per_batch
    xrp = jnp.where(pos == 0, 0.0, proj(xr_lo, lru_w, modulate(xp_ref[...])))
    xrn = jnp.where(pos == tiles_per_batch - 1, 0.0, proj(xr_lo, lru_w, modulate(xn_ref[...])))
    gr = proj(3 * na_w + lru_w, lru_w)
    q_ref[...] = (proj(0, na_w) * (HEAD_DIM ** -0.5 * LOG2E)).astype(BF16)
    xl_ref[...] = _conv4(xrp, xr, xrn, cw_ref[...])
    gg_ref[...] = jax.nn.gelu(gr).astype(BF16)
    xlb = xl_ref[...].astype(BF16)
    gates = jnp.dot(xlb, wg_ref[:, 0:2 * lru_w], preferred_element_type=F32)
    k_ref[...] = proj(na_w, na_w).astype(BF16)
    a, b = _lru_coeffs(xl_ref[...], gates, vec_ref[0])
    af_ref[...] = a
    bf_ref[...] = b.astype(BF16)
    gates = jnp.dot(xlb, wg_ref[:, 2 * lru_w:4 * lru_w], preferred_element_type=F32)
    v_ref[...] = proj(2 * na_w, na_w).astype(BF16)
    a, b = _lru_coeffs(xl_ref[...], gates, vec_ref[1])
    ab_ref[...] = a
    bb_ref[...] = b.astype(BF16)
    f_ref[...] = proj(3 * na_w + 2 * lru_w, w_ref.shape[1] - 3 * na_w - 2 * lru_w)


def _inproj(x_all, mods_l, w_in, lru_cw, lru_gate_w, lru_vec, *, l, tm, tiles_per_batch, nbatch, na_w, lru_w):
    T, D = x_all.shape
    cols = w_in.shape[2]
    fno_w = cols - 3 * na_w - 2 * lru_w
    row_idx = lambda i: jnp.minimum(i // tiles_per_batch, nbatch)
    nblk8 = T // SUBLANES
    widths = ([(na_w, BF16)] * 3 + [(lru_w, F32), (lru_w, BF16), (fno_w, F32)] + [(lru_w, F32), (lru_w, BF16)] * 2)
    const = pl.Buffered(1)
    return pl.pallas_call(
        functools.partial(_inproj_kernel, na_w=na_w, lru_w=lru_w, tiles_per_batch=tiles_per_batch),
        grid=(T // tm,),
        in_specs=[pl.BlockSpec((tm, D), lambda i: (i, 0)),
                  pl.BlockSpec((SUBLANES, D), lambda i: (jnp.maximum(i * (tm // SUBLANES) - 1, 0), 0)),
                  pl.BlockSpec((SUBLANES, D), lambda i: (jnp.minimum((i + 1) * (tm // SUBLANES), nblk8 - 1), 0)),
                  pl.BlockSpec((None, None, N_MOD, D), lambda i: (l, row_idx(i), 0, 0)),
                  pl.BlockSpec((None, D, cols), lambda i: (l, 0, 0), pipeline_mode=const),
                  pl.BlockSpec((None, SUBLANES, lru_w), lambda i: (l, 0, 0)),
                  pl.BlockSpec((None, lru_w, 4 * lru_w), lambda i: (l, 0, 0), pipeline_mode=const),
                  pl.BlockSpec((None, 2, SUBLANES, lru_w), lambda i: (l, 0, 0, 0))],
        out_specs=[pl.BlockSpec((tm, w), lambda i: (i, 0)) for w, _ in widths],
        out_shape=[jax.ShapeDtypeStruct((T, w), dt) for w, dt in widths],
        scratch_shapes=[pltpu.VMEM((tm, D), BF16), pltpu.VMEM((tm, lru_w), F32), pltpu.VMEM((tm, 4 * lru_w), F32)],
        compiler_params=_cparams(("parallel",)),
        name="inproj",
    )(x_all, x_all, x_all, mods_l, w_in, lru_cw, lru_gate_w, lru_vec)


def _na_bias_tables(rpb):
    L, H = rpb.shape[:2]
    W, ndr, ndc = GRID_W, 2 * NA_KH - 1, 2 * NA_KW - 1
    c = np.arange(W)[:, None]
    j = np.arange(W)[None, :]
    col0 = np.clip(c - NA_KW // 2, 0, W - NA_KW)
    valid = (j >= col0) & (j < col0 + NA_KW)
    entry = np.where(valid, np.clip(j - c + (NA_KW - 1), 0, ndc - 1), ndc)
    onehot = (np.arange(ndc + 1)[:, None, None] == entry[None]).astype(np.float32)
    ext = jnp.concatenate([rpb * LOG2E, jnp.full((L, H, ndr, 1), MASK_BIAS, F32)], axis=-1)
    bt = jnp.einsum('lhde,ecj->lhcdj', ext, onehot, precision=HIGHEST).reshape(L, H, W, ndr * W)
    wb = jnp.stack([bt[..., (NA_KH - 1 - case) * W:(2 * NA_KH - 1 - case) * W] for case in range(NA_KH)], axis=2)
    return wb.reshape(L, H // 2, 2, NA_KH, W, NA_KH * W)


def _masked_heads(q):
    lane = lax.broadcasted_iota(jnp.int32, q.shape, 1)
    qf = q.astype(F32)
    return (jnp.where(lane < HEAD_DIM, qf, 0.0).astype(BF16),
            jnp.where(lane >= HEAD_DIM, qf, 0.0).astype(BF16))


def _qk(q, k):
    return lax.dot_general(q, k, (((1,), (1,)), ((), ())), preferred_element_type=F32)


def _attn_kernel(q_ref, k_ref, v_ref, kc_ref, vc_ref, wb_ref, o_ref, *, R, G, rows):
    jb = pl.program_id(2)
    kc = kc_ref[...]
    vc = vc_ref[...]
    W = GRID_W
    nwin = NA_KH * W
    lane = lax.broadcasted_iota(jnp.int32, (W, 2 * HEAD_DIM), 1)

    def group(g, carry):
        qoff = pl.multiple_of(g * (R * W), R * W)
        qcat = []
        for rr in range(R):
            qe, qo = _masked_heads(q_ref[pl.ds(qoff + rr * W, W), :])
            qcat += [qe, qo]
        s_ctx_all = _qk(jnp.concatenate(qcat, axis=0), kc)
        o_loc, p_ctx, dens = [], [], []
        for rr in range(R):
            r = (jb * G + g) * R + rr
            rs = jnp.clip(r - NA_KH // 2, 0, rows - NA_KH)
            koff = pl.multiple_of(rs * W, W)
            kw = k_ref[pl.ds(koff, nwin), :]
            vw = v_ref[pl.ds(koff, nwin), :]
            qc = jnp.concatenate(qcat[2 * rr:2 * rr + 2], axis=0)
            s_loc = _qk(qc, kw) + jnp.concatenate([wb_ref[0, r - rs], wb_ref[1, r - rs]], axis=0)
            s_ctx = s_ctx_all[rr * 2 * W:(rr + 1) * 2 * W]
            m = jnp.maximum(jnp.max(s_loc, axis=-1, keepdims=True), jnp.max(s_ctx, axis=-1, keepdims=True))
            p_loc = jnp.exp2(s_loc - m)
            pc = jnp.exp2(s_ctx - m)
            dens.append(jnp.sum(p_loc, axis=-1, keepdims=True) + jnp.sum(pc, axis=-1, keepdims=True))
            o_loc.append(jnp.dot(p_loc.astype(BF16), vw, preferred_element_type=F32))
            p_ctx.append(pc.astype(BF16))
        o_ctx_all = jnp.dot(jnp.concatenate(p_ctx, axis=0), vc, preferred_element_type=F32)
        outs = []
        for rr in range(R):
            o = (o_loc[rr] + o_ctx_all[rr * 2 * W:(rr + 1) * 2 * W]) / dens[rr]
            outs.append(jnp.where(lane < HEAD_DIM, o[:W], o[W:]).astype(BF16))
        o_ref[pl.ds(qoff, R * W), :] = jnp.concatenate(outs, axis=0)
        return carry

    lax.fori_loop(0, G, group, 0, unroll=True)


def _attn_lat(q, k, v, wb, *, l, B, S, CTX):
    T, na_w = q.shape
    npair = na_w // (2 * HEAD_DIM)
    rows = S // GRID_W
    R = ATTN_ROWS_PER_GROUP
    G = ATTN_GROUPS_PER_STEP if rows % (ATTN_GROUPS_PER_STEP * R) == 0 else 1
    nrb = rows // (R * G)
    pw = 2 * HEAD_DIM
    ctx0 = (B * S) // CTX
    qblk = R * G * GRID_W
    return pl.pallas_call(
        functools.partial(_attn_kernel, R=R, G=G, rows=rows),
        grid=(B, npair, nrb),
        in_specs=[pl.BlockSpec((qblk, pw), lambda b, p, j: (b * nrb + j, p)),
                  pl.BlockSpec((S, pw), lambda b, p, j: (b, p)),
                  pl.BlockSpec((S, pw), lambda b, p, j: (b, p)),
                  pl.BlockSpec((CTX, pw), lambda b, p, j: (ctx0 + b, p)),
                  pl.BlockSpec((CTX, pw), lambda b, p, j: (ctx0 + b, p)),
                  pl.BlockSpec((None, None, 2, NA_KH, GRID_W, NA_KH * GRID_W), lambda b, p, j: (l, p, 0, 0, 0, 0))],
        out_specs=pl.BlockSpec((qblk, pw), lambda b, p, j: (b * nrb + j, p)),
        out_shape=jax.ShapeDtypeStruct((B * S, na_w), BF16),
        compiler_params=_cparams(("parallel", "parallel", "parallel")),
        name="attn_lat",
    )(q, k, v, k, v, wb)


def _attn_ctx_kernel(q_ref, k_ref, v_ref, o_ref):
    pw = 2 * HEAD_DIM
    lane = lax.broadcasted_iota(jnp.int32, (o_ref.shape[0], pw), 1)
    pair_outs = []
    for p0 in range(0, o_ref.shape[1], pw):
        k = k_ref[:, p0:p0 + pw]
        v = v_ref[:, p0:p0 + pw]
        outs = []
        for qm in _masked_heads(q_ref[:, p0:p0 + pw]):
            s = _qk(qm, k)
            p = jnp.exp2(s - jnp.max(s, axis=-1, keepdims=True))
            den = jnp.sum(p, axis=-1, keepdims=True)
            outs.append(jnp.dot(p.astype(BF16), v, preferred_element_type=F32) / den)
        pair_outs.append(jnp.where(lane < HEAD_DIM, outs[0], outs[1]).astype(BF16))
    o_ref[...] = jnp.concatenate(pair_outs, axis=1)


def _attn_ctx(q, k, v, *, B, S, CTX):
    T, na_w = q.shape
    ctx0 = (B * S) // CTX
    spec = pl.BlockSpec((CTX, na_w), lambda b: (ctx0 + b, 0))
    return pl.pallas_call(
        _attn_ctx_kernel,
        grid=(B,),
        in_specs=[spec, spec, spec],
        out_specs=pl.BlockSpec((CTX, na_w), lambda b: (b, 0)),
        out_shape=jax.ShapeDtypeStruct((B * CTX, na_w), BF16),
        compiler_params=_cparams(("parallel",)),
        name="attn_ctx",
    )(q, k, v)


def _conv4(x_prev, x_main, x_next, cw):
    n = x_main.shape[0]
    xe = jnp.concatenate([x_prev, x_main, x_next], axis=0)
    ne = n + 2 * SUBLANES
    y = cw[2:3] * x_main
    y = y + cw[0:1] * pltpu.roll(xe, 2, 0)[SUBLANES:SUBLANES + n]
    y = y + cw[1:2] * pltpu.roll(xe, 1, 0)[SUBLANES:SUBLANES + n]
    y = y + cw[3:4] * pltpu.roll(xe, ne - 1, 0)[SUBLANES:SUBLANES + n]
    return y + cw[4:5]


def _softplus(x):
    return jnp.maximum(x, 0.0) + jnp.log1p(jnp.exp(-jnp.abs(x)))


def _lru_coeffs(xl, gates, vec):
    C = xl.shape[1]
    r = jax.nn.sigmoid(gates[:, :C] + vec[0:1])
    i = jax.nn.sigmoid(gates[:, C:] + vec[1:2])
    log_a = (-LRU_C * r) * _softplus(-vec[2:3])
    a = jnp.exp(log_a)
    b = jnp.sqrt(-jnp.tanh(log_a) * (1.0 + a * a)) * (i * xl)
    return a, b


def _group_scan(a, b, reverse):
    n, C = a.shape
    a = a.reshape(n // SUBLANES, SUBLANES, C)
    b = b.reshape(n // SUBLANES, SUBLANES, C)
    row = lax.broadcasted_iota(jnp.int32, a.shape, 1)
    for k in (1, 2, 4):
        shift = SUBLANES - k if reverse else k
        mask = (row < SUBLANES - k) if reverse else (row >= k)
        a_s = pltpu.roll(a, shift, 1)
        b_s = pltpu.roll(b, shift, 1)
        b = jnp.where(mask, a * b_s + b, b)
        a = jnp.where(mask, a * a_s, a)
    return a.reshape(n, C), b.reshape(n, C)


def _chunk_scan(a, b, h_in, reverse, acum_ref, bcum_ref, out_ref):
    n, C = a.shape
    ac, bc = _group_scan(a, b, reverse)
    acum_ref[...] = ac
    bcum_ref[...] = bc
    ng = n // SUBLANES

    def body(g, hb):
        idx = (ng - 1 - g) if reverse else g
        sl = pl.ds(pl.multiple_of(idx * SUBLANES, SUBLANES), SUBLANES)
        h = acum_ref[sl, :] * hb + bcum_ref[sl, :]
        out_ref[sl, :] = h
        edge = h[0:1] if reverse else h[SUBLANES - 1:SUBLANES]
        return jnp.broadcast_to(edge, (SUBLANES, C))

    return lax.fori_loop(0, ng, body, h_in, unroll=8)


def _lru_ctx_kernel(xr_ref, gg_ref, cw_ref, w_ref, vec_ref, o_ref, hend_ref, acum_ref, bcum_ref, hf_ref, hb_ref):
    xr = xr_ref[...]
    zeros8 = jnp.zeros((SUBLANES, xr.shape[1]), F32)
    xl = _conv4(zeros8, xr, zeros8, cw_ref[...])
    gates = jnp.dot(xl.astype(BF16), w_ref[...], preferred_element_type=F32)
    half = gates.shape[1] // 2
    a, b = _lru_coeffs(xl, gates[:, :half], vec_ref[0])
    hf_end = _chunk_scan(a, b, zeros8, False, acum_ref, bcum_ref, hf_ref)
    a, b = _lru_coeffs(xl, gates[:, half:], vec_ref[1])
    hb_end = _chunk_scan(a, b, zeros8, True, acum_ref, bcum_ref, hb_ref)
    o_ref[...] = ((hf_ref[...] + hb_ref[...]) * gg_ref[...]).astype(BF16)
    row = lax.broadcasted_iota(jnp.int32, hf_end.shape, 0)
    hend_ref[...] = jnp.where(row == 0, hf_end, hb_end)


def _lru_ctx(xr, gg, cw, w, vec, *, l, B, S, CTX):
    C = xr.shape[1]
    ctx0 = (B * S) // CTX
    spec = pl.BlockSpec((CTX, C), lambda b: (ctx0 + b, 0))
    return pl.pallas_call(
        _lru_ctx_kernel,
        grid=(B,),
        in_specs=[spec, spec,
                  pl.BlockSpec((None, SUBLANES, C), lambda b: (l, 0, 0)),
                  pl.BlockSpec((None, C, 4 * C), lambda b: (l, 0, 0)),
                  pl.BlockSpec((None, 2, SUBLANES, C), lambda b: (l, 0, 0, 0))],
        out_specs=[pl.BlockSpec((CTX, C), lambda b: (b, 0)),
                   pl.BlockSpec((None, SUBLANES, C), lambda b: (b, 0, 0))],
        out_shape=[jax.ShapeDtypeStruct((B * CTX, C), BF16),
                   jax.ShapeDtypeStruct((B, SUBLANES, C), F32)],
        scratch_shapes=[pltpu.VMEM((CTX, C), F32)] * 4,
        compiler_params=_cparams(("parallel",)),
        name="lru_ctx",
    )(xr, gg, cw, w, vec)


LRU_RING = 3


def _lru_lat_kernel(*refs, reverse, nch, nstream):
    hbm = refs[:nstream]
    hend_ref, o_ref, carry_ref, acum_ref, bcum_ref, hs_ref = refs[nstream:nstream + 6]
    bufs = refs[nstream + 6:2 * nstream + 6]
    sem = refs[2 * nstream + 6]
    tt, C = o_ref.shape
    step = pl.program_id(0) * nch + pl.program_id(1)
    total = pl.num_programs(0) * nch

    def copies(st, slot):
        ci = st % nch
        chunk = (nch - 1 - ci) if reverse else ci
        row = pl.multiple_of(((st // nch) * nch + chunk) * tt, tt)
        return [pltpu.make_async_copy(h.at[pl.ds(row, tt), :], buf.at[slot], sem.at[k, slot])
                for k, (h, buf) in enumerate(zip(hbm, bufs))]

    @pl.when(step == 0)
    def _():
        for st in range(LRU_RING - 1):
            for cp in copies(st, st):
                cp.start()

    slot = step % LRU_RING
    for cp in copies(step, slot):
        cp.wait()
    ahead = step + (LRU_RING - 1)

    @pl.when(ahead < total)
    def _():
        for cp in copies(ahead, ahead % LRU_RING):
            cp.start()

    @pl.when(pl.program_id(1) == 0)
    def _():
        row = hend_ref[1:2, :] if reverse else hend_ref[0:1, :]
        carry_ref[...] = jnp.broadcast_to(row, (SUBLANES, C))

    a = bufs[0][slot]
    b = bufs[1][slot].astype(F32)
    carry_ref[...] = _chunk_scan(a, b, carry_ref[...], reverse, acum_ref, bcum_ref, hs_ref)
    if reverse:
        o_ref[...] = hs_ref[...].astype(BF16)
    else:
        o_ref[...] = ((hs_ref[...] + bufs[3][slot].astype(F32)) * bufs[2][slot]).astype(BF16)


def _lru_lat(a, b, gg, hb, hend, *, reverse, B, S, tt):
    C = a.shape[1]
    nch = S // tt
    assert B * nch >= LRU_RING - 1
    chunk = (lambda i: nch - 1 - i) if reverse else (lambda i: i)
    streamed = [a, b] if reverse else [a, b, gg, hb]
    return pl.pallas_call(
        functools.partial(_lru_lat_kernel, reverse=reverse, nch=nch, nstream=len(streamed)),
        grid=(B, nch),
        in_specs=[pl.BlockSpec(memory_space=pl.ANY)] * len(streamed)
        + [pl.BlockSpec((None, SUBLANES, C), lambda b_, i: (b_, 0, 0))],
        out_specs=pl.BlockSpec((tt, C), lambda b_, i: (b_ * nch + chunk(i), 0)),
        out_shape=jax.ShapeDtypeStruct((B * S, C), BF16),
        scratch_shapes=[pltpu.VMEM((SUBLANES, C), F32)] + [pltpu.VMEM((tt, C), F32)] * 3
        + [pltpu.VMEM((LRU_RING, tt, C), x.dtype) for x in streamed]
        + [pltpu.SemaphoreType.DMA((len(streamed), LRU_RING))],
        compiler_params=_cparams(("arbitrary", "arbitrary")),
        name="lru_bwd" if reverse else "lru_fwd",
    )(*streamed, hend)


def _fno_w_kernel(fw_ref, cs_ref, o_ref):
    fw = fw_ref[...]
    o_ref[0] = jnp.dot(cs_ref[0], fw, preferred_element_type=F32, precision=HIGHEST).astype(BF16)
    o_ref[1] = jnp.dot(cs_ref[1], fw, preferred_element_type=F32, precision=HIGHEST).astype(BF16)


def _fno_weights(fw_bd, cs_bd):
    L, C, _ = fw_bd.shape
    return pl.pallas_call(
        _fno_w_kernel,
        grid=(L,),
        in_specs=[pl.BlockSpec((None, C, C), lambda l: (l, 0, 0)),
                  pl.BlockSpec((2, C, C), lambda l: (0, 0, 0))],
        out_specs=pl.BlockSpec((None, 2, C, C), lambda l: (l, 0, 0, 0)),
        out_shape=jax.ShapeDtypeStruct((L, 2, C, C), BF16),
        compiler_params=_cparams(("parallel",)),
        name="fno_w",
    )(fw_bd, cs_bd)


def _fno_a_kernel(x_ref, w_ref, m_ref, o_ref, *, nb):
    n1 = x_ref.shape[0] // nb
    _, s, C = x_ref.shape
    uvs = []
    for bb in range(nb):
        x = x_ref[bb * n1:(bb + 1) * n1].reshape(n1 * s, C).astype(BF16)
        u = jnp.dot(x, w_ref[0], preferred_element_type=F32).astype(BF16)
        v = jnp.dot(x, w_ref[1], preferred_element_type=F32).astype(BF16)
        uvs.append(jnp.concatenate([u, v], axis=0))
    uv = jnp.concatenate(uvs, axis=1)
    a = jnp.dot(m_ref[...], uv, preferred_element_type=F32).astype(BF16)
    rows = 2 * s
    for bb in range(nb):
        for k1 in range(n1):
            o_ref[bb, :, k1 * C:(k1 + 1) * C] = a[k1 * rows:(k1 + 1) * rows, bb * C:(bb + 1) * C]


def _fno_c_kernel(a_ref, f_ref, o_ref, *, scale):
    res = jnp.dot(f_ref[...], a_ref[...], preferred_element_type=F32) * scale
    o_ref[...] = res.reshape(o_ref.shape).astype(BF16)


def _fno_lat(f, w_l, m_a, f_c, *, B, S):
    T, C = f.shape
    n1 = FNO_N1
    n2 = S // n1
    nj = n2 // SUBLANES
    nb = FNO_BATCHES_PER_STEP if B % FNO_BATCHES_PER_STEP == 0 else 1
    f4 = f.reshape(T // n2, nj, SUBLANES, C)
    rows_a = 2 * SUBLANES * n1
    a2 = pl.pallas_call(
        functools.partial(_fno_a_kernel, nb=nb),
        grid=(nj, B // nb),
        in_specs=[pl.BlockSpec((nb * n1, None, SUBLANES, C), lambda j, b: (b, j, 0, 0)),
                  pl.BlockSpec((2, C, C), lambda j, b: (0, 0, 0)),
                  pl.BlockSpec((None, rows_a, rows_a), lambda j, b: (j, 0, 0))],
        out_specs=pl.BlockSpec((nb, 2 * SUBLANES, n1 * C), lambda j, b: (b, j, 0)),
        out_shape=jax.ShapeDtypeStruct((B, 2 * n2, n1 * C), BF16),
        compiler_params=_cparams(("parallel", "parallel")),
        name="fno_a",
    )(f4, w_l, m_a)
    ncol = n1 * C
    tc = min(ncol, FNO_C_COLS)
    scale = 1.0 / math.sqrt(S * (C // FNO_GROUPS))
    out = pl.pallas_call(
        functools.partial(_fno_c_kernel, scale=scale),
        grid=(B, ncol // tc),
        in_specs=[pl.BlockSpec((None, 2 * n2, tc), lambda b, t: (b, 0, t)),
                  pl.BlockSpec((n2, 2 * n2), lambda b, t: (0, 0))],
        out_specs=pl.BlockSpec((None, n2, tc // C, C), lambda b, t: (b, 0, t, 0)),
        out_shape=jax.ShapeDtypeStruct((B, n2, n1, C), BF16),
        compiler_params=_cparams(("parallel", "parallel")),
        name="fno_c",
    )(a2, f_c)
    return out.reshape(B * S, C)


def _fno_ctx_kernel(x_ref, w_ref, f_ref, o_ref, *, scale):
    x = x_ref[...].astype(BF16)
    u = jnp.dot(x, w_ref[0], preferred_element_type=F32).astype(BF16)
    v = jnp.dot(x, w_ref[1], preferred_element_type=F32).astype(BF16)
    uv = jnp.concatenate([u, v], axis=0)
    o_ref[...] = (jnp.dot(f_ref[...], uv, preferred_element_type=F32) * scale).astype(BF16)


def _fno_ctx(f, w_l, f_ctx, *, B, S, CTX):
    C = f.shape[1]
    ctx0 = (B * S) // CTX
    scale = 1.0 / math.sqrt(CTX * (C // FNO_GROUPS))
    return pl.pallas_call(
        functools.partial(_fno_ctx_kernel, scale=scale),
        grid=(B,),
        in_specs=[pl.BlockSpec((CTX, C), lambda b: (ctx0 + b, 0)),
                  pl.BlockSpec((2, C, C), lambda b: (0, 0, 0)),
                  pl.BlockSpec((CTX, 2 * CTX), lambda b: (0, 0))],
        out_specs=pl.BlockSpec((CTX, C), lambda b: (b, 0)),
        out_shape=jax.ShapeDtypeStruct((B * CTX, C), BF16),
        compiler_params=_cparams(("parallel",)),
        name="fno_ctx",
    )(f, w_l, f_ctx)


def _dft_constants(S, CTX, C):
    gd = C // FNO_GROUPS
    n1, n2 = FNO_N1, S // FNO_N1
    nj = n2 // SUBLANES
    idx = jnp.arange(gd, dtype=jnp.int32)
    ang = (2.0 * math.pi / gd) * ((idx[:, None] * idx[None, :]) % gd).astype(F32)
    eye_g = jnp.eye(FNO_GROUPS, dtype=F32)
    cs_bd = jnp.stack([jnp.kron(eye_g, jnp.cos(ang)), jnp.kron(eye_g, jnp.sin(ang))])
    k1 = jnp.arange(n1, dtype=jnp.int32)[:, None, None]
    nn1 = jnp.arange(n1, dtype=jnp.int32)[None, :, None]
    nn2 = jnp.arange(n2, dtype=jnp.int32)[None, None, :]
    ph = (2.0 * math.pi / S) * ((k1 * (n2 * nn1 + nn2)) % S).astype(F32)
    cph = jnp.cos(ph).reshape(n1, n1, nj, SUBLANES).transpose(2, 0, 3, 1)
    sph = jnp.sin(ph).reshape(n1, n1, nj, SUBLANES).transpose(2, 0, 3, 1)
    t4 = jnp.stack([jnp.stack([cph, -sph], axis=3), jnp.stack([-sph, -cph], axis=3)], axis=2)
    nr, nq = 2 * SUBLANES * n1, 2 * n1
    expand = (jnp.arange(nq * SUBLANES, dtype=jnp.int32)[None, :] // SUBLANES
              == jnp.arange(nq, dtype=jnp.int32)[:, None]).astype(BF16)
    m_a = jnp.einsum('jrq,qc->jrc', t4.reshape(nj, nr, nq).astype(BF16), expand, preferred_element_type=F32)
    same_s = (jnp.arange(nr, dtype=jnp.int32)[:, None] % SUBLANES
              == jnp.arange(nq * SUBLANES, dtype=jnp.int32)[None, :] % SUBLANES)
    m_a = jnp.where(same_s[None], m_a, 0.0).astype(BF16)
    i2 = jnp.arange(n2, dtype=jnp.int32)
    a2 = (2.0 * math.pi / n2) * ((i2[:, None] * i2[None, :]) % n2).astype(F32)
    f_c = jnp.stack([jnp.cos(a2).reshape(n2, nj, SUBLANES), jnp.sin(a2).reshape(n2, nj, SUBLANES)], axis=2)
    f_c = f_c.reshape(n2, 2 * n2).astype(BF16)
    ic = jnp.arange(CTX, dtype=jnp.int32)
    ac = (2.0 * math.pi / CTX) * ((ic[:, None] * ic[None, :]) % CTX).astype(F32)
    f_ctx = jnp.concatenate([jnp.cos(ac), -jnp.sin(ac)], axis=1).astype(BF16)
    return cs_bd, m_a, f_c, f_ctx


def _block_diag(w):
    G, n = w.shape[-3], w.shape[-1]
    eye = jnp.eye(G, dtype=w.dtype)
    out = jnp.einsum('...gij,gh->...gihj', w, eye)
    return out.reshape(w.shape[:-3] + (G * n, G * n))


def kernel(x, c, ctx, c_ctx, w_ada, b_ada, ln_g, ln_b, ff1_gate, ff1_up, ff1_down, ff2_gate, ff2_up, ff2_down,
           w_in, w_out, na_rpb, lru_conv_w, lru_conv_b, lru_wa, lru_ba, lru_wx, lru_bx, lru_lambda, fno_w):
    B, S, D = x.shape
    CTX = ctx.shape[1]
    L = w_ada.shape[0]
    na_w = na_rpb.shape[1] * HEAD_DIM
    lru_w = lru_conv_w.shape[2]
    alpha = float((2 * L) ** 0.25)
    tm = min(TOKEN_TILE, B * CTX)
    tiles_per_batch = S // tm
    nlat = (B * S) // tm
    ntiles = nlat + (B * CTX) // tm
    assert B < 16 and S % tm == 0 and (B * CTX) % tm == 0 and S % (GRID_W * NA_KH) == 0

    cc = jnp.zeros((16, D), F32).at[:B].set(c).at[B].set(c_ctx)
    mods = _ada_params(cc, w_ada, b_ada)

    bf = lambda w: w.astype(BF16)
    ff1 = (bf(ff1_gate), bf(ff1_up), bf(ff1_down))
    ff2 = (bf(ff2_gate), bf(ff2_up), bf(ff2_down))
    w_in_b, w_out_b = bf(w_in), bf(w_out)
    ln_g4 = ln_g[:, :, None, :]
    ln_b4 = ln_b[:, :, None, :]
    wb = _na_bias_tables(na_rpb)
    gate_w = jnp.concatenate([_block_diag(lru_wa), _block_diag(lru_wx)], axis=-1)
    lru_gate_w = bf(jnp.concatenate([gate_w[:, 0], gate_w[:, 1]], axis=-1))
    zrow = jnp.zeros((L, 2, SUBLANES - 3, lru_w), F32)
    lru_vec = jnp.concatenate([lru_ba[:, :, None], lru_bx[:, :, None], lru_lambda[:, :, None], zrow], axis=2)
    lru_cw = jnp.concatenate([lru_conv_w, lru_conv_b[:, None], jnp.zeros((L, SUBLANES - LRU_CONV - 1, lru_w), F32)], axis=1)
    cs_bd, m_a, f_c, f_ctx = _dft_constants(S, CTX, D - na_w - lru_w)
    fno_wcs = _fno_weights(_block_diag(fno_w), cs_bd)

    common = dict(tm=tm, tiles_per_batch=tiles_per_batch, nbatch=B)
    xs = (x.reshape(B * S, D), ctx.reshape(B * CTX, D))
    for l in range(L):
        last = l == L - 1
        x1 = _ffn(xs, mods, *ff1, ln_g4[:, 0], ln_b4[:, 0], j=0, l=l, alpha=alpha, nlat=nlat, ntiles=ntiles, **common)
        q, k, v, xr, gg, f, a_f, b_f, a_b, b_b = _inproj(x1, mods, w_in_b, lru_cw, lru_gate_w, lru_vec,
                                                        l=l, na_w=na_w, lru_w=lru_w, **common)
        na_lat = _attn_lat(q, k, v, wb, l=l, B=B, S=S, CTX=CTX)
        lru_ctx, hend = _lru_ctx(xr, gg, lru_cw, lru_gate_w, lru_vec, l=l, B=B, S=S, CTX=CTX)
        tt = min(S, LRU_CHUNK)
        hb = _lru_lat(a_b, b_b, None, None, hend, reverse=True, B=B, S=S, tt=tt)
        lru_lat = _lru_lat(a_f, b_f, gg, hb, hend, reverse=False, B=B, S=S, tt=tt)
        fno_lat = _fno_lat(f, fno_wcs[l], m_a, f_c, B=B, S=S)
        if last:
            ctx_parts, nt = None, nlat
        else:
            ctx_parts = (_attn_ctx(q, k, v, B=B, S=S, CTX=CTX), lru_ctx, _fno_ctx(f, fno_wcs[l], f_ctx, B=B, S=S, CTX=CTX))
            nt = ntiles
        x3 = _ffn((x1,), mods, *ff2, ln_g4[:, 2], ln_b4[:, 2], j=2, l=l, alpha=alpha, nlat=nlat, ntiles=nt,
                  mix=((na_lat, lru_lat, fno_lat), ctx_parts, w_out_b, ln_g4[:, 1], ln_b4[:, 1]), **common)
        xs = (x3,)
    return xs[0].reshape(B, S, D)
```
